```python
import math
import jax, jax.numpy as jnp
from jax import lax
import numpy as np

D_MODEL = 4096
BATCH = 2
SEQ = 4096
DEPTH = 2

GRID_W = 64
CTX_LEN = 256

CONV_C = D_MODEL // 4
CONV_K = 31
RET_HEADS = 8
RET_DK = 256
RET_DV = 256
RET_QK = RET_HEADS * RET_DK
RET_V = RET_HEADS * RET_DV
RET_CHUNK = 128
ROPE_BASE = 10000.0
S5_C = D_MODEL // 4
S5_P = 16
S5_G = S5_C // S5_P
S5_N = 64
S5_DT_MIN = 1e-3
S5_DT_MAX = 1e-1
N_BRANCH = 3
IN_SPLITS = (2 * CONV_C,
             2 * CONV_C + RET_QK,
             2 * CONV_C + 2 * RET_QK,
             2 * CONV_C + 2 * RET_QK + RET_V,
             2 * CONV_C + 2 * RET_QK + 2 * RET_V,
             2 * CONV_C + 2 * RET_QK + 2 * RET_V + S5_C)
D_IN = 2 * CONV_C + 2 * RET_QK + 2 * RET_V + S5_C + N_BRANCH * D_MODEL
N_EXPERTS = 64
EXPERT_FF = D_MODEL // 16
TOP_K = 8
N_GROUPS = 8
EXPERTS_PER_GROUP = N_EXPERTS // N_GROUPS
TOPK_GROUPS = 4
ROUTED_SCALE = 2.5
SHARED_FF = D_MODEL // 4
LN_EPS = 1e-5
HEAD_NORM_EPS = 1e-5
NEG_BIG = -1e30
DEEPNORM_ALPHA = (2.0 * DEPTH) ** 0.25
DEEPNORM_BETA = (8.0 * DEPTH) ** -0.25

kernel_name = 'hybrid_conv_retention_s5_moe_dit'


def layer_norm(x, g, b):
    xf = x.astype(jnp.float32)
    mu = jnp.mean(xf, axis=-1, keepdims=True)
    var = jnp.mean(jnp.square(xf - mu), axis=-1, keepdims=True)
    return ((xf - mu) * lax.rsqrt(var + LN_EPS) * g + b).astype(x.dtype)


def modulate(x, shift, scale):
    return x * (1.0 + scale) + shift


def _rev(t, direction, axis):
    return t if direction == 0 else jnp.flip(t, axis=axis)


def conformer_conv(a, p, rows):
    b, l, _ = a.shape
    u = a[..., :CONV_C] * jax.nn.sigmoid(a[..., CONV_C:])
    seqs = u if rows is None else u.reshape(b * rows, GRID_W, CONV_C)
    y = lax.conv_general_dilated(seqs, p['conv_w'][:, None, :].astype(seqs.dtype), (1,),
                                 [(CONV_K // 2, CONV_K // 2)],
                                 dimension_numbers=('NWC', 'WIO', 'NWC'),
                                 feature_group_count=CONV_C)
    y = y.reshape(b, l, CONV_C) + p['conv_b']
    y = jax.nn.silu(layer_norm(y, p['conv_ln_g'], p['conv_ln_b']))
    return y @ p['conv_proj']


def split_heads(z, head_dim):
    b, l = z.shape[:2]
    return z.astype(jnp.float32).reshape(b, l, -1, head_dim).transpose(0, 2, 1, 3)


def rotary(t, pos):
    half = t.shape[-1] // 2
    freq = ROPE_BASE ** (-jnp.arange(half, dtype=jnp.float32) / half)
    ang = pos[:, None] * freq[None, :]
    cos, sin = jnp.cos(ang), jnp.sin(ang)
    t1, t2 = t[..., :half], t[..., half:]
    return jnp.concatenate([t1 * cos - t2 * sin, t1 * sin + t2 * cos], axis=-1)


def retention_final_state(k, v, log_g):
    l = k.shape[2]
    w = jnp.exp(log_g[:, None] * (l - 1 - jnp.arange(l, dtype=jnp.float32))[None, :])
    return jnp.einsum('bhld,hl,bhle->bhde', k, w, v)


def retention_chunked(q, k, v, log_g, s0):
    b, h, l, dk = q.shape
    dv = v.shape[-1]
    n = l // RET_CHUNK
    q = q.reshape(b, h, n, RET_CHUNK, dk)
    k = k.reshape(b, h, n, RET_CHUNK, dk)
    v = v.reshape(b, h, n, RET_CHUNK, dv)
    idx = jnp.arange(RET_CHUNK, dtype=jnp.float32)
    diff = idx[:, None] - idx[None, :]
    decay = jnp.where(diff >= 0, jnp.exp(log_g[:, None, None] * jnp.maximum(diff, 0.0)[None]), 0.0)
    scores = jnp.einsum('bhncd,bhnsd->bhncs', q, k) * decay[:, None]
    inner = jnp.einsum('bhncs,bhnse->bhnce', scores, v)
    w_in = jnp.exp(log_g[:, None] * (RET_CHUNK - 1.0 - idx)[None, :])
    kv = jnp.einsum('bhnsd,hs,bhnse->bhnde', k, w_in, v)
    g_chunk = jnp.exp(log_g * RET_CHUNK)[None, :, None, None]

    def step(s, kv_c):
        return g_chunk * s + kv_c, s

    _, s_prev = lax.scan(step, s0, jnp.moveaxis(kv, 2, 0))
    w_out = jnp.exp(log_g[:, None] * (idx + 1.0)[None, :])
    cross = jnp.einsum('bhncd,nbhde,hc->bhnce', q, s_prev, w_out)
    return (inner + cross).reshape(b, h, l, dv)


def retention_out(o, g, w):
    of = o.astype(jnp.float32)
    mu = jnp.mean(of, axis=-1, keepdims=True)
    var = jnp.mean(jnp.square(of - mu), axis=-1, keepdims=True)
    on = (of - mu) * lax.rsqrt(var + HEAD_NORM_EPS)
    b, h, l, dv = o.shape
    on = on.transpose(0, 2, 1, 3).reshape(b, l, h * dv)
    return (on * jax.nn.silu(g)) @ w


def retention_branch(zc, zx, p, pos_c, pos_x, with_ctx):
    qc_raw, kc_raw, vc_raw, gc = zc
    qx_raw, kx_raw, vx_raw, gx = zx
    log_g = jax.nn.log_sigmoid(p['ret_decay_logit'].astype(jnp.float32))
    k_scale = RET_DK ** -0.5
    kc = rotary(split_heads(kc_raw, RET_DK), pos_c) * k_scale
    vc = split_heads(vc_raw, RET_DV)
    qx = rotary(split_heads(qx_raw, RET_DK), pos_x)
    kx = rotary(split_heads(kx_raw, RET_DK), pos_x) * k_scale
    vx = split_heads(vx_raw, RET_DV)
    qc = rotary(split_heads(qc_raw, RET_DK), pos_c) if with_ctx else None
    o_x = 0.0
    o_c = 0.0
    for d in range(2):
        s_ctx = retention_final_state(_rev(kc, d, 2), _rev(vc, d, 2), log_g[d])
        o_x = o_x + _rev(retention_chunked(_rev(qx, d, 2), _rev(kx, d, 2), _rev(vx, d, 2), log_g[d], s_ctx), d, 2)
        if with_ctx:
            o_c = o_c + _rev(retention_chunked(_rev(qc, d, 2), _rev(kc, d, 2), _rev(vc, d, 2), log_g[d],
                                               jnp.zeros_like(s_ctx)), d, 2)
    y_x = retention_out(o_x, gx, p['ret_proj'])
    y_c = retention_out(o_c, gc, p['ret_proj']) if with_ctx else None
    return y_c, y_x


def s5_params(p, d):
    a = lax.complex(p['s5_a_re'][d].astype(jnp.float32), p['s5_a_im'][d].astype(jnp.float32))
    adt = a * jnp.exp(p['s5_log_dt'][d].astype(jnp.float32))[:, None]
    a_bar = jnp.exp(adt)
    bmat = lax.complex(p['s5_b_re'][d].astype(jnp.float32), p['s5_b_im'][d].astype(jnp.float32))
    b_bar = ((a_bar - 1.0) / a)[..., None] * bmat
    cmat = lax.complex(p['s5_c_re'][d].astype(jnp.float32), p['s5_c_im'][d].astype(jnp.float32))
    return adt, a_bar, b_bar, cmat


def _ssm_combine(e1, e2):
    a1, b1 = e1
    a2, b2 = e2
    return a1 * a2, a2 * b1 + b2


def s5_final_state(u, adt, b_bar):
    l = u.shape[1]
    bu = jnp.einsum('blgp,gnp->blgn', u, b_bar)
    powers = jnp.exp(adt[None] * (l - 1 - jnp.arange(l, dtype=jnp.float32))[:, None, None])
    return jnp.einsum('blgn,lgn->bgn', bu, powers)


def s5_scan(u, a_bar, b_bar, cmat, s0):
    bu = jnp.einsum('blgp,gnp->blgn', u, b_bar)
    if s0 is not None:
        bu = bu.at[:, 0].add(a_bar * s0)
    a = jnp.broadcast_to(a_bar, (1, u.shape[1]) + a_bar.shape)
    _, xs = lax.associative_scan(_ssm_combine, (a, bu), axis=1)
    return jnp.einsum('blgn,gpn->blgp', xs, cmat).real


def s5_out(y, u, p):
    y = jax.nn.gelu(y + p['s5_d'] * u)
    y = y * jax.nn.sigmoid(y @ p['s5_w_glu'])
    return y @ p['s5_proj']


def s5_branch(uc, ux, p, rows, with_ctx):
    b, l, _ = ux.shape
    lc = uc.shape[1]
    uc_f = uc.astype(jnp.float32)
    ux_f = ux.astype(jnp.float32)
    ux_cm = ux_f.reshape(b, rows, GRID_W, S5_C).transpose(0, 2, 1, 3).reshape(b, l, S5_G, S5_P)
    uc_g = uc_f.reshape(b, lc, S5_G, S5_P)
    y_x = 0.0
    y_c = 0.0
    for d in range(2):
        adt, a_bar, b_bar, cmat = s5_params(p, d)
        uc_d = _rev(uc_g, d, 1)
        s_ctx = s5_final_state(uc_d, adt, b_bar)
        y_x = y_x + _rev(s5_scan(_rev(ux_cm, d, 1), a_bar, b_bar, cmat, s_ctx), d, 1)
        if with_ctx:
            y_c = y_c + _rev(s5_scan(uc_d, a_bar, b_bar, cmat, None), d, 1)
    y_x = y_x.reshape(b, GRID_W, rows, S5_C).transpose(0, 2, 1, 3).reshape(b, l, S5_C)
    out_x = s5_out(y_x, ux_f, p)
    out_c = s5_out(y_c.reshape(b, lc, S5_C), uc_f, p) if with_ctx else None
    return out_c, out_x


def merge(ys, gate_logits, w_out):
    g = jnp.split(jax.nn.sigmoid(gate_logits), N_BRANCH, axis=-1)
    merged = g[0] * ys[0] + g[1] * ys[1] + g[2] * ys[2]
    return merged @ w_out


def token_mixer(hc, hx, p, pos, rows, with_ctx):
    lc = hc.shape[1]
    zc = hc @ p['w_in']
    zx = hx @ p['w_in']
    ac, qc, kc, vc, gc, uc, sc = jnp.split(zc, IN_SPLITS, axis=-1)
    ax, qx, kx, vx, gx, ux, sx = jnp.split(zx, IN_SPLITS, axis=-1)
    ya_x = conformer_conv(ax, p, rows)
    yb_c, yb_x = retention_branch((qc, kc, vc, gc), (qx, kx, vx, gx), p, pos[:lc], pos[lc:], with_ctx)
    yc_c, yc_x = s5_branch(uc, ux, p, rows, with_ctx)
    out_x = merge((ya_x, yb_x, yc_x), sx, p['w_out'])
    if not with_ctx:
        return None, out_x
    ya_c = conformer_conv(ac, p, None)
    out_c = merge((ya_c, yb_c, yc_c), sc, p['w_out'])
    return out_c, out_x


def moe_ffn(h, p):
    n = h.shape[0]
    scores = jax.nn.sigmoid(h.astype(jnp.float32) @ p['router_w'].astype(jnp.float32))
    sel = scores + p['router_bias'].astype(jnp.float32)
    grp = lax.top_k(sel.reshape(n, N_GROUPS, EXPERTS_PER_GROUP), 2)[0].sum(-1)
    _, g_idx = lax.top_k(grp, TOPK_GROUPS)
    g_mask = jax.nn.one_hot(g_idx, N_GROUPS, dtype=jnp.float32).sum(axis=1)
    e_mask = jnp.repeat(g_mask, EXPERTS_PER_GROUP, axis=1)
    _, e_idx = lax.top_k(jnp.where(e_mask > 0, sel, NEG_BIG), TOP_K)
    w = jnp.take_along_axis(scores, e_idx, axis=-1)
    w = ROUTED_SCALE * w / jnp.sum(w, axis=-1, keepdims=True)
    gate = jnp.sum(jax.nn.one_hot(e_idx, N_EXPERTS, dtype=jnp.float32) * w[..., None], axis=1)
    hg = jnp.einsum('nd,edf->nef', h, p['exp_w_gate'])
    hu = jnp.einsum('nd,edf->nef', h, p['exp_w_up'])
    routed = jnp.einsum('nef,efd->nd', jax.nn.silu(hg) * hu * gate[..., None].astype(h.dtype), p['exp_w_down'])
    shared = (jax.nn.silu(h @ p['sh_w_gate']) * (h @ p['sh_w_up'])) @ p['sh_w_down']
    return routed + shared


def setup_inputs(seed: int = 0) -> dict:
    key = jax.random.key(seed)
    keys = jax.random.split(key, 48)
    ctr = [0]

    def nxt():
        k = keys[ctr[0]]
        ctr[0] += 1
        return k

    def nrm(shape, scale):
        return scale * jax.random.normal(nxt(), shape, jnp.float32)

    def gain(shape):
        return 1.0 + nrm(shape, 0.01)

    L_ = DEPTH
    D = D_MODEL
    heads = jnp.arange(RET_HEADS, dtype=jnp.float32)
    ret_base = jnp.log(2.0 ** (5.0 + heads) - 1.0)
    n_idx = jnp.arange(S5_N, dtype=jnp.float32)
    return {
        'x': nrm((BATCH, SEQ, D), 1.0),
        'c': nrm((BATCH, D), 1.0),
        'ctx': nrm((BATCH, CTX_LEN, D), 1.0),
        'c_ctx': nrm((D,), 1.0),
        'emb_ln_g': gain((D,)),
        'emb_ln_b': nrm((D,), 0.01),
        'ada_w': nrm((L_, D, 6 * D), 0.5 * D ** -0.5),
        'ada_b': nrm((L_, 6 * D), 0.01),
        'w_in': nrm((L_, D, D_IN), D ** -0.5),
        'conv_w': nrm((L_, CONV_K, CONV_C), CONV_K ** -0.5),
        'conv_b': nrm((L_, CONV_C), 0.01),
        'conv_ln_g': gain((L_, CONV_C)),
        'conv_ln_b': nrm((L_, CONV_C), 0.01),
        'conv_proj': nrm((L_, CONV_C, D), DEEPNORM_BETA * CONV_C ** -0.5),
        'ret_decay_logit': ret_base + nrm((L_, 2, RET_HEADS), 0.01),
        'ret_proj': nrm((L_, RET_V, D), DEEPNORM_BETA * RET_V ** -0.5),
        's5_a_re': -0.5 + nrm((L_, 2, S5_G, S5_N), 0.01),
        's5_a_im': math.pi * n_idx + nrm((L_, 2, S5_G, S5_N), 0.01),
        's5_log_dt': jax.random.uniform(nxt(), (L_, 2, S5_G), jnp.float32, math.log(S5_DT_MIN), math.log(S5_DT_MAX)),
        's5_b_re': nrm((L_, 2, S5_G, S5_N, S5_P), (2.0 * S5_P) ** -0.5),
        's5_b_im': nrm((L_, 2, S5_G, S5_N, S5_P), (2.0 * S5_P) ** -0.5),
        's5_c_re': nrm((L_, 2, S5_G, S5_P, S5_N), (2.0 * S5_N) ** -0.5),
        's5_c_im': nrm((L_, 2, S5_G, S5_P, S5_N), (2.0 * S5_N) ** -0.5),
        's5_d': nrm((L_, S5_C), 1.0),
        's5_w_glu': nrm((L_, S5_C, S5_C), S5_C ** -0.5),
        's5_proj': nrm((L_, S5_C, D), DEEPNORM_BETA * S5_C ** -0.5),
        'w_out': nrm((L_, D, D), DEEPNORM_BETA * D ** -0.5),
        'ln1_g': gain((L_, D)),
        'ln1_b': nrm((L_, D), 0.01),
        'ln2_g': gain((L_, D)),
        'ln2_b': nrm((L_, D), 0.01),
        'router_w': nrm((L_, D, N_EXPERTS), D ** -0.5),
        'router_bias': nrm((L_, N_EXPERTS), 0.01),
        'exp_w_gate': nrm((L_, N_EXPERTS, D, EXPERT_FF), D ** -0.5),
        'exp_w_up': nrm((L_, N_EXPERTS, D, EXPERT_FF), D ** -0.5),
        'exp_w_down': nrm((L_, N_EXPERTS, EXPERT_FF, D), DEEPNORM_BETA * EXPERT_FF ** -0.5),
        'sh_w_gate': nrm((L_, D, SHARED_FF), D ** -0.5),
        'sh_w_up': nrm((L_, D, SHARED_FF), D ** -0.5),
        'sh_w_down': nrm((L_, SHARED_FF, D), DEEPNORM_BETA * SHARED_FF ** -0.5),
    }


def reference(x, c, ctx, c_ctx, emb_ln_g, emb_ln_b, ada_w, ada_b, w_in, conv_w, conv_b, conv_ln_g, conv_ln_b,
              conv_proj, ret_decay_logit, ret_proj, s5_a_re, s5_a_im, s5_log_dt, s5_b_re, s5_b_im, s5_c_re,
              s5_c_im, s5_d, s5_w_glu, s5_proj, w_out, ln1_g, ln1_b, ln2_g, ln2_b, router_w, router_bias,
              exp_w_gate, exp_w_up, exp_w_down, sh_w_gate, sh_w_up, sh_w_down):
    b, l, _ = x.shape
    rows = l // GRID_W
    lc = ctx.shape[1]
    pos = jnp.arange(lc + l, dtype=jnp.float32)
    xl = layer_norm(x, emb_ln_g, emb_ln_b)
    xc = layer_norm(ctx, emb_ln_g, emb_ln_b)
    for i in range(DEPTH):
        with_ctx = i < DEPTH - 1
        p = {'w_in': w_in[i], 'conv_w': conv_w[i], 'conv_b': conv_b[i], 'conv_ln_g': conv_ln_g[i],
             'conv_ln_b': conv_ln_b[i], 'conv_proj': conv_proj[i], 'ret_decay_logit': ret_decay_logit[i],
             'ret_proj': ret_proj[i], 's5_a_re': s5_a_re[i], 's5_a_im': s5_a_im[i], 's5_log_dt': s5_log_dt[i],
             's5_b_re': s5_b_re[i], 's5_b_im': s5_b_im[i], 's5_c_re': s5_c_re[i], 's5_c_im': s5_c_im[i],
             's5_d': s5_d[i], 's5_w_glu': s5_w_glu[i], 's5_proj': s5_proj[i], 'w_out': w_out[i],
             'router_w': router_w[i], 'router_bias': router_bias[i], 'exp_w_gate': exp_w_gate[i],
             'exp_w_up': exp_w_up[i], 'exp_w_down': exp_w_down[i], 'sh_w_gate': sh_w_gate[i],
             'sh_w_up': sh_w_up[i], 'sh_w_down': sh_w_down[i]}
        mod_x = jnp.split((jax.nn.silu(c) @ ada_w[i] + ada_b[i])[:, None, :], 6, axis=-1)
        mod_c = jnp.split(jax.nn.silu(c_ctx) @ ada_w[i] + ada_b[i], 6, axis=-1)
        hx = modulate(xl, mod_x[0], mod_x[1])
        hc = modulate(xc, mod_c[0], mod_c[1])
        yc, yx = token_mixer(hc, hx, p, pos, rows, with_ctx)
        xl = layer_norm(DEEPNORM_ALPHA * xl + mod_x[2] * yx, ln1_g[i], ln1_b[i])
        hx = modulate(xl, mod_x[3], mod_x[4])
        if with_ctx:
            xc = layer_norm(DEEPNORM_ALPHA * xc + mod_c[2] * yc, ln1_g[i], ln1_b[i])
            hc = modulate(xc, mod_c[3], mod_c[4])
            h = jnp.concatenate([hc, hx], axis=1)
            y = moe_ffn(h.reshape(-1, D_MODEL), p).reshape(h.shape)
            xc = layer_norm(DEEPNORM_ALPHA * xc + mod_c[5] * y[:, :lc], ln2_g[i], ln2_b[i])
            y = y[:, lc:]
        else:
            y = moe_ffn(hx.reshape(-1, D_MODEL), p).reshape(hx.shape)
        xl = layer_norm(DEEPNORM_ALPHA * xl + mod_x[5] * y, ln2_g[i], ln2_b[i])
    return xl
```

```python
import functools
import math

import jax
import jax.numpy as jnp
from jax import lax
from jax.experimental import pallas as pl
from jax.experimental.pallas import tpu as pltpu

F32 = jnp.float32
BF16 = jnp.bfloat16

GRID_W = 64
RET_HEADS = 8
RET_DK = 256
S5_P = 16
S5_N = 64
ROPE_BASE = 10000.0
N_GROUPS = 8
TOPK_GROUPS = 4
TOP_K = 8
ROUTED_SCALE = 2.5
LN_EPS = 1e-5
HEAD_NORM_EPS = 1e-5
NEG_BIG = -1e30
N_BRANCH = 3

ROW_TILE = 256
S5_T = 16
MOD_ROWS = 8
V7X_VMEM_LIMIT = 56 * 1024 * 1024


def _largest_row_tile(ntok, max_tiles):
    n = ntok // ROW_TILE
    k = max(t for t in range(1, max_tiles + 1) if n % t == 0)
    return k * ROW_TILE


def _cparams(sem, vmem=None):
    return pltpu.CompilerParams(dimension_semantics=sem, vmem_limit_bytes=vmem)


def _split_bf16(v):
    hi = v.astype(BF16)
    lo = (v - hi.astype(F32)).astype(BF16)
    return hi, lo


def _dot(a, b):
    return jnp.dot(a, b, preferred_element_type=F32)


def _dot_nt(a, b):
    return lax.dot_general(a, b, (((1,), (1,)), ((), ())), preferred_element_type=F32)


def _dot_tn(a, b):
    return lax.dot_general(a, b, (((0,), (0,)), ((), ())), preferred_element_type=F32)


def _sigmoid(v):
    return 1.0 / (1.0 + jnp.exp(-v))


def _silu(v):
    return v * _sigmoid(v)


def _layer_norm_rows(v, g, b, eps):
    mu = jnp.mean(v, axis=-1, keepdims=True)
    vc = v - mu
    var = jnp.mean(vc * vc, axis=-1, keepdims=True)
    return vc * lax.rsqrt(var + eps) * g + b


def _ada_kernel(c_ref, w_ref, b_ref, o_ref):
    c = _silu(c_ref[...])
    ch, cl = _split_bf16(c)
    wh, wl = _split_bf16(w_ref[...])
    o_ref[...] = _dot(ch, wh) + _dot(ch, wl) + _dot(cl, wh) + b_ref[...]


def ada_modulation(cvec, ada_w, ada_b_l, layer):
    _, d, n = ada_w.shape
    tn = 512
    return pl.pallas_call(
        _ada_kernel,
        grid=(n // tn,),
        in_specs=[
            pl.BlockSpec((MOD_ROWS, d), lambda j: (0, 0)),
            pl.BlockSpec((None, d, tn), lambda j: (layer, 0, j)),
            pl.BlockSpec((1, tn), lambda j: (0, j)),
        ],
        out_specs=pl.BlockSpec((MOD_ROWS, tn), lambda j: (0, j)),
        out_shape=jax.ShapeDtypeStruct((MOD_ROWS, n), F32),
        compiler_params=_cparams(("arbitrary",), V7X_VMEM_LIMIT),
        name="ada_modulation",
    )(cvec, ada_w, ada_b_l)


def _group_of_tile(i, n_latent_tiles, tiles_per_batch, n_batch):
    return jnp.where(i < n_latent_tiles, i // tiles_per_batch, n_batch)


def _route(h, rwt_ref, rb_ref):
    hh, hl = _split_bf16(h)
    wh, wl = _split_bf16(rwt_ref[...])
    logits = _dot_nt(wh, hh) + _dot_nt(wh, hl) + _dot_nt(wl, hh)
    scores = _sigmoid(logits)
    sel = scores + rb_ref[...]
    n_e, tm = sel.shape
    per = n_e // N_GROUPS
    shape3 = (N_GROUPS, per, tm)
    sel3 = sel.reshape(shape3)
    io_e = lax.broadcasted_iota(jnp.int32, shape3, 1)
    io_g = lax.broadcasted_iota(jnp.int32, shape3, 0)
    m1 = jnp.max(sel3, axis=1, keepdims=True)
    first = jnp.min(jnp.where(sel3 == m1, io_e, per), axis=1, keepdims=True)
    m2 = jnp.max(jnp.where(io_e == first, -jnp.inf, sel3), axis=1, keepdims=True)
    work = m1 + m2
    iog1 = lax.broadcasted_iota(jnp.int32, work.shape, 0)
    gsel = jnp.zeros(work.shape, F32)
    for _ in range(TOPK_GROUPS):
        m = jnp.max(work, axis=0, keepdims=True)
        fi = jnp.min(jnp.where(work == m, iog1, N_GROUPS), axis=0, keepdims=True)
        hit = iog1 == fi
        gsel = jnp.where(hit, 1.0, gsel)
        work = jnp.where(hit, -jnp.inf, work)
    work = jnp.where(jnp.broadcast_to(gsel, shape3) > 0.0, sel3, NEG_BIG)
    flat = io_g * per + io_e
    esel = jnp.zeros(shape3, F32)
    for _ in range(TOP_K):
        m = jnp.max(jnp.max(work, axis=1, keepdims=True), axis=0, keepdims=True)
        cand = jnp.where(work == m, flat, n_e)
        fi = jnp.min(jnp.min(cand, axis=1, keepdims=True), axis=0, keepdims=True)
        hit = flat == fi
        esel = jnp.where(hit, 1.0, esel)
        work = jnp.where(hit, -jnp.inf, work)
    w = jnp.where(esel.reshape(n_e, tm) > 0.0, scores, 0.0)
    return ROUTED_SCALE * w / jnp.sum(w, axis=0, keepdims=True)


def _resid_ln_mod_kernel(*refs, alpha, n_y, has_mod, has_router, group_fn):
    it = iter(refs)
    x_ref = next(it)
    y_refs = [next(it) for _ in range(n_y)]
    gate_ref = next(it) if n_y else None
    g_ref, b_ref = next(it), next(it)
    shift_ref = scale_ref = rwt_ref = rb_ref = None
    if has_mod:
        shift_ref, scale_ref = next(it), next(it)
    if has_router:
        rwt_ref, rb_ref = next(it), next(it)
    xl_ref = next(it)
    h_ref = next(it) if has_mod else None
    gate_out_ref = next(it) if has_router else None

    grp = group_fn(pl.program_id(0))
    v = x_ref[...]
    if n_y:
        y = y_refs[0][...]
        for r in y_refs[1:]:
            y = y + r[...]
        v = alpha * v + gate_ref[pl.ds(grp, 1), :] * y
    xl = _layer_norm_rows(v, g_ref[...], b_ref[...], LN_EPS)
    xl_ref[...] = xl
    if has_mod:
        h = xl * (1.0 + scale_ref[pl.ds(grp, 1), :]) + shift_ref[pl.ds(grp, 1), :]
        h_ref[...] = h.astype(h_ref.dtype)
        if has_router:
            gate_out_ref[...] = _route(h, rwt_ref, rb_ref)


def resid_ln_mod(x, ys, gate_mod, gate_col, ln_g, ln_b, mod, shift_col, scale_col, *,
                 alpha, group_fn, n_tiles, router=None):
    d = x.shape[1]
    n_y = len(ys)
    has_mod = mod is not None
    has_router = router is not None
    row = pl.BlockSpec((ROW_TILE, d), lambda i: (i, 0))
    vec = pl.BlockSpec((1, d), lambda i: (0, 0))
    args, specs = [x], [row]
    for y in ys:
        args.append(y)
        specs.append(row)
    if n_y:
        args.append(gate_mod)
        specs.append(pl.BlockSpec((MOD_ROWS, d), lambda i: (0, gate_col)))
    args += [ln_g, ln_b]
    specs += [vec, vec]
    if has_mod:
        args += [mod, mod]
        specs += [pl.BlockSpec((MOD_ROWS, d), lambda i: (0, shift_col)),
                  pl.BlockSpec((MOD_ROWS, d), lambda i: (0, scale_col))]
    out_shapes = [jax.ShapeDtypeStruct((n_tiles * ROW_TILE, d), F32)]
    out_specs = [row]
    if has_mod:
        out_shapes.append(jax.ShapeDtypeStruct((n_tiles * ROW_TILE, d), BF16))
        out_specs.append(row)
    if has_router:
        rwt, rb = router
        n_e = rwt.shape[0]
        args += [rwt, rb]
        specs += [pl.BlockSpec((n_e, d), lambda i: (0, 0)), pl.BlockSpec((n_e, 1), lambda i: (0, 0))]
        out_shapes.append(jax.ShapeDtypeStruct((n_e, n_tiles * ROW_TILE), F32))
        out_specs.append(pl.BlockSpec((n_e, ROW_TILE), lambda i: (0, i)))
    kern = functools.partial(_resid_ln_mod_kernel, alpha=alpha, n_y=n_y, has_mod=has_mod,
                             has_router=has_router, group_fn=group_fn)
    return pl.pallas_call(
        kern, grid=(n_tiles,), in_specs=specs, out_specs=out_specs, out_shape=out_shapes,
        compiler_params=_cparams(("arbitrary",), V7X_VMEM_LIMIT),
        name="resid_ln_mod",
    )(*args)


def _mm_kernel(x_ref, w_ref, o_ref, wbf_ref):
    @pl.when(pl.program_id(1) == 0)
    def _():
        wbf_ref[...] = w_ref[...].astype(BF16)

    o_ref[...] = _dot(x_ref[...], wbf_ref[...]).astype(o_ref.dtype)


def matmul_stacked_w(x, w, layer, out_dtype, tm, tn):
    m, k = x.shape
    n = w.shape[2]
    return pl.pallas_call(
        _mm_kernel,
        grid=(n // tn, m // tm),
        in_specs=[
            pl.BlockSpec((tm, k), lambda j, i: (i, 0)),
            pl.BlockSpec((None, k, tn), lambda j, i: (layer, 0, j)),
        ],
        out_specs=pl.BlockSpec((tm, tn), lambda j, i: (i, j)),
        out_shape=jax.ShapeDtypeStruct((m, n), out_dtype),
        scratch_shapes=[pltpu.VMEM((k, tn), BF16)],
        compiler_params=_cparams(("arbitrary", "arbitrary"), V7X_VMEM_LIMIT),
        name="matmul",
    )(x, w)


def _conv_kernel(a1_ref, a2_ref, w_ref, cb_ref, g_ref, b_ref, o_ref, pad_ref, y_ref, *, seg, n_tap):
    half = n_tap // 2
    front = ((half + 7) // 8) * 8
    nseg = ROW_TILE // seg
    c = a1_ref.shape[1]
    u = a1_ref[...].astype(F32) * _sigmoid(a2_ref[...].astype(F32))
    pad_ref[...] = jnp.zeros(pad_ref.shape, F32)
    for s in range(nseg):
        pad_ref[s, front:front + seg, :] = u[s * seg:(s + 1) * seg, :]
    lanes = 128

    def chunk(ci, carry):
        c0 = pl.multiple_of(ci * lanes, lanes)
        acc = jnp.zeros((nseg, seg, lanes), F32)
        for k in range(n_tap):
            off = front - half + k
            acc = acc + w_ref[k:k + 1, pl.ds(c0, lanes)] * pad_ref[:, off:off + seg, pl.ds(c0, lanes)]
        y_ref[:, pl.ds(c0, lanes)] = acc.reshape(ROW_TILE, lanes)
        return carry

    lax.fori_loop(0, c // lanes, chunk, 0)
    y = y_ref[...] + cb_ref[...]
    o_ref[...] = _silu(_layer_norm_rows(y, g_ref[...], b_ref[...], LN_EPS)).astype(o_ref.dtype)


def conformer_conv_act(z, conv_w_l, conv_b_l, ln_g_l, ln_b_l, *, row_tile0, n_tiles, seg):
    n_tap, c = conv_w_l.shape
    half = n_tap // 2
    front = ((half + 7) // 8) * 8
    nseg = ROW_TILE // seg
    vec = pl.BlockSpec((1, c), lambda i: (0, 0))
    kern = functools.partial(_conv_kernel, seg=seg, n_tap=n_tap)
    return pl.pallas_call(
        kern,
        grid=(n_tiles,),
        in_specs=[
            pl.BlockSpec((ROW_TILE, c), lambda i: (row_tile0 + i, 0)),
            pl.BlockSpec((ROW_TILE, c), lambda i: (row_tile0 + i, 1)),
            pl.BlockSpec((n_tap, c), lambda i: (0, 0)),
            vec, vec, vec,
        ],
        out_specs=pl.BlockSpec((ROW_TILE, c), lambda i: (i, 0)),
        out_shape=jax.ShapeDtypeStruct((n_tiles * ROW_TILE, c), BF16),
        scratch_shapes=[pltpu.VMEM((nseg, seg + 2 * front, c), F32), pltpu.VMEM((ROW_TILE, c), F32)],
        compiler_params=_cparams(("arbitrary",), V7X_VMEM_LIMIT),
        name="conformer_conv",
    )(z, z, conv_w_l, conv_b_l, ln_g_l, ln_b_l)


def _rotary(t, cos, sin):
    half = t.shape[1] // 2
    t1, t2 = t[:, :half], t[:, half:]
    return jnp.concatenate([t1 * cos - t2 * sin, t1 * sin + t2 * cos], axis=1)


def _retention_kernel(logg_ref, qf_ref, kf_ref, vf_ref, cf_ref, sf_ref,
                      qb_ref, kb_ref, vb_ref, cb_ref, sb_ref,
                      of_ref, ob_ref, state_ref):
    h = pl.program_id(1)
    n = pl.program_id(2)
    c = ROW_TILE
    k_scale = RET_DK ** -0.5

    @pl.when(n == 0)
    def _():
        state_ref[...] = jnp.zeros(state_ref.shape, F32)

    row = lax.broadcasted_iota(jnp.int32, (c, c), 0)
    col = lax.broadcasted_iota(jnp.int32, (c, c), 1)
    ridx = lax.broadcasted_iota(jnp.int32, (c, 1), 0).astype(F32)

    def one_direction(d, q_ref, k_ref, v_ref, cos_ref, sin_ref, o_ref):
        lg = logg_ref[d, h]
        cos, sin = cos_ref[...], sin_ref[...]
        q = _rotary(q_ref[...].astype(F32), cos, sin)
        k = _rotary(k_ref[...].astype(F32), cos, sin) * k_scale
        v = v_ref[...].astype(BF16)
        dist = (row - col) if d == 0 else (col - row)
        decay = jnp.where(dist >= 0, jnp.exp(lg * jnp.maximum(dist, 0).astype(F32)), 0.0)
        qb = q.astype(BF16)
        scores = _dot_nt(qb, k.astype(BF16)) * decay
        inner = _dot(scores.astype(BF16), v)
        to_prev = (ridx + 1.0) if d == 0 else (c - ridx)
        to_end = (c - 1.0 - ridx) if d == 0 else ridx
        s_prev = state_ref[d]
        cross = _dot(qb, s_prev.astype(BF16)) * jnp.exp(lg * to_prev)
        o_ref[...] = inner + cross
        kw = (k * jnp.exp(lg * to_end)).astype(BF16)
        state_ref[d] = jnp.exp(lg * jnp.full((1, 1), float(c), F32)) * s_prev + _dot_tn(kw, v)

    one_direction(0, qf_ref, kf_ref, vf_ref, cf_ref, sf_ref, of_ref)
    one_direction(1, qb_ref, kb_ref, vb_ref, cb_ref, sb_ref, ob_ref)


def retention_scan(z, log_g, cos_tab, sin_tab, *, n_batch, tiles_per_batch, q_col0):
    ntok = z.shape[0]
    n_lat = n_batch * tiles_per_batch
    n_steps = tiles_per_batch + 1
    dk = RET_DK
    hh = RET_HEADS

    def row_f(b, n):
        return jnp.where(n == 0, n_lat + b, b * tiles_per_batch + n - 1)

    def row_b(b, n):
        return jnp.where(n == 0, n_lat + b, b * tiles_per_batch + tiles_per_batch - n)

    def pos_f(n):
        return n

    def pos_b(n):
        return jnp.where(n == 0, 0, tiles_per_batch + 1 - n)

    def zspec(rowfn, sec):
        return pl.BlockSpec((ROW_TILE, dk), lambda b, h, n: (rowfn(b, n), q_col0 + sec * hh + h))

    def tspec(posfn):
        return pl.BlockSpec((ROW_TILE, dk // 2), lambda b, h, n: (posfn(n), 0))

    def ospec(rowfn):
        return pl.BlockSpec((ROW_TILE, dk), lambda b, h, n: (rowfn(b, n), h))

    smem = pl.BlockSpec(memory_space=pltpu.SMEM)
    return pl.pallas_call(
        _retention_kernel,
        grid=(n_batch, hh, n_steps),
        in_specs=[smem,
                  zspec(row_f, 0), zspec(row_f, 1), zspec(row_f, 2), tspec(pos_f), tspec(pos_f),
                  zspec(row_b, 0), zspec(row_b, 1), zspec(row_b, 2), tspec(pos_b), tspec(pos_b)],
        out_specs=[ospec(row_f), ospec(row_b)],
        out_shape=[jax.ShapeDtypeStruct((ntok, hh * dk), F32)] * 2,
        scratch_shapes=[pltpu.VMEM((2, dk, dk), F32)],
        compiler_params=_cparams(("arbitrary", "arbitrary", "arbitrary"), V7X_VMEM_LIMIT),
        name="retention_scan",
    )(log_g, z, z, z, cos_tab, sin_tab, z, z, z, cos_tab, sin_tab)


def _ret_post_kernel(of_ref, ob_ref, g_ref, o_ref):
    dk = RET_DK
    for h in range(RET_HEADS):
        sl = slice(h * dk, (h + 1) * dk)
        o = of_ref[:, sl] + ob_ref[:, sl]
        mu = jnp.mean(o, axis=-1, keepdims=True)
        oc = o - mu
        var = jnp.mean(oc * oc, axis=-1, keepdims=True)
        on = oc * lax.rsqrt(var + HEAD_NORM_EPS)
        o_ref[:, sl] = (on * _silu(g_ref[:, sl].astype(F32))).astype(o_ref.dtype)


def retention_post(o_f, o_b, z, *, g_col):
    ntok, w = o_f.shape
    row = pl.BlockSpec((ROW_TILE, w), lambda i: (i, 0))
    return pl.pallas_call(
        _ret_post_kernel,
        grid=(ntok // ROW_TILE,),
        in_specs=[row, row, pl.BlockSpec((ROW_TILE, w), lambda i: (i, g_col))],
        out_specs=row,
        out_shape=jax.ShapeDtypeStruct((ntok, w), BF16),
        compiler_params=_cparams(("arbitrary",), V7X_VMEM_LIMIT),
        name="retention_post",
    )(o_f, o_b, z)


def _s5_local_kernel(u_ref, toep_ref, minc_ref, yl_ref, xre_ref, xim_ref):
    u0, u1 = u_ref[0], u_ref[1]
    yl_ref[0] = _dot(u0, toep_ref[0])
    yl_ref[1] = _dot(u1, toep_ref[1])
    xi = _dot(jnp.concatenate([u0, u1], axis=1), minc_ref[...])
    half = xi.shape[1] // 2
    xre_ref[...] = xi[:, :half]
    xim_ref[...] = xi[:, half:]


def _s5_scan_kernel(xre_ref, xim_ref, are_ref, aim_ref, ore_ref, oim_ref):
    n_chunk = xre_ref.shape[0]
    ar, ai = are_ref[...], aim_ref[...]

    def step(c, carry):
        sr, si = carry
        ore_ref[pl.ds(c, 1), :] = sr
        oim_ref[pl.ds(c, 1), :] = si
        nr = ar * sr - ai * si + xre_ref[pl.ds(c, 1), :]
        ni = ar * si + ai * sr + xim_ref[pl.ds(c, 1), :]
        return nr, ni

    zero = jnp.zeros(ar.shape, F32)
    lax.fori_loop(0, n_chunk, step, (zero, zero))


def _s5_state_kernel(xre_ref, xim_ref, mst_ref, yl_ref, y_ref):
    x0 = jnp.concatenate([xre_ref[...], xim_ref[...]], axis=1).astype(BF16)
    ys = _dot(x0, mst_ref[...])
    half = ys.shape[1] // 2
    y_ref[0] = yl_ref[0] + ys[:, :half]
    y_ref[1] = yl_ref[1] + ys[:, half:]


def s5_chunked(uc, toep, minc_pair, mstate_pair, a_re, a_im):
    bd, g, nch, wdt = uc.shape
    gp = g // 2
    lanes = 2 * S5_N
    upair = pl.BlockSpec((None, 2, nch, wdt), lambda i, j: (i, j, 0, 0))
    xcol = pl.BlockSpec((None, nch, lanes), lambda i, j: (i, 0, j))
    yl, xre, xim = pl.pallas_call(
        _s5_local_kernel,
        grid=(bd, gp),
        in_specs=[upair,
                  pl.BlockSpec((None, 2, wdt, wdt), lambda i, j: (i % 2, j, 0, 0)),
                  pl.BlockSpec((None, None, 2 * wdt, 2 * lanes), lambda i, j: (i % 2, j, 0, 0))],
        out_specs=[upair, xcol, xcol],
        out_shape=[jax.ShapeDtypeStruct((bd, g, nch, wdt), F32),
                   jax.ShapeDtypeStruct((bd, nch, g * S5_N), F32),
                   jax.ShapeDtypeStruct((bd, nch, g * S5_N), F32)],
        compiler_params=_cparams(("arbitrary", "arbitrary"), V7X_VMEM_LIMIT),
        name="s5_local",
    )(uc, toep, minc_pair)
    full = pl.BlockSpec((None, nch, g * S5_N), lambda i: (i, 0, 0))
    avec = pl.BlockSpec((None, 1, g * S5_N), lambda i: (i % 2, 0, 0))
    x0re, x0im = pl.pallas_call(
        _s5_scan_kernel,
        grid=(bd,),
        in_specs=[full, full, avec, avec],
        out_specs=[full, full],
        out_shape=[jax.ShapeDtypeStruct((bd, nch, g * S5_N), F32)] * 2,
        compiler_params=_cparams(("arbitrary",), V7X_VMEM_LIMIT),
        name="s5_scan",
    )(xre, xim, a_re, a_im)
    return pl.pallas_call(
        _s5_state_kernel,
        grid=(bd, gp),
        in_specs=[xcol, xcol,
                  pl.BlockSpec((None, None, 2 * lanes, 2 * wdt), lambda i, j: (i % 2, j, 0, 0)),
                  upair],
        out_specs=upair,
        out_shape=jax.ShapeDtypeStruct((bd, g, nch, wdt), F32),
        compiler_params=_cparams(("arbitrary", "arbitrary"), V7X_VMEM_LIMIT),
        name="s5_state",
    )(x0re, x0im, mstate_pair, yl)


def _gelu_tanh(v):
    return 0.5 * v * (1.0 + jnp.tanh(math.sqrt(2.0 / math.pi) * (v + 0.044715 * v * v * v)))


def _s5_post_kernel(yf_ref, yb_ref, u_ref, d_ref, w_ref, o_ref, wbf_ref):
    @pl.when(pl.program_id(0) == 0)
    def _():
        wbf_ref[...] = w_ref[...].astype(BF16)

    t = _gelu_tanh(yf_ref[...] + yb_ref[...] + d_ref[...] * u_ref[...].astype(F32))
    o_ref[...] = (t * _sigmoid(_dot(t.astype(BF16), wbf_ref[...]))).astype(o_ref.dtype)


def s5_post(y_f, y_b, z, s5_d_l, w_glu, layer, *, u_col):
    ntok, c = y_f.shape
    row = pl.BlockSpec((ROW_TILE, c), lambda i: (i, 0))
    return pl.pallas_call(
        _s5_post_kernel,
        grid=(ntok // ROW_TILE,),
        in_specs=[row, row, pl.BlockSpec((ROW_TILE, c), lambda i: (i, u_col)),
                  pl.BlockSpec((1, c), lambda i: (0, 0)),
                  pl.BlockSpec((None, c, c), lambda i: (layer, 0, 0))],
        out_specs=row,
        out_shape=jax.ShapeDtypeStruct((ntok, c), BF16),
        scratch_shapes=[pltpu.VMEM((c, c), BF16)],
        compiler_params=_cparams(("arbitrary",), V7X_VMEM_LIMIT),
        name="s5_post",
    )(y_f, y_b, z, s5_d_l, w_glu)


def _merge_kernel(a_ref, b_ref, c_ref, s0_ref, s1_ref, s2_ref, wa_ref, wb_ref, wc_ref, o_ref,
                  wa_bf, wb_bf, wc_bf):
    @pl.when(pl.program_id(1) == 0)
    def _():
        wa_bf[...] = wa_ref[...].astype(BF16)
        wb_bf[...] = wb_ref[...].astype(BF16)
        wc_bf[...] = wc_ref[...].astype(BF16)

    m = _sigmoid(s0_ref[...].astype(F32)) * _dot(a_ref[...], wa_bf[...])
    m = m + _sigmoid(s1_ref[...].astype(F32)) * _dot(b_ref[...], wb_bf[...])
    m = m + _sigmoid(s2_ref[...].astype(F32)) * _dot(c_ref[...], wc_bf[...])
    o_ref[...] = m.astype(o_ref.dtype)


def merge_branches(act_a, act_b, act_c, z, conv_proj, ret_proj, s5_proj, layer, *, s_col0, tm, tn):
    m = act_a.shape[0]
    d = conv_proj.shape[2]
    ka, kb, kc = act_a.shape[1], act_b.shape[1], act_c.shape[1]
    nblk = d // tn

    def aspec(k):
        return pl.BlockSpec((tm, k), lambda j, i: (i, 0))

    def sspec(br):
        return pl.BlockSpec((tm, tn), lambda j, i: (i, s_col0 // tn + br * nblk + j))

    def wspec(k):
        return pl.BlockSpec((None, k, tn), lambda j, i: (layer, 0, j))

    return pl.pallas_call(
        _merge_kernel,
        grid=(nblk, m // tm),
        in_specs=[aspec(ka), aspec(kb), aspec(kc), sspec(0), sspec(1), sspec(2),
                  wspec(ka), wspec(kb), wspec(kc)],
        out_specs=pl.BlockSpec((tm, tn), lambda j, i: (i, j)),
        out_shape=jax.ShapeDtypeStruct((m, d), BF16),
        scratch_shapes=[pltpu.VMEM((ka, tn), BF16), pltpu.VMEM((kb, tn), BF16), pltpu.VMEM((kc, tn), BF16)],
        compiler_params=_cparams(("arbitrary", "arbitrary"), V7X_VMEM_LIMIT),
        name="merge_branches",
    )(act_a, act_b, act_c, z, z, z, conv_proj, ret_proj, s5_proj)


def _ffn_kernel(*refs, has_gate):
    if has_gate:
        x_ref, wg_ref, wu_ref, wd_ref, gate_ref, o_ref = refs
    else:
        x_ref, wg_ref, wu_ref, wd_ref, o_ref = refs
        gate_ref = None

    @pl.when(pl.program_id(1) == 0)
    def _():
        o_ref[...] = jnp.zeros(o_ref.shape, F32)

    x = x_ref[...]
    hg = _dot(x, wg_ref[...].astype(BF16))
    hu = _dot(x, wu_ref[...].astype(BF16))
    a = _silu(hg) * hu
    if has_gate:
        a = a * gate_ref[...]
    o_ref[...] += _dot(a.astype(BF16), wd_ref[...].astype(BF16))


def ffn_blocks(x, wg, wu, wd, wg_spec, wu_spec, wd_spec, n_blocks, gate, *, tm):
    m, d = x.shape
    args = [x, wg, wu, wd]
    specs = [pl.BlockSpec((tm, d), lambda i, e: (i, 0)), wg_spec, wu_spec, wd_spec]
    if gate is not None:
        args.append(gate)
        specs.append(pl.BlockSpec((None, tm, 1), lambda i, e: (e, i, 0)))
    return pl.pallas_call(
        functools.partial(_ffn_kernel, has_gate=gate is not None),
        grid=(m // tm, n_blocks),
        in_specs=specs,
        out_specs=pl.BlockSpec((tm, d), lambda i, e: (i, 0)),
        out_shape=jax.ShapeDtypeStruct((m, d), F32),
        compiler_params=_cparams(("arbitrary", "arbitrary"), V7X_VMEM_LIMIT),
        name="ffn_blocks",
    )(*args)


_HI = lax.Precision.HIGHEST


def _cmul(ar, ai, br, bi):
    return ar * br - ai * bi, ar * bi + ai * br


def s5_operators(a_re, a_im, log_dt, b_re, b_im, c_re, c_im):
    t = S5_T
    dt = jnp.exp(log_dt)[..., None]
    adt_re, adt_im = a_re * dt, a_im * dt
    tau = jnp.arange(t + 1, dtype=F32)[None, None, :, None]
    mag = jnp.exp(adt_re[:, :, None, :] * tau)
    ang = adt_im[:, :, None, :] * tau
    pw_re, pw_im = mag * jnp.cos(ang), mag * jnp.sin(ang)
    ab_re, ab_im = pw_re[:, :, 1], pw_im[:, :, 1]
    den = a_re * a_re + a_im * a_im
    nr, ni = ab_re - 1.0, ab_im
    f_re = (nr * a_re + ni * a_im) / den
    f_im = (ni * a_re - nr * a_im) / den
    bb_re, bb_im = _cmul(f_re[..., None], f_im[..., None], b_re, b_im)
    m1_re, m1_im = _cmul(pw_re[..., None], pw_im[..., None], bb_re[:, :, None], bb_im[:, :, None])
    kk = (jnp.einsum('dgpn,dgtnq->dgtpq', c_re, m1_re[:, :, :t], precision=_HI)
          - jnp.einsum('dgpn,dgtnq->dgtpq', c_im, m1_im[:, :, :t], precision=_HI))
    ti = jnp.arange(t)
    lag = ti[:, None] - ti[None, :]
    kg = kk[:, :, jnp.clip(lag, 0, t - 1)]
    kg = jnp.where((lag >= 0)[None, None, :, :, None, None], kg, 0.0)
    nd, g = a_re.shape[0], a_re.shape[1]
    p = b_re.shape[-1]
    n = a_re.shape[-1]
    toep = kg.transpose(0, 1, 3, 5, 2, 4).reshape(nd, g, t * p, t * p)
    inc_re = m1_re[:, :, t - 1 - ti]
    inc_im = m1_im[:, :, t - 1 - ti]
    minc = jnp.stack([inc_re, inc_im], axis=3)
    minc = minc.transpose(0, 1, 2, 5, 3, 4).reshape(nd, g, t * p, 2 * n)
    w_re, w_im = _cmul(c_re[:, :, None], c_im[:, :, None],
                       pw_re[:, :, 1:, None, :], pw_im[:, :, 1:, None, :])
    mst = jnp.stack([w_re, -w_im], axis=2)
    mst = mst.transpose(0, 1, 2, 5, 3, 4).reshape(nd, g, 2 * n, t * p)
    gp = g // 2
    mi = minc.reshape(nd, gp, 2, t * p, 2, n)
    eye = jnp.eye(2, dtype=F32)
    minc_pair = jnp.einsum('dgjrkn,jl->dgjrkln', mi, eye).reshape(nd, gp, 2 * t * p, 4 * n)
    ms = mst.reshape(nd, gp, 2, 2, n, t * p)
    mstate_pair = jnp.einsum('dgjknc,jl->dgkjnlc', ms, eye).reshape(nd, gp, 4 * n, 2 * t * p)
    aT_re = pw_re[:, :, t].reshape(nd, 1, g * n)
    aT_im = pw_im[:, :, t].reshape(nd, 1, g * n)
    return toep.astype(BF16), minc_pair.astype(BF16), mstate_pair.astype(BF16), aT_re, aT_im


def rotary_tables(n_pos, half):
    freq = ROPE_BASE ** (-jnp.arange(half, dtype=F32) / half)
    ang = jnp.arange(n_pos, dtype=F32)[:, None] * freq[None, :]
    return jnp.cos(ang), jnp.sin(ang)


def kernel(x, c, ctx, c_ctx, emb_ln_g, emb_ln_b, ada_w, ada_b, w_in, conv_w, conv_b, conv_ln_g, conv_ln_b, conv_proj, ret_decay_logit, ret_proj, s5_a_re, s5_a_im, s5_log_dt, s5_b_re, s5_b_im, s5_c_re, s5_c_im, s5_d, s5_w_glu, s5_proj, w_out, ln1_g, ln1_b, ln2_g, ln2_b, router_w, router_bias, exp_w_gate, exp_w_up, exp_w_down, sh_w_gate, sh_w_up, sh_w_down):
    n_batch, seq, d = x.shape
    lc = ctx.shape[1]
    depth = w_in.shape[0]
    conv_c = conv_w.shape[2]
    ret_w = ret_proj.shape[1]
    s5_c = s5_d.shape[1]
    s5_g = s5_c // S5_P
    n_exp, _, exp_ff = exp_w_gate.shape[1:]
    sh_ff = sh_w_gate.shape[2]
    rows = seq // GRID_W
    assert lc == ROW_TILE and seq % ROW_TILE == 0 and n_batch + 1 <= MOD_ROWS
    assert RET_HEADS * RET_DK == ret_w and S5_T * S5_P == ROW_TILE
    tiles_per_batch = seq // ROW_TILE
    n_lat_tiles = n_batch * tiles_per_batch
    n_tiles = n_lat_tiles + n_batch
    n_lat = n_batch * seq
    alpha = (2.0 * depth) ** 0.25
    ntok = n_tiles * ROW_TILE
    tm_mm = _largest_row_tile(ntok, 2)
    tm_ffn = _largest_row_tile(ntok, 2) if ntok % 544 else 544
    col_a, col_q = 0, 2 * conv_c
    col_g = col_q + 3 * ret_w
    col_u = col_g + ret_w
    col_s = col_u + s5_c

    group_fn = functools.partial(_group_of_tile, n_latent_tiles=n_lat_tiles,
                                 tiles_per_batch=tiles_per_batch, n_batch=n_batch)

    tokens = jnp.concatenate([x.reshape(n_lat, d), ctx.reshape(n_batch * lc, d)], axis=0)
    cvec = jnp.concatenate([c, c_ctx[None, :], jnp.zeros((MOD_ROWS - n_batch - 1, d), F32)], axis=0)
    mods = [ada_modulation(cvec, ada_w, ada_b[i][None, :], i) for i in range(depth)]
    cos_tab, sin_tab = rotary_tables(lc + seq, RET_DK // 2)
    log_g = jax.nn.log_sigmoid(ret_decay_logit.astype(F32))

    xl, h = resid_ln_mod(tokens, [], None, 0, emb_ln_g[None, :], emb_ln_b[None, :], mods[0], 0, 1,
                         alpha=1.0, group_fn=group_fn, n_tiles=n_tiles)

    for i in range(depth):
        last = i == depth - 1
        mod = mods[i]
        z = matmul_stacked_w(h, w_in, i, F32, tm_mm, 512)
        act_a = jnp.concatenate([
            conformer_conv_act(z, conv_w[i], conv_b[i][None, :], conv_ln_g[i][None, :], conv_ln_b[i][None, :],
                               row_tile0=0, n_tiles=n_lat_tiles, seg=GRID_W),
            conformer_conv_act(z, conv_w[i], conv_b[i][None, :], conv_ln_g[i][None, :], conv_ln_b[i][None, :],
                               row_tile0=n_lat_tiles, n_tiles=n_batch, seg=lc)], axis=0)
        o_f, o_b = retention_scan(z, log_g[i], cos_tab, sin_tab, n_batch=n_batch,
                                  tiles_per_batch=tiles_per_batch, q_col0=col_q // RET_DK)
        act_b = retention_post(o_f, o_b, z, g_col=col_g // ret_w)
        u = z[:, col_u:col_u + s5_c]
        ux_cm = u[:n_lat].reshape(n_batch, rows, GRID_W, s5_c).transpose(0, 2, 1, 3).reshape(n_batch, seq, s5_c)
        uc = u[n_lat:].reshape(n_batch, lc, s5_c)
        seq_f = jnp.concatenate([uc, ux_cm], axis=1)
        seq_b = jnp.concatenate([uc[:, ::-1], ux_cm[:, ::-1]], axis=1)
        n_chunk = (lc + seq) // S5_T
        u_fold = jnp.stack([seq_f, seq_b], axis=1).reshape(n_batch * 2, n_chunk, S5_T, s5_g, S5_P)
        u_fold = u_fold.transpose(0, 3, 1, 2, 4).reshape(n_batch * 2, s5_g, n_chunk, S5_T * S5_P).astype(BF16)
        ops = s5_operators(s5_a_re[i], s5_a_im[i], s5_log_dt[i], s5_b_re[i], s5_b_im[i], s5_c_re[i], s5_c_im[i])
        y_fold = s5_chunked(u_fold, *ops)
        y_seq = y_fold.reshape(n_batch, 2, s5_g, n_chunk, S5_T, S5_P).transpose(0, 1, 3, 4, 2, 5)
        y_seq = y_seq.reshape(n_batch, 2, lc + seq, s5_c)
        yf_c, yf_x = y_seq[:, 0, :lc], y_seq[:, 0, lc:]
        y_bwd = y_seq[:, 1, ::-1]
        yb_x, yb_c = y_bwd[:, :seq], y_bwd[:, seq:]

        def to_rows(v):
            return v.reshape(n_batch, GRID_W, rows, s5_c).transpose(0, 2, 1, 3).reshape(n_lat, s5_c)

        y_f = jnp.concatenate([to_rows(yf_x), yf_c.reshape(n_batch * lc, s5_c)], axis=0)
        y_b = jnp.concatenate([to_rows(yb_x), yb_c.reshape(n_batch * lc, s5_c)], axis=0)
        act_c = s5_post(y_f, y_b, z, s5_d[i][None, :], s5_w_glu, i, u_col=col_u // s5_c)
        merged = merge_branches(act_a, act_b, act_c, z, conv_proj, ret_proj, s5_proj, i,
                                s_col0=col_s, tm=tm_mm, tn=512)
        y_mix = matmul_stacked_w(merged, w_out, i, F32, tm_mm, 512)
        xl, h2, gate_t = resid_ln_mod(xl, [y_mix], mod, 2, ln1_g[i][None, :], ln1_b[i][None, :], mod, 3, 4,
                                      alpha=alpha, group_fn=group_fn, n_tiles=n_tiles,
                                      router=(router_w[i].T, router_bias[i][:, None]))
        y_routed = ffn_blocks(
            h2, exp_w_gate, exp_w_up, exp_w_down,
            pl.BlockSpec((None, None, d, exp_ff), lambda r, e: (i, e, 0, 0)),
            pl.BlockSpec((None, None, d, exp_ff), lambda r, e: (i, e, 0, 0)),
            pl.BlockSpec((None, None, exp_ff, d), lambda r, e: (i, e, 0, 0)),
            n_exp, gate_t[:, :, None], tm=tm_ffn)
        y_shared = ffn_blocks(
            h2, sh_w_gate, sh_w_up, sh_w_down,
            pl.BlockSpec((None, d, exp_ff), lambda r, e: (i, 0, e)),
            pl.BlockSpec((None, d, exp_ff), lambda r, e: (i, 0, e)),
            pl.BlockSpec((None, exp_ff, d), lambda r, e: (i, e, 0)),
            sh_ff // exp_ff, None, tm=tm_ffn)
        if last:
            (xl,) = resid_ln_mod(xl, [y_routed, y_shared], mod, 5, ln2_g[i][None, :], ln2_b[i][None, :],
                                 None, 0, 0, alpha=alpha, group_fn=group_fn, n_tiles=n_lat_tiles)
        else:
            xl, h = resid_ln_mod(xl, [y_routed, y_shared], mod, 5, ln2_g[i][None, :], ln2_b[i][None, :],
                                 mods[i + 1], 0, 1, alpha=alpha, group_fn=group_fn, n_tiles=n_tiles)
    return xl.reshape(n_batch, seq, d)
```

```python
import functools
import math

import jax
import jax.numpy as jnp
from jax import lax
from jax.experimental import pallas as pl
from jax.experimental.pallas import tpu as pltpu

F32 = jnp.float32
BF16 = jnp.bfloat16

GRID_W = 64
RET_HEADS = 8
RET_DK = 256
S5_P = 16
S5_N = 64
ROPE_BASE = 10000.0
N_GROUPS = 8
TOPK_GROUPS = 4
TOP_K = 8
ROUTED_SCALE = 2.5
LN_EPS = 1e-5
HEAD_NORM_EPS = 1e-5
NEG_BIG = -1e30
N_BRANCH = 3

ROW_TILE = 256
S5_LANES = 128
S5_T = 8
MOD_ROWS = 8
V7X_VMEM_LIMIT = 56 * 1024 * 1024


def _largest_row_tile(ntok, max_tiles):
    n = ntok // ROW_TILE
    k = max(t for t in range(1, max_tiles + 1) if n % t == 0)
    return k * ROW_TILE


def _cparams(sem, vmem=None):
    return pltpu.CompilerParams(dimension_semantics=sem, vmem_limit_bytes=vmem)


def _split_bf16(v):
    hi = v.astype(BF16)
    lo = (v - hi.astype(F32)).astype(BF16)
    return hi, lo


def _dot(a, b):
    return jnp.dot(a, b, preferred_element_type=F32)


def _dot_nt(a, b):
    return lax.dot_general(a, b, (((1,), (1,)), ((), ())), preferred_element_type=F32)


def _dot_tn(a, b):
    return lax.dot_general(a, b, (((0,), (0,)), ((), ())), preferred_element_type=F32)


def _sigmoid(v):
    return 1.0 / (1.0 + jnp.exp(-v))


def _silu(v):
    return v * _sigmoid(v)


def _layer_norm_rows(v, g, b, eps):
    mu = jnp.mean(v, axis=-1, keepdims=True)
    vc = v - mu
    var = jnp.mean(vc * vc, axis=-1, keepdims=True)
    return vc * lax.rsqrt(var + eps) * g + b


def _ada_kernel(c_ref, w_ref, b_ref, o_ref):
    c = _silu(c_ref[...])
    ch, cl = _split_bf16(c)
    wh, wl = _split_bf16(w_ref[...])
    o_ref[...] = _dot(ch, wh) + _dot(ch, wl) + _dot(cl, wh) + b_ref[...]


def ada_modulation(cvec, ada_w, ada_b_l, layer):
    _, d, n = ada_w.shape
    tn = 512
    return pl.pallas_call(
        _ada_kernel,
        grid=(n // tn,),
        in_specs=[
            pl.BlockSpec((MOD_ROWS, d), lambda j: (0, 0)),
            pl.BlockSpec((None, d, tn), lambda j: (layer, 0, j)),
            pl.BlockSpec((1, tn), lambda j: (0, j)),
        ],
        out_specs=pl.BlockSpec((MOD_ROWS, tn), lambda j: (0, j)),
        out_shape=jax.ShapeDtypeStruct((MOD_ROWS, n), F32),
        compiler_params=_cparams(("arbitrary",), V7X_VMEM_LIMIT),
        name="ada_modulation",
    )(cvec, ada_w, ada_b_l)


def _group_of_tile(i, n_latent_tiles, tiles_per_batch, n_batch):
    return jnp.where(i < n_latent_tiles, i // tiles_per_batch, n_batch)


def _route(h, rwt_ref, rb_ref):
    hh, hl = _split_bf16(h)
    wh, wl = _split_bf16(rwt_ref[...])
    logits = _dot_nt(wh, hh) + _dot_nt(wh, hl) + _dot_nt(wl, hh)
    scores = _sigmoid(logits)
    sel = scores + rb_ref[...]
    n_e, tm = sel.shape
    per = n_e // N_GROUPS
    shape3 = (N_GROUPS, per, tm)
    sel3 = sel.reshape(shape3)
    io_e = lax.broadcasted_iota(jnp.int32, shape3, 1)
    io_g = lax.broadcasted_iota(jnp.int32, shape3, 0)
    m1 = jnp.max(sel3, axis=1, keepdims=True)
    first = jnp.min(jnp.where(sel3 == m1, io_e, per), axis=1, keepdims=True)
    m2 = jnp.max(jnp.where(io_e == first, -jnp.inf, sel3), axis=1, keepdims=True)
    work = m1 + m2
    iog1 = lax.broadcasted_iota(jnp.int32, work.shape, 0)
    gsel = jnp.zeros(work.shape, F32)
    for _ in range(TOPK_GROUPS):
        m = jnp.max(work, axis=0, keepdims=True)
        fi = jnp.min(jnp.where(work == m, iog1, N_GROUPS), axis=0, keepdims=True)
        hit = iog1 == fi
        gsel = jnp.where(hit, 1.0, gsel)
        work = jnp.where(hit, -jnp.inf, work)
    work = jnp.where(jnp.broadcast_to(gsel, shape3) > 0.0, sel3, NEG_BIG)
    flat = io_g * per + io_e
    esel = jnp.zeros(shape3, F32)
    for _ in range(TOP_K):
        m = jnp.max(jnp.max(work, axis=1, keepdims=True), axis=0, keepdims=True)
        cand = jnp.where(work == m, flat, n_e)
        fi = jnp.min(jnp.min(cand, axis=1, keepdims=True), axis=0, keepdims=True)
        hit = flat == fi
        esel = jnp.where(hit, 1.0, esel)
        work = jnp.where(hit, -jnp.inf, work)
    w = jnp.where(esel.reshape(n_e, tm) > 0.0, scores, 0.0)
    return ROUTED_SCALE * w / jnp.sum(w, axis=0, keepdims=True)


def _resid_ln_mod_kernel(*refs, alpha, n_y, has_mod, has_router, group_fn):
    it = iter(refs)
    x_ref = next(it)
    y_refs = [next(it) for _ in range(n_y)]
    gate_ref = next(it) if n_y else None
    g_ref, b_ref = next(it), next(it)
    shift_ref = scale_ref = rwt_ref = rb_ref = None
    if has_mod:
        shift_ref, scale_ref = next(it), next(it)
    if has_router:
        rwt_ref, rb_ref = next(it), next(it)
    xl_ref = next(it)
    h_ref = next(it) if has_mod else None
    gate_out_ref = next(it) if has_router else None

    grp = group_fn(pl.program_id(0))
    v = x_ref[...]
    if n_y:
        y = y_refs[0][...]
        for r in y_refs[1:]:
            y = y + r[...]
        v = alpha * v + gate_ref[pl.ds(grp, 1), :] * y
    xl = _layer_norm_rows(v, g_ref[...], b_ref[...], LN_EPS)
    xl_ref[...] = xl
    if has_mod:
        h = xl * (1.0 + scale_ref[pl.ds(grp, 1), :]) + shift_ref[pl.ds(grp, 1), :]
        h_ref[...] = h.astype(h_ref.dtype)
        if has_router:
            gate_out_ref[...] = _route(h, rwt_ref, rb_ref)


def resid_ln_mod(x, ys, gate_mod, gate_col, ln_g, ln_b, mod, shift_col, scale_col, *,
                 alpha, group_fn, n_tiles, router=None):
    d = x.shape[1]
    n_y = len(ys)
    has_mod = mod is not None
    has_router = router is not None
    row = pl.BlockSpec((ROW_TILE, d), lambda i: (i, 0))
    vec = pl.BlockSpec((1, d), lambda i: (0, 0))
    args, specs = [x], [row]
    for y in ys:
        args.append(y)
        specs.append(row)
    if n_y:
        args.append(gate_mod)
        specs.append(pl.BlockSpec((MOD_ROWS, d), lambda i: (0, gate_col)))
    args += [ln_g, ln_b]
    specs += [vec, vec]
    if has_mod:
        args += [mod, mod]
        specs += [pl.BlockSpec((MOD_ROWS, d), lambda i: (0, shift_col)),
                  pl.BlockSpec((MOD_ROWS, d), lambda i: (0, scale_col))]
    out_shapes = [jax.ShapeDtypeStruct((n_tiles * ROW_TILE, d), F32)]
    out_specs = [row]
    if has_mod:
        out_shapes.append(jax.ShapeDtypeStruct((n_tiles * ROW_TILE, d), BF16))
        out_specs.append(row)
    if has_router:
        rwt, rb = router
        n_e = rwt.shape[0]
        args += [rwt, rb]
        specs += [pl.BlockSpec((n_e, d), lambda i: (0, 0)), pl.BlockSpec((n_e, 1), lambda i: (0, 0))]
        out_shapes.append(jax.ShapeDtypeStruct((n_e, n_tiles * ROW_TILE), F32))
        out_specs.append(pl.BlockSpec((n_e, ROW_TILE), lambda i: (0, i)))
    kern = functools.partial(_resid_ln_mod_kernel, alpha=alpha, n_y=n_y, has_mod=has_mod,
                             has_router=has_router, group_fn=group_fn)
    return pl.pallas_call(
        kern, grid=(n_tiles,), in_specs=specs, out_specs=out_specs, out_shape=out_shapes,
        compiler_params=_cparams(("arbitrary",), V7X_VMEM_LIMIT),
        name="resid_ln_mod",
    )(*args)


def _mm_kernel(x_ref, w_ref, o_ref, wbf_ref):
    @pl.when(pl.program_id(1) == 0)
    def _():
        wbf_ref[...] = w_ref[...].astype(BF16)

    o_ref[...] = _dot(x_ref[...], wbf_ref[...]).astype(o_ref.dtype)


def matmul_stacked_w(x, w, layer, out_dtype, tm, tn):
    m, k = x.shape
    n = w.shape[2]
    return pl.pallas_call(
        _mm_kernel,
        grid=(n // tn, m // tm),
        in_specs=[
            pl.BlockSpec((tm, k), lambda j, i: (i, 0)),
            pl.BlockSpec((None, k, tn), lambda j, i: (layer, 0, j)),
        ],
        out_specs=pl.BlockSpec((tm, tn), lambda j, i: (i, j)),
        out_shape=jax.ShapeDtypeStruct((m, n), out_dtype),
        scratch_shapes=[pltpu.VMEM((k, tn), BF16)],
        compiler_params=_cparams(("arbitrary", "arbitrary"), V7X_VMEM_LIMIT),
        name="matmul",
    )(x, w)


def _conv_kernel(a1_ref, a2_ref, w_ref, cb_ref, g_ref, b_ref, o_ref, pad_ref, y_ref, *, seg, n_tap):
    half = n_tap // 2
    front = ((half + 7) // 8) * 8
    nseg = ROW_TILE // seg
    c = a1_ref.shape[1]
    u = a1_ref[...].astype(F32) * _sigmoid(a2_ref[...].astype(F32))
    pad_ref[...] = jnp.zeros(pad_ref.shape, F32)
    for s in range(nseg):
        pad_ref[s, front:front + seg, :] = u[s * seg:(s + 1) * seg, :]
    lanes = 128

    def chunk(ci, carry):
        c0 = pl.multiple_of(ci * lanes, lanes)
        acc = jnp.zeros((nseg, seg, lanes), F32)
        for k in range(n_tap):
            off = front - half + k
            acc = acc + w_ref[k:k + 1, pl.ds(c0, lanes)] * pad_ref[:, off:off + seg, pl.ds(c0, lanes)]
        y_ref[:, pl.ds(c0, lanes)] = acc.reshape(ROW_TILE, lanes)
        return carry

    lax.fori_loop(0, c // lanes, chunk, 0)
    y = y_ref[...] + cb_ref[...]
    o_ref[...] = _silu(_layer_norm_rows(y, g_ref[...], b_ref[...], LN_EPS)).astype(o_ref.dtype)


def conformer_conv_act(z, conv_w_l, conv_b_l, ln_g_l, ln_b_l, *, row_tile0, n_tiles, seg):
    n_tap, c = conv_w_l.shape
    half = n_tap // 2
    front = ((half + 7) // 8) * 8
    nseg = ROW_TILE // seg
    vec = pl.BlockSpec((1, c), lambda i: (0, 0))
    kern = functools.partial(_conv_kernel, seg=seg, n_tap=n_tap)
    return pl.pallas_call(
        kern,
        grid=(n_tiles,),
        in_specs=[
            pl.BlockSpec((ROW_TILE, c), lambda i: (row_tile0 + i, 0)),
            pl.BlockSpec((ROW_TILE, c), lambda i: (row_tile0 + i, 1)),
            pl.BlockSpec((n_tap, c), lambda i: (0, 0)),
            vec, vec, vec,
        ],
        out_specs=pl.BlockSpec((ROW_TILE, c), lambda i: (i, 0)),
        out_shape=jax.ShapeDtypeStruct((n_tiles * ROW_TILE, c), BF16),
        scratch_shapes=[pltpu.VMEM((nseg, seg + 2 * front, c), F32), pltpu.VMEM((ROW_TILE, c), F32)],
        compiler_params=_cparams(("arbitrary",), V7X_VMEM_LIMIT),
        name="conformer_conv",
    )(z, z, conv_w_l, conv_b_l, ln_g_l, ln_b_l)


def _rotary(t, cos, sin):
    half = t.shape[1] // 2
    t1, t2 = t[:, :half], t[:, half:]
    return jnp.concatenate([t1 * cos - t2 * sin, t1 * sin + t2 * cos], axis=1)


def _retention_kernel(logg_ref, qf_ref, kf_ref, vf_ref, cf_ref, sf_ref,
                      qb_ref, kb_ref, vb_ref, cb_ref, sb_ref,
                      of_ref, ob_ref, state_ref):
    h = pl.program_id(1)
    n = pl.program_id(2)
    c = ROW_TILE
    k_scale = RET_DK ** -0.5

    @pl.when(n == 0)
    def _():
        state_ref[...] = jnp.zeros(state_ref.shape, F32)

    row = lax.broadcasted_iota(jnp.int32, (c, c), 0)
    col = lax.broadcasted_iota(jnp.int32, (c, c), 1)
    ridx = lax.broadcasted_iota(jnp.int32, (c, 1), 0).astype(F32)

    def one_direction(d, q_ref, k_ref, v_ref, cos_ref, sin_ref, o_ref):
        lg = logg_ref[d, h]
        cos, sin = cos_ref[...], sin_ref[...]
        q = _rotary(q_ref[...].astype(F32), cos, sin)
        k = _rotary(k_ref[...].astype(F32), cos, sin) * k_scale
        v = v_ref[...].astype(BF16)
        dist = (row - col) if d == 0 else (col - row)
        decay = jnp.where(dist >= 0, jnp.exp(lg * jnp.maximum(dist, 0).astype(F32)), 0.0)
        qb = q.astype(BF16)
        scores = _dot_nt(qb, k.astype(BF16)) * decay
        inner = _dot(scores.astype(BF16), v)
        to_prev = (ridx + 1.0) if d == 0 else (c - ridx)
        to_end = (c - 1.0 - ridx) if d == 0 else ridx
        s_prev = state_ref[d]
        cross = _dot(qb, s_prev.astype(BF16)) * jnp.exp(lg * to_prev)
        o_ref[...] = inner + cross
        kw = (k * jnp.exp(lg * to_end)).astype(BF16)
        state_ref[d] = jnp.exp(lg * jnp.full((1, 1), float(c), F32)) * s_prev + _dot_tn(kw, v)

    one_direction(0, qf_ref, kf_ref, vf_ref, cf_ref, sf_ref, of_ref)
    one_direction(1, qb_ref, kb_ref, vb_ref, cb_ref, sb_ref, ob_ref)


def retention_scan(z, log_g, cos_tab, sin_tab, *, n_batch, tiles_per_batch, q_col0):
    ntok = z.shape[0]
    n_lat = n_batch * tiles_per_batch
    n_steps = tiles_per_batch + 1
    dk = RET_DK
    hh = RET_HEADS

    def row_f(b, n):
        return jnp.where(n == 0, n_lat + b, b * tiles_per_batch + n - 1)

    def row_b(b, n):
        return jnp.where(n == 0, n_lat + b, b * tiles_per_batch + tiles_per_batch - n)

    def pos_f(n):
        return n

    def pos_b(n):
        return jnp.where(n == 0, 0, tiles_per_batch + 1 - n)

    def zspec(rowfn, sec):
        return pl.BlockSpec((ROW_TILE, dk), lambda b, h, n: (rowfn(b, n), q_col0 + sec * hh + h))

    def tspec(posfn):
        return pl.BlockSpec((ROW_TILE, dk // 2), lambda b, h, n: (posfn(n), 0))

    def ospec(rowfn):
        return pl.BlockSpec((ROW_TILE, dk), lambda b, h, n: (rowfn(b, n), h))

    smem = pl.BlockSpec(memory_space=pltpu.SMEM)
    return pl.pallas_call(
        _retention_kernel,
        grid=(n_batch, hh, n_steps),
        in_specs=[smem,
                  zspec(row_f, 0), zspec(row_f, 1), zspec(row_f, 2), tspec(pos_f), tspec(pos_f),
                  zspec(row_b, 0), zspec(row_b, 1), zspec(row_b, 2), tspec(pos_b), tspec(pos_b)],
        out_specs=[ospec(row_f), ospec(row_b)],
        out_shape=[jax.ShapeDtypeStruct((ntok, hh * dk), F32)] * 2,
        scratch_shapes=[pltpu.VMEM((2, dk, dk), F32)],
        compiler_params=_cparams(("arbitrary", "arbitrary", "arbitrary"), V7X_VMEM_LIMIT),
        name="retention_scan",
    )(log_g, z, z, z, cos_tab, sin_tab, z, z, z, cos_tab, sin_tab)


def _ret_post_kernel(of_ref, ob_ref, g_ref, o_ref):
    dk = RET_DK
    for h in range(RET_HEADS):
        sl = slice(h * dk, (h + 1) * dk)
        o = of_ref[:, sl] + ob_ref[:, sl]
        mu = jnp.mean(o, axis=-1, keepdims=True)
        oc = o - mu
        var = jnp.mean(oc * oc, axis=-1, keepdims=True)
        on = oc * lax.rsqrt(var + HEAD_NORM_EPS)
        o_ref[:, sl] = (on * _silu(g_ref[:, sl].astype(F32))).astype(o_ref.dtype)


def retention_post(o_f, o_b, z, *, g_col):
    ntok, w = o_f.shape
    row = pl.BlockSpec((ROW_TILE, w), lambda i: (i, 0))
    return pl.pallas_call(
        _ret_post_kernel,
        grid=(ntok // ROW_TILE,),
        in_specs=[row, row, pl.BlockSpec((ROW_TILE, w), lambda i: (i, g_col))],
        out_specs=row,
        out_shape=jax.ShapeDtypeStruct((ntok, w), BF16),
        compiler_params=_cparams(("arbitrary",), V7X_VMEM_LIMIT),
        name="retention_post",
    )(o_f, o_b, z)


def _s5_local_kernel(u_ref, toep_ref, minc_ref, yl_ref, xre_ref, xim_ref):
    u = u_ref[...]
    yl_ref[...] = _dot(u, toep_ref[...])
    xi = _dot(u, minc_ref[...])
    half = xi.shape[1] // 2
    xre_ref[...] = xi[:, :half]
    xim_ref[...] = xi[:, half:]


def _s5_scan_kernel(xre_ref, xim_ref, are_ref, aim_ref, ore_ref, oim_ref, *, n_ctx_chunk):
    n_chunk = xre_ref.shape[0]
    backward = pl.program_id(1) == 1
    ar, ai = are_ref[...], aim_ref[...]

    def step(i, carry):
        sr, si = carry
        rev = jnp.where(i < n_ctx_chunk, n_ctx_chunk - 1 - i, n_chunk + n_ctx_chunk - 1 - i)
        c = jnp.where(backward, rev, i)
        ore_ref[pl.ds(c, 1), :] = sr
        oim_ref[pl.ds(c, 1), :] = si
        nr = ar * sr - ai * si + xre_ref[pl.ds(c, 1), :]
        ni = ar * si + ai * sr + xim_ref[pl.ds(c, 1), :]
        return nr, ni

    zero = jnp.zeros(ar.shape, F32)
    lax.fori_loop(0, n_chunk, step, (zero, zero))


def _s5_state_kernel(xre_ref, xim_ref, mst_ref, yl_ref, y_ref):
    y = yl_ref[0] + yl_ref[1]
    for d in range(2):
        x0 = jnp.concatenate([xre_ref[d], xim_ref[d]], axis=1).astype(BF16)
        y = y + _dot(x0, mst_ref[d])
    y_ref[...] = y


def s5_chunked(u_fold, toep, minc, mstate, a_re, a_im, *, n_ctx_chunk):
    nb, gb, nch, fw = u_fold.shape
    sw = fw // 2
    op = pl.BlockSpec((None, None, fw, fw), lambda g, d, b: (d, g, 0, 0))
    yl, xre, xim = pl.pallas_call(
        _s5_local_kernel,
        grid=(gb, 2, nb),
        in_specs=[pl.BlockSpec((None, None, nch, fw), lambda g, d, b: (b, g, 0, 0)), op, op],
        out_specs=[pl.BlockSpec((None, None, None, nch, fw), lambda g, d, b: (b, d, g, 0, 0)),
                   pl.BlockSpec((None, None, nch, sw), lambda g, d, b: (b, d, 0, g)),
                   pl.BlockSpec((None, None, nch, sw), lambda g, d, b: (b, d, 0, g))],
        out_shape=[jax.ShapeDtypeStruct((nb, 2, gb, nch, fw), F32),
                   jax.ShapeDtypeStruct((nb, 2, nch, gb * sw), F32),
                   jax.ShapeDtypeStruct((nb, 2, nch, gb * sw), F32)],
        compiler_params=_cparams(("arbitrary", "arbitrary", "arbitrary"), V7X_VMEM_LIMIT),
        name="s5_local",
    )(u_fold, toep, minc)
    scan_w = 2 * sw
    full = pl.BlockSpec((None, None, nch, scan_w), lambda b, d, j: (b, d, 0, j))
    avec = pl.BlockSpec((None, 1, scan_w), lambda b, d, j: (d, 0, j))
    x0re, x0im = pl.pallas_call(
        functools.partial(_s5_scan_kernel, n_ctx_chunk=n_ctx_chunk),
        grid=(nb, 2, gb * sw // scan_w),
        in_specs=[full, full, avec, avec],
        out_specs=[full, full],
        out_shape=[jax.ShapeDtypeStruct((nb, 2, nch, gb * sw), F32)] * 2,
        compiler_params=_cparams(("arbitrary", "arbitrary", "arbitrary"), V7X_VMEM_LIMIT),
        name="s5_scan",
    )(xre, xim, a_re, a_im)
    xcol = pl.BlockSpec((None, 2, nch, sw), lambda g, b: (b, 0, 0, g))
    return pl.pallas_call(
        _s5_state_kernel,
        grid=(gb, nb),
        in_specs=[xcol, xcol,
                  pl.BlockSpec((2, None, fw, fw), lambda g, b: (0, g, 0, 0)),
                  pl.BlockSpec((None, 2, None, nch, fw), lambda g, b: (b, 0, g, 0, 0))],
        out_specs=pl.BlockSpec((None, None, nch, fw), lambda g, b: (b, g, 0, 0)),
        out_shape=jax.ShapeDtypeStruct((nb, gb, nch, fw), F32),
        compiler_params=_cparams(("arbitrary", "arbitrary"), V7X_VMEM_LIMIT),
        name="s5_state",
    )(x0re, x0im, mstate, yl)


def _gelu_tanh(v):
    return 0.5 * v * (1.0 + jnp.tanh(math.sqrt(2.0 / math.pi) * (v + 0.044715 * v * v * v)))


def _s5_post_kernel(y_ref, u_ref, d_ref, w_ref, o_ref, wbf_ref):
    @pl.when(pl.program_id(0) == 0)
    def _():
        wbf_ref[...] = w_ref[...].astype(BF16)

    t = _gelu_tanh(y_ref[...] + d_ref[...] * u_ref[...].astype(F32))
    o_ref[...] = (t * _sigmoid(_dot(t.astype(BF16), wbf_ref[...]))).astype(o_ref.dtype)


def s5_post(y, z, s5_d_l, w_glu, layer, *, u_col):
    ntok, c = y.shape
    row = pl.BlockSpec((ROW_TILE, c), lambda i: (i, 0))
    return pl.pallas_call(
        _s5_post_kernel,
        grid=(ntok // ROW_TILE,),
        in_specs=[row, pl.BlockSpec((ROW_TILE, c), lambda i: (i, u_col)),
                  pl.BlockSpec((1, c), lambda i: (0, 0)),
                  pl.BlockSpec((None, c, c), lambda i: (layer, 0, 0))],
        out_specs=row,
        out_shape=jax.ShapeDtypeStruct((ntok, c), BF16),
        scratch_shapes=[pltpu.VMEM((c, c), BF16)],
        compiler_params=_cparams(("arbitrary",), V7X_VMEM_LIMIT),
        name="s5_post",
    )(y, z, s5_d_l, w_glu)


def _merge_kernel(a_ref, b_ref, c_ref, s0_ref, s1_ref, s2_ref, wa_ref, wb_ref, wc_ref, o_ref,
                  wa_bf, wb_bf, wc_bf):
    @pl.when(pl.program_id(1) == 0)
    def _():
        wa_bf[...] = wa_ref[...].astype(BF16)
        wb_bf[...] = wb_ref[...].astype(BF16)
        wc_bf[...] = wc_ref[...].astype(BF16)

    m = _sigmoid(s0_ref[...].astype(F32)) * _dot(a_ref[...], wa_bf[...])
    m = m + _sigmoid(s1_ref[...].astype(F32)) * _dot(b_ref[...], wb_bf[...])
    m = m + _sigmoid(s2_ref[...].astype(F32)) * _dot(c_ref[...], wc_bf[...])
    o_ref[...] = m.astype(o_ref.dtype)


def merge_branches(act_a, act_b, act_c, z, conv_proj, ret_proj, s5_proj, layer, *, s_col0, tm, tn):
    m = act_a.shape[0]
    d = conv_proj.shape[2]
    ka, kb, kc = act_a.shape[1], act_b.shape[1], act_c.shape[1]
    nblk = d // tn

    def aspec(k):
        return pl.BlockSpec((tm, k), lambda j, i: (i, 0))

    def sspec(br):
        return pl.BlockSpec((tm, tn), lambda j, i: (i, s_col0 // tn + br * nblk + j))

    def wspec(k):
        return pl.BlockSpec((None, k, tn), lambda j, i: (layer, 0, j))

    return pl.pallas_call(
        _merge_kernel,
        grid=(nblk, m // tm),
        in_specs=[aspec(ka), aspec(kb), aspec(kc), sspec(0), sspec(1), sspec(2),
                  wspec(ka), wspec(kb), wspec(kc)],
        out_specs=pl.BlockSpec((tm, tn), lambda j, i: (i, j)),
        out_shape=jax.ShapeDtypeStruct((m, d), BF16),
        scratch_shapes=[pltpu.VMEM((ka, tn), BF16), pltpu.VMEM((kb, tn), BF16), pltpu.VMEM((kc, tn), BF16)],
        compiler_params=_cparams(("arbitrary", "arbitrary"), V7X_VMEM_LIMIT),
        name="merge_branches",
    )(act_a, act_b, act_c, z, z, z, conv_proj, ret_proj, s5_proj)


def _ffn_kernel(*refs, has_gate):
    if has_gate:
        x_ref, wg_ref, wu_ref, wd_ref, gate_ref, o_ref = refs
    else:
        x_ref, wg_ref, wu_ref, wd_ref, o_ref = refs
        gate_ref = None

    @pl.when(pl.program_id(1) == 0)
    def _():
        o_ref[...] = jnp.zeros(o_ref.shape, F32)

    x = x_ref[...]
    hg = _dot(x, wg_ref[...].astype(BF16))
    hu = _dot(x, wu_ref[...].astype(BF16))
    a = _silu(hg) * hu
    if has_gate:
        a = a * gate_ref[...]
    o_ref[...] += _dot(a.astype(BF16), wd_ref[...].astype(BF16))


def ffn_blocks(x, wg, wu, wd, wg_spec, wu_spec, wd_spec, n_blocks, gate, *, tm):
    m, d = x.shape
    args = [x, wg, wu, wd]
    specs = [pl.BlockSpec((tm, d), lambda i, e: (i, 0)), wg_spec, wu_spec, wd_spec]
    if gate is not None:
        args.append(gate)
        specs.append(pl.BlockSpec((None, tm, 1), lambda i, e: (e, i, 0)))
    return pl.pallas_call(
        functools.partial(_ffn_kernel, has_gate=gate is not None),
        grid=(m // tm, n_blocks),
        in_specs=specs,
        out_specs=pl.BlockSpec((tm, d), lambda i, e: (i, 0)),
        out_shape=jax.ShapeDtypeStruct((m, d), F32),
        compiler_params=_cparams(("arbitrary", "arbitrary"), V7X_VMEM_LIMIT),
        name="ffn_blocks",
    )(*args)


_HI = lax.Precision.HIGHEST


def _cmul(ar, ai, br, bi):
    return ar * br - ai * bi, ar * bi + ai * br


def s5_operators(a_re, a_im, log_dt, b_re, b_im, c_re, c_im):
    t = S5_T
    dt = jnp.exp(log_dt)[..., None]
    adt_re, adt_im = a_re * dt, a_im * dt
    tau = jnp.arange(t + 1, dtype=F32)[None, None, :, None]
    mag = jnp.exp(adt_re[:, :, None, :] * tau)
    ang = adt_im[:, :, None, :] * tau
    pw_re, pw_im = mag * jnp.cos(ang), mag * jnp.sin(ang)
    ab_re, ab_im = pw_re[:, :, 1], pw_im[:, :, 1]
    den = a_re * a_re + a_im * a_im
    nr, ni = ab_re - 1.0, ab_im
    f_re = (nr * a_re + ni * a_im) / den
    f_im = (ni * a_re - nr * a_im) / den
    bb_re, bb_im = _cmul(f_re[..., None], f_im[..., None], b_re, b_im)
    m1_re, m1_im = _cmul(pw_re[..., None], pw_im[..., None], bb_re[:, :, None], bb_im[:, :, None])
    kk = (jnp.einsum('dgpn,dgtnq->dgtpq', c_re, m1_re[:, :, :t], precision=_HI)
          - jnp.einsum('dgpn,dgtnq->dgtpq', c_im, m1_im[:, :, :t], precision=_HI))
    ti = jnp.arange(t)
    lag = ti[:, None] - ti[None, :]
    kg = kk[:, :, jnp.clip(lag, 0, t - 1)]
    kg = jnp.where((lag >= 0)[None, None, :, :, None, None], kg, 0.0)
    nd, g = a_re.shape[0], a_re.shape[1]
    p = b_re.shape[-1]
    n = a_re.shape[-1]
    inc = jnp.stack([m1_re[:, :, t - 1 - ti], m1_im[:, :, t - 1 - ti]], axis=3)
    w_re, w_im = _cmul(c_re[:, :, None], c_im[:, :, None],
                       pw_re[:, :, 1:, None, :], pw_im[:, :, 1:, None, :])
    mst = jnp.stack([w_re, -w_im], axis=2)

    def reverse_backward(v, axes):
        return jnp.stack([v[0], jnp.flip(v[1], axes)], axis=0)

    kg = reverse_backward(kg, (1, 2))
    inc = reverse_backward(inc, (1,))
    mst = reverse_backward(mst, (2,))
    gpb = S5_LANES // p
    gb = g // gpb
    eye = jnp.eye(gpb, dtype=BF16)
    fw = t * S5_LANES
    toep = jnp.einsum('dbgtspq,gh->dbsgqthp', kg.astype(BF16).reshape(nd, gb, gpb, t, t, p, p), eye)
    minc = jnp.einsum('dbgsknq,gh->dbsgqkhn', inc.astype(BF16).reshape(nd, gb, gpb, t, 2, n, p), eye)
    mstate = jnp.einsum('dbgktpn,gh->dbkgnthp', mst.astype(BF16).reshape(nd, gb, gpb, 2, t, p, n), eye)
    a_t_re = pw_re[:, :, t].reshape(nd, 1, g * n)
    a_t_im = pw_im[:, :, t].reshape(nd, 1, g * n)
    return (toep.reshape(nd, gb, fw, fw), minc.reshape(nd, gb, fw, 2 * gpb * n),
            mstate.reshape(nd, gb, 2 * gpb * n, fw), a_t_re, a_t_im)


def rotary_tables(n_pos, half):
    freq = ROPE_BASE ** (-jnp.arange(half, dtype=F32) / half)
    ang = jnp.arange(n_pos, dtype=F32)[:, None] * freq[None, :]
    return jnp.cos(ang), jnp.sin(ang)


def kernel(x, c, ctx, c_ctx, emb_ln_g, emb_ln_b, ada_w, ada_b, w_in, conv_w, conv_b, conv_ln_g, conv_ln_b, conv_proj, ret_decay_logit, ret_proj, s5_a_re, s5_a_im, s5_log_dt, s5_b_re, s5_b_im, s5_c_re, s5_c_im, s5_d, s5_w_glu, s5_proj, w_out, ln1_g, ln1_b, ln2_g, ln2_b, router_w, router_bias, exp_w_gate, exp_w_up, exp_w_down, sh_w_gate, sh_w_up, sh_w_down):
    n_batch, seq, d = x.shape
    lc = ctx.shape[1]
    depth = w_in.shape[0]
    conv_c = conv_w.shape[2]
    ret_w = ret_proj.shape[1]
    s5_c = s5_d.shape[1]
    s5_g = s5_c // S5_P
    n_exp, _, exp_ff = exp_w_gate.shape[1:]
    sh_ff = sh_w_gate.shape[2]
    rows = seq // GRID_W
    assert lc == ROW_TILE and seq % ROW_TILE == 0 and n_batch + 1 <= MOD_ROWS
    assert RET_HEADS * RET_DK == ret_w and S5_T * S5_LANES == 2 * (S5_LANES // S5_P) * S5_N
    assert s5_c % S5_LANES == 0 and lc % S5_T == 0 and rows % S5_T == 0
    tiles_per_batch = seq // ROW_TILE
    n_lat_tiles = n_batch * tiles_per_batch
    n_tiles = n_lat_tiles + n_batch
    n_lat = n_batch * seq
    alpha = (2.0 * depth) ** 0.25
    ntok = n_tiles * ROW_TILE
    tm_mm = _largest_row_tile(ntok, 2)
    tm_ffn = _largest_row_tile(ntok, 2) if ntok % 544 else 544
    col_a, col_q = 0, 2 * conv_c
    col_g = col_q + 3 * ret_w
    col_u = col_g + ret_w
    col_s = col_u + s5_c

    group_fn = functools.partial(_group_of_tile, n_latent_tiles=n_lat_tiles,
                                 tiles_per_batch=tiles_per_batch, n_batch=n_batch)

    tokens = jnp.concatenate([x.reshape(n_lat, d), ctx.reshape(n_batch * lc, d)], axis=0)
    cvec = jnp.concatenate([c, c_ctx[None, :], jnp.zeros((MOD_ROWS - n_batch - 1, d), F32)], axis=0)
    mods = [ada_modulation(cvec, ada_w, ada_b[i][None, :], i) for i in range(depth)]
    cos_tab, sin_tab = rotary_tables(lc + seq, RET_DK // 2)
    log_g = jax.nn.log_sigmoid(ret_decay_logit.astype(F32))

    xl, h = resid_ln_mod(tokens, [], None, 0, emb_ln_g[None, :], emb_ln_b[None, :], mods[0], 0, 1,
                         alpha=1.0, group_fn=group_fn, n_tiles=n_tiles)

    for i in range(depth):
        last = i == depth - 1
        mod = mods[i]
        z = matmul_stacked_w(h, w_in, i, BF16, tm_mm, 512)
        act_a = jnp.concatenate([
            conformer_conv_act(z, conv_w[i], conv_b[i][None, :], conv_ln_g[i][None, :], conv_ln_b[i][None, :],
                               row_tile0=0, n_tiles=n_lat_tiles, seg=GRID_W),
            conformer_conv_act(z, conv_w[i], conv_b[i][None, :], conv_ln_g[i][None, :], conv_ln_b[i][None, :],
                               row_tile0=n_lat_tiles, n_tiles=n_batch, seg=lc)], axis=0)
        o_f, o_b = retention_scan(z, log_g[i], cos_tab, sin_tab, n_batch=n_batch,
                                  tiles_per_batch=tiles_per_batch, q_col0=col_q // RET_DK)
        act_b = retention_post(o_f, o_b, z, g_col=col_g // ret_w)
        u = z[:, col_u:col_u + s5_c].astype(BF16)
        gb = s5_c // S5_LANES
        fw = S5_T * S5_LANES
        u_lat = u[:n_lat].reshape(n_batch, rows, GRID_W, gb, S5_LANES).transpose(0, 3, 2, 1, 4)
        u_ctx = u[n_lat:].reshape(n_batch, lc, gb, S5_LANES).transpose(0, 2, 1, 3)
        u_fold = jnp.concatenate([u_ctx.reshape(n_batch, gb, lc // S5_T, fw),
                                  u_lat.reshape(n_batch, gb, seq // S5_T, fw)], axis=2)
        ops = s5_operators(s5_a_re[i], s5_a_im[i], s5_log_dt[i], s5_b_re[i], s5_b_im[i], s5_c_re[i], s5_c_im[i])
        y_fold = s5_chunked(u_fold, *ops, n_ctx_chunk=lc // S5_T)
        y_ctx = y_fold[:, :, :lc // S5_T].reshape(n_batch, gb, lc, S5_LANES).transpose(0, 2, 1, 3)
        y_lat = y_fold[:, :, lc // S5_T:].reshape(n_batch, gb, GRID_W, rows, S5_LANES).transpose(0, 3, 2, 1, 4)
        y_s5 = jnp.concatenate([y_lat.reshape(n_lat, s5_c), y_ctx.reshape(n_batch * lc, s5_c)], axis=0)
        act_c = s5_post(y_s5, z, s5_d[i][None, :], s5_w_glu, i, u_col=col_u // s5_c)
        merged = merge_branches(act_a, act_b, act_c, z, conv_proj, ret_proj, s5_proj, i,
                                s_col0=col_s, tm=tm_mm, tn=512)
        y_mix = matmul_stacked_w(merged, w_out, i, F32, tm_mm, 512)
        xl, h2, gate_t = resid_ln_mod(xl, [y_mix], mod, 2, ln1_g[i][None, :], ln1_b[i][None, :], mod, 3, 4,
                                      alpha=alpha, group_fn=group_fn, n_tiles=n_tiles,
                                      router=(router_w[i].T, router_bias[i][:, None]))
        y_routed = ffn_blocks(
            h2, exp_w_gate, exp_w_up, exp_w_down,
            pl.BlockSpec((None, None, d, exp_ff), lambda r, e: (i, e, 0, 0)),
            pl.BlockSpec((None, None, d, exp_ff), lambda r, e: (i, e, 0, 0)),
            pl.BlockSpec((None, None, exp_ff, d), lambda r, e: (i, e, 0, 0)),
            n_exp, gate_t[:, :, None], tm=tm_ffn)
        y_shared = ffn_blocks(
            h2, sh_w_gate, sh_w_up, sh_w_down,
            pl.BlockSpec((None, d, exp_ff), lambda r, e: (i, 0, e)),
            pl.BlockSpec((None, d, exp_ff), lambda r, e: (i, 0, e)),
            pl.BlockSpec((None, exp_ff, d), lambda r, e: (i, e, 0)),
            sh_ff // exp_ff, None, tm=tm_ffn)
        if last:
            (xl,) = resid_ln_mod(xl, [y_routed, y_shared], mod, 5, ln2_g[i][None, :], ln2_b[i][None, :],
                                 None, 0, 0, alpha=alpha, group_fn=group_fn, n_tiles=n_lat_tiles)
        else:
            xl, h = resid_ln_mod(xl, [y_routed, y_shared], mod, 5, ln2_g[i][None, :], ln2_b[i][None, :],
                                 mods[i + 1], 0, 1, alpha=alpha, group_fn=group_fn, n_tiles=n_tiles)
    return xl.reshape(n_batch, seq, d)
```

```python
import functools
import math

import jax
import jax.numpy as jnp
from jax import lax
from jax.experimental import pallas as pl
from jax.experimental.pallas import tpu as pltpu

F32 = jnp.float32
BF16 = jnp.bfloat16

GRID_W = 64
RET_HEADS = 8
RET_DK = 256
S5_P = 16
S5_N = 64
ROPE_BASE = 10000.0
N_GROUPS = 8
TOPK_GROUPS = 4
TOP_K = 8
ROUTED_SCALE = 2.5
LN_EPS = 1e-5
HEAD_NORM_EPS = 1e-5
NEG_BIG = -1e30
N_BRANCH = 3

ROW_TILE = 256
S5_LANES = 128
S5_T = 8
MOD_ROWS = 8
MOE_ROW_TILE = 256
MOE_COMBINE_TILE = 64
V7X_VMEM_LIMIT = 56 * 1024 * 1024


def _largest_row_tile(ntok, max_tiles):
    n = ntok // ROW_TILE
    k = max(t for t in range(1, max_tiles + 1) if n % t == 0)
    return k * ROW_TILE


def _cparams(sem, vmem=None):
    return pltpu.CompilerParams(dimension_semantics=sem, vmem_limit_bytes=vmem)


def _split_bf16(v):
    hi = v.astype(BF16)
    lo = (v - hi.astype(F32)).astype(BF16)
    return hi, lo


def _dot(a, b):
    return jnp.dot(a, b, preferred_element_type=F32)


def _dot_nt(a, b):
    return lax.dot_general(a, b, (((1,), (1,)), ((), ())), preferred_element_type=F32)


def _dot_tn(a, b):
    return lax.dot_general(a, b, (((0,), (0,)), ((), ())), preferred_element_type=F32)


def _sigmoid(v):
    return 1.0 / (1.0 + jnp.exp(-v))


def _silu(v):
    return v * _sigmoid(v)


def _layer_norm_rows(v, g, b, eps):
    mu = jnp.mean(v, axis=-1, keepdims=True)
    vc = v - mu
    var = jnp.mean(vc * vc, axis=-1, keepdims=True)
    return vc * lax.rsqrt(var + eps) * g + b


def _ada_kernel(c_ref, w_ref, b_ref, o_ref):
    c = _silu(c_ref[...])
    ch, cl = _split_bf16(c)
    wh, wl = _split_bf16(w_ref[...])
    o_ref[...] = _dot(ch, wh) + _dot(ch, wl) + _dot(cl, wh) + b_ref[...]


def ada_modulation(cvec, ada_w, ada_b_l, layer):
    _, d, n = ada_w.shape
    tn = 512
    return pl.pallas_call(
        _ada_kernel,
        grid=(n // tn,),
        in_specs=[
            pl.BlockSpec((MOD_ROWS, d), lambda j: (0, 0)),
            pl.BlockSpec((None, d, tn), lambda j: (layer, 0, j)),
            pl.BlockSpec((1, tn), lambda j: (0, j)),
        ],
        out_specs=pl.BlockSpec((MOD_ROWS, tn), lambda j: (0, j)),
        out_shape=jax.ShapeDtypeStruct((MOD_ROWS, n), F32),
        compiler_params=_cparams(("arbitrary",), V7X_VMEM_LIMIT),
        name="ada_modulation",
    )(cvec, ada_w, ada_b_l)


def _group_of_tile(i, n_latent_tiles, tiles_per_batch, n_batch):
    return jnp.where(i < n_latent_tiles, i // tiles_per_batch, n_batch)


def _route(h, rwt_ref, rb_ref):
    hh, hl = _split_bf16(h)
    wh, wl = _split_bf16(rwt_ref[...])
    logits = _dot_nt(wh, hh) + _dot_nt(wh, hl) + _dot_nt(wl, hh)
    scores = _sigmoid(logits)
    sel = scores + rb_ref[...]
    n_e, tm = sel.shape
    per = n_e // N_GROUPS
    shape3 = (N_GROUPS, per, tm)
    sel3 = sel.reshape(shape3)
    io_e = lax.broadcasted_iota(jnp.int32, shape3, 1)
    io_g = lax.broadcasted_iota(jnp.int32, shape3, 0)
    m1 = jnp.max(sel3, axis=1, keepdims=True)
    first = jnp.min(jnp.where(sel3 == m1, io_e, per), axis=1, keepdims=True)
    m2 = jnp.max(jnp.where(io_e == first, -jnp.inf, sel3), axis=1, keepdims=True)
    work = m1 + m2
    iog1 = lax.broadcasted_iota(jnp.int32, work.shape, 0)
    gsel = jnp.zeros(work.shape, F32)
    for _ in range(TOPK_GROUPS):
        m = jnp.max(work, axis=0, keepdims=True)
        fi = jnp.min(jnp.where(work == m, iog1, N_GROUPS), axis=0, keepdims=True)
        hit = iog1 == fi
        gsel = jnp.where(hit, 1.0, gsel)
        work = jnp.where(hit, -jnp.inf, work)
    work = jnp.where(jnp.broadcast_to(gsel, shape3) > 0.0, sel3, NEG_BIG)
    flat = io_g * per + io_e
    scores3 = scores.reshape(shape3)
    picked, picked_score = [], []
    for _ in range(TOP_K):
        m = jnp.max(jnp.max(work, axis=1, keepdims=True), axis=0, keepdims=True)
        cand = jnp.where(work == m, flat, n_e)
        fi = jnp.min(jnp.min(cand, axis=1, keepdims=True), axis=0, keepdims=True)
        hit = flat == fi
        sk = jnp.sum(jnp.sum(jnp.where(hit, scores3, 0.0), axis=1, keepdims=True), axis=0, keepdims=True)
        picked.append(fi.reshape(1, tm))
        picked_score.append(sk.reshape(1, tm))
        work = jnp.where(hit, -jnp.inf, work)
    eidx = jnp.concatenate(picked, axis=0)
    w = jnp.concatenate(picked_score, axis=0)
    return eidx, ROUTED_SCALE * w / jnp.sum(w, axis=0, keepdims=True)


def _resid_ln_mod_kernel(*refs, alpha, n_y, has_mod, has_router, group_fn):
    it = iter(refs)
    x_ref = next(it)
    y_refs = [next(it) for _ in range(n_y)]
    gate_ref = next(it) if n_y else None
    g_ref, b_ref = next(it), next(it)
    shift_ref = scale_ref = rwt_ref = rb_ref = None
    if has_mod:
        shift_ref, scale_ref = next(it), next(it)
    if has_router:
        rwt_ref, rb_ref = next(it), next(it)
    xl_ref = next(it)
    h_ref = next(it) if has_mod else None
    hf_ref, eidx_ref, wk_ref = (next(it), next(it), next(it)) if has_router else (None, None, None)

    grp = group_fn(pl.program_id(0))
    v = x_ref[...]
    if n_y:
        y = y_refs[0][...]
        for r in y_refs[1:]:
            y = y + r[...]
        v = alpha * v + gate_ref[pl.ds(grp, 1), :] * y
    xl = _layer_norm_rows(v, g_ref[...], b_ref[...], LN_EPS)
    xl_ref[...] = xl
    if has_mod:
        h = xl * (1.0 + scale_ref[pl.ds(grp, 1), :]) + shift_ref[pl.ds(grp, 1), :]
        h_ref[...] = h.astype(h_ref.dtype)
        if has_router:
            hf_ref[...] = h
            eidx_ref[...], wk_ref[...] = _route(h, rwt_ref, rb_ref)


def resid_ln_mod(x, ys, gate_mod, gate_col, ln_g, ln_b, mod, shift_col, scale_col, *,
                 alpha, group_fn, n_tiles, router=None):
    d = x.shape[1]
    n_y = len(ys)
    has_mod = mod is not None
    has_router = router is not None
    row = pl.BlockSpec((ROW_TILE, d), lambda i: (i, 0))
    vec = pl.BlockSpec((1, d), lambda i: (0, 0))
    args, specs = [x], [row]
    for y in ys:
        args.append(y)
        specs.append(row)
    if n_y:
        args.append(gate_mod)
        specs.append(pl.BlockSpec((MOD_ROWS, d), lambda i: (0, gate_col)))
    args += [ln_g, ln_b]
    specs += [vec, vec]
    if has_mod:
        args += [mod, mod]
        specs += [pl.BlockSpec((MOD_ROWS, d), lambda i: (0, shift_col)),
                  pl.BlockSpec((MOD_ROWS, d), lambda i: (0, scale_col))]
    out_shapes = [jax.ShapeDtypeStruct((n_tiles * ROW_TILE, d), F32)]
    out_specs = [row]
    if has_mod:
        out_shapes.append(jax.ShapeDtypeStruct((n_tiles * ROW_TILE, d), BF16))
        out_specs.append(row)
    if has_router:
        rwt, rb = router
        n_e = rwt.shape[0]
        args += [rwt, rb]
        specs += [pl.BlockSpec((n_e, d), lambda i: (0, 0)), pl.BlockSpec((n_e, 1), lambda i: (0, 0))]
        out_shapes += [jax.ShapeDtypeStruct((n_tiles * ROW_TILE, d), F32),
                       jax.ShapeDtypeStruct((TOP_K, n_tiles * ROW_TILE), jnp.int32),
                       jax.ShapeDtypeStruct((TOP_K, n_tiles * ROW_TILE), F32)]
        out_specs += [row, pl.BlockSpec((TOP_K, ROW_TILE), lambda i: (0, i)),
                      pl.BlockSpec((TOP_K, ROW_TILE), lambda i: (0, i))]
    kern = functools.partial(_resid_ln_mod_kernel, alpha=alpha, n_y=n_y, has_mod=has_mod,
                             has_router=has_router, group_fn=group_fn)
    return pl.pallas_call(
        kern, grid=(n_tiles,), in_specs=specs, out_specs=out_specs, out_shape=out_shapes,
        compiler_params=_cparams(("arbitrary",), V7X_VMEM_LIMIT),
        name="resid_ln_mod",
    )(*args)


def _mm_kernel(x_ref, w_ref, o_ref, wbf_ref):
    @pl.when(pl.program_id(1) == 0)
    def _():
        wbf_ref[...] = w_ref[...].astype(BF16)

    o_ref[...] = _dot(x_ref[...], wbf_ref[...]).astype(o_ref.dtype)


def matmul_stacked_w(x, w, layer, out_dtype, tm, tn):
    m, k = x.shape
    n = w.shape[2]
    return pl.pallas_call(
        _mm_kernel,
        grid=(n // tn, m // tm),
        in_specs=[
            pl.BlockSpec((tm, k), lambda j, i: (i, 0)),
            pl.BlockSpec((None, k, tn), lambda j, i: (layer, 0, j)),
        ],
        out_specs=pl.BlockSpec((tm, tn), lambda j, i: (i, j)),
        out_shape=jax.ShapeDtypeStruct((m, n), out_dtype),
        scratch_shapes=[pltpu.VMEM((k, tn), BF16)],
        compiler_params=_cparams(("arbitrary", "arbitrary"), V7X_VMEM_LIMIT),
        name="matmul",
    )(x, w)


def _conv_kernel(a1_ref, a2_ref, w_ref, cb_ref, g_ref, b_ref, o_ref, pad_ref, y_ref, *, seg, n_tap):
    half = n_tap // 2
    front = ((half + 7) // 8) * 8
    nseg = ROW_TILE // seg
    c = a1_ref.shape[1]
    u = a1_ref[...].astype(F32) * _sigmoid(a2_ref[...].astype(F32))
    pad_ref[...] = jnp.zeros(pad_ref.shape, F32)
    for s in range(nseg):
        pad_ref[s, front:front + seg, :] = u[s * seg:(s + 1) * seg, :]
    lanes = 128

    def chunk(ci, carry):
        c0 = pl.multiple_of(ci * lanes, lanes)
        acc = jnp.zeros((nseg, seg, lanes), F32)
        for k in range(n_tap):
            off = front - half + k
            acc = acc + w_ref[k:k + 1, pl.ds(c0, lanes)] * pad_ref[:, off:off + seg, pl.ds(c0, lanes)]
        y_ref[:, pl.ds(c0, lanes)] = acc.reshape(ROW_TILE, lanes)
        return carry

    lax.fori_loop(0, c // lanes, chunk, 0)
    y = y_ref[...] + cb_ref[...]
    o_ref[...] = _silu(_layer_norm_rows(y, g_ref[...], b_ref[...], LN_EPS)).astype(o_ref.dtype)


def conformer_conv_act(z, conv_w_l, conv_b_l, ln_g_l, ln_b_l, *, row_tile0, n_tiles, seg):
    n_tap, c = conv_w_l.shape
    half = n_tap // 2
    front = ((half + 7) // 8) * 8
    nseg = ROW_TILE // seg
    vec = pl.BlockSpec((1, c), lambda i: (0, 0))
    kern = functools.partial(_conv_kernel, seg=seg, n_tap=n_tap)
    return pl.pallas_call(
        kern,
        grid=(n_tiles,),
        in_specs=[
            pl.BlockSpec((ROW_TILE, c), lambda i: (row_tile0 + i, 0)),
            pl.BlockSpec((ROW_TILE, c), lambda i: (row_tile0 + i, 1)),
            pl.BlockSpec((n_tap, c), lambda i: (0, 0)),
            vec, vec, vec,
        ],
        out_specs=pl.BlockSpec((ROW_TILE, c), lambda i: (i, 0)),
        out_shape=jax.ShapeDtypeStruct((n_tiles * ROW_TILE, c), BF16),
        scratch_shapes=[pltpu.VMEM((nseg, seg + 2 * front, c), F32), pltpu.VMEM((ROW_TILE, c), F32)],
        compiler_params=_cparams(("arbitrary",), V7X_VMEM_LIMIT),
        name="conformer_conv",
    )(z, z, conv_w_l, conv_b_l, ln_g_l, ln_b_l)


def _rotary(t, cos, sin):
    half = t.shape[1] // 2
    t1, t2 = t[:, :half], t[:, half:]
    return jnp.concatenate([t1 * cos - t2 * sin, t1 * sin + t2 * cos], axis=1)


def _retention_kernel(logg_ref, qf_ref, kf_ref, vf_ref, cf_ref, sf_ref,
                      qb_ref, kb_ref, vb_ref, cb_ref, sb_ref,
                      of_ref, ob_ref, state_ref):
    h = pl.program_id(1)
    n = pl.program_id(2)
    c = ROW_TILE
    k_scale = RET_DK ** -0.5

    @pl.when(n == 0)
    def _():
        state_ref[...] = jnp.zeros(state_ref.shape, F32)

    row = lax.broadcasted_iota(jnp.int32, (c, c), 0)
    col = lax.broadcasted_iota(jnp.int32, (c, c), 1)
    ridx = lax.broadcasted_iota(jnp.int32, (c, 1), 0).astype(F32)

    def one_direction(d, q_ref, k_ref, v_ref, cos_ref, sin_ref, o_ref):
        lg = logg_ref[d, h]
        cos, sin = cos_ref[...], sin_ref[...]
        q = _rotary(q_ref[...].astype(F32), cos, sin)
        k = _rotary(k_ref[...].astype(F32), cos, sin) * k_scale
        v = v_ref[...].astype(BF16)
        dist = (row - col) if d == 0 else (col - row)
        decay = jnp.where(dist >= 0, jnp.exp(lg * jnp.maximum(dist, 0).astype(F32)), 0.0)
        qb = q.astype(BF16)
        scores = _dot_nt(qb, k.astype(BF16)) * decay
        inner = _dot(scores.astype(BF16), v)
        to_prev = (ridx + 1.0) if d == 0 else (c - ridx)
        to_end = (c - 1.0 - ridx) if d == 0 else ridx
        s_prev = state_ref[d]
        cross = _dot(qb, s_prev.astype(BF16)) * jnp.exp(lg * to_prev)
        o_ref[...] = inner + cross
        kw = (k * jnp.exp(lg * to_end)).astype(BF16)
        state_ref[d] = jnp.exp(lg * jnp.full((1, 1), float(c), F32)) * s_prev + _dot_tn(kw, v)

    one_direction(0, qf_ref, kf_ref, vf_ref, cf_ref, sf_ref, of_ref)
    one_direction(1, qb_ref, kb_ref, vb_ref, cb_ref, sb_ref, ob_ref)


def retention_scan(z, log_g, cos_tab, sin_tab, *, n_batch, tiles_per_batch, q_col0):
    ntok = z.shape[0]
    n_lat = n_batch * tiles_per_batch
    n_steps = tiles_per_batch + 1
    dk = RET_DK
    hh = RET_HEADS

    def row_f(b, n):
        return jnp.where(n == 0, n_lat + b, b * tiles_per_batch + n - 1)

    def row_b(b, n):
        return jnp.where(n == 0, n_lat + b, b * tiles_per_batch + tiles_per_batch - n)

    def pos_f(n):
        return n

    def pos_b(n):
        return jnp.where(n == 0, 0, tiles_per_batch + 1 - n)

    def zspec(rowfn, sec):
        return pl.BlockSpec((ROW_TILE, dk), lambda b, h, n: (rowfn(b, n), q_col0 + sec * hh + h))

    def tspec(posfn):
        return pl.BlockSpec((ROW_TILE, dk // 2), lambda b, h, n: (posfn(n), 0))

    def ospec(rowfn):
        return pl.BlockSpec((ROW_TILE, dk), lambda b, h, n: (rowfn(b, n), h))

    smem = pl.BlockSpec(memory_space=pltpu.SMEM)
    return pl.pallas_call(
        _retention_kernel,
        grid=(n_batch, hh, n_steps),
        in_specs=[smem,
                  zspec(row_f, 0), zspec(row_f, 1), zspec(row_f, 2), tspec(pos_f), tspec(pos_f),
                  zspec(row_b, 0), zspec(row_b, 1), zspec(row_b, 2), tspec(pos_b), tspec(pos_b)],
        out_specs=[ospec(row_f), ospec(row_b)],
        out_shape=[jax.ShapeDtypeStruct((ntok, hh * dk), F32)] * 2,
        scratch_shapes=[pltpu.VMEM((2, dk, dk), F32)],
        compiler_params=_cparams(("arbitrary", "arbitrary", "arbitrary"), V7X_VMEM_LIMIT),
        name="retention_scan",
    )(log_g, z, z, z, cos_tab, sin_tab, z, z, z, cos_tab, sin_tab)


def _ret_post_kernel(of_ref, ob_ref, g_ref, o_ref):
    dk = RET_DK
    for h in range(RET_HEADS):
        sl = slice(h * dk, (h + 1) * dk)
        o = of_ref[:, sl] + ob_ref[:, sl]
        mu = jnp.mean(o, axis=-1, keepdims=True)
        oc = o - mu
        var = jnp.mean(oc * oc, axis=-1, keepdims=True)
        on = oc * lax.rsqrt(var + HEAD_NORM_EPS)
        o_ref[:, sl] = (on * _silu(g_ref[:, sl].astype(F32))).astype(o_ref.dtype)


def retention_post(o_f, o_b, z, *, g_col):
    ntok, w = o_f.shape
    row = pl.BlockSpec((ROW_TILE, w), lambda i: (i, 0))
    return pl.pallas_call(
        _ret_post_kernel,
        grid=(ntok // ROW_TILE,),
        in_specs=[row, row, pl.BlockSpec((ROW_TILE, w), lambda i: (i, g_col))],
        out_specs=row,
        out_shape=jax.ShapeDtypeStruct((ntok, w), BF16),
        compiler_params=_cparams(("arbitrary",), V7X_VMEM_LIMIT),
        name="retention_post",
    )(o_f, o_b, z)


def _s5_local_kernel(u_ref, toep_ref, minc_ref, yl_ref, xre_ref, xim_ref):
    u = u_ref[...]
    yl_ref[...] = _dot(u, toep_ref[...])
    xi = _dot(u, minc_ref[...])
    half = xi.shape[1] // 2
    xre_ref[...] = xi[:, :half]
    xim_ref[...] = xi[:, half:]


def _s5_scan_kernel(xre_ref, xim_ref, are_ref, aim_ref, ore_ref, oim_ref, *, n_ctx_chunk):
    n_chunk = xre_ref.shape[0]
    backward = pl.program_id(1) == 1
    ar, ai = are_ref[...], aim_ref[...]

    def step(i, carry):
        sr, si = carry
        rev = jnp.where(i < n_ctx_chunk, n_ctx_chunk - 1 - i, n_chunk + n_ctx_chunk - 1 - i)
        c = jnp.where(backward, rev, i)
        ore_ref[pl.ds(c, 1), :] = sr
        oim_ref[pl.ds(c, 1), :] = si
        nr = ar * sr - ai * si + xre_ref[pl.ds(c, 1), :]
        ni = ar * si + ai * sr + xim_ref[pl.ds(c, 1), :]
        return nr, ni

    zero = jnp.zeros(ar.shape, F32)
    lax.fori_loop(0, n_chunk, step, (zero, zero))


def _s5_state_kernel(xre_ref, xim_ref, mst_ref, yl_ref, y_ref):
    y = yl_ref[0] + yl_ref[1]
    for d in range(2):
        x0 = jnp.concatenate([xre_ref[d], xim_ref[d]], axis=1).astype(BF16)
        y = y + _dot(x0, mst_ref[d])
    y_ref[...] = y


def s5_chunked(u_fold, toep, minc, mstate, a_re, a_im, *, n_ctx_chunk):
    nb, gb, nch, fw = u_fold.shape
    sw = fw // 2
    op = pl.BlockSpec((None, None, fw, fw), lambda g, d, b: (d, g, 0, 0))
    yl, xre, xim = pl.pallas_call(
        _s5_local_kernel,
        grid=(gb, 2, nb),
        in_specs=[pl.BlockSpec((None, None, nch, fw), lambda g, d, b: (b, g, 0, 0)), op, op],
        out_specs=[pl.BlockSpec((None, None, None, nch, fw), lambda g, d, b: (b, d, g, 0, 0)),
                   pl.BlockSpec((None, None, nch, sw), lambda g, d, b: (b, d, 0, g)),
                   pl.BlockSpec((None, None, nch, sw), lambda g, d, b: (b, d, 0, g))],
        out_shape=[jax.ShapeDtypeStruct((nb, 2, gb, nch, fw), F32),
                   jax.ShapeDtypeStruct((nb, 2, nch, gb * sw), F32),
                   jax.ShapeDtypeStruct((nb, 2, nch, gb * sw), F32)],
        compiler_params=_cparams(("arbitrary", "arbitrary", "arbitrary"), V7X_VMEM_LIMIT),
        name="s5_local",
    )(u_fold, toep, minc)
    scan_w = 2 * sw
    full = pl.BlockSpec((None, None, nch, scan_w), lambda b, d, j: (b, d, 0, j))
    avec = pl.BlockSpec((None, 1, scan_w), lambda b, d, j: (d, 0, j))
    x0re, x0im = pl.pallas_call(
        functools.partial(_s5_scan_kernel, n_ctx_chunk=n_ctx_chunk),
        grid=(nb, 2, gb * sw // scan_w),
        in_specs=[full, full, avec, avec],
        out_specs=[full, full],
        out_shape=[jax.ShapeDtypeStruct((nb, 2, nch, gb * sw), F32)] * 2,
        compiler_params=_cparams(("arbitrary", "arbitrary", "arbitrary"), V7X_VMEM_LIMIT),
        name="s5_scan",
    )(xre, xim, a_re, a_im)
    xcol = pl.BlockSpec((None, 2, nch, sw), lambda g, b: (b, 0, 0, g))
    return pl.pallas_call(
        _s5_state_kernel,
        grid=(gb, nb),
        in_specs=[xcol, xcol,
                  pl.BlockSpec((2, None, fw, fw), lambda g, b: (0, g, 0, 0)),
                  pl.BlockSpec((None, 2, None, nch, fw), lambda g, b: (b, 0, g, 0, 0))],
        out_specs=pl.BlockSpec((None, None, nch, fw), lambda g, b: (b, g, 0, 0)),
        out_shape=jax.ShapeDtypeStruct((nb, gb, nch, fw), F32),
        compiler_params=_cparams(("arbitrary", "arbitrary"), V7X_VMEM_LIMIT),
        name="s5_state",
    )(x0re, x0im, mstate, yl)


def _gelu_tanh(v):
    return 0.5 * v * (1.0 + jnp.tanh(math.sqrt(2.0 / math.pi) * (v + 0.044715 * v * v * v)))


def _s5_post_kernel(y_ref, u_ref, d_ref, w_ref, o_ref, wbf_ref):
    @pl.when(pl.program_id(0) == 0)
    def _():
        wbf_ref[...] = w_ref[...].astype(BF16)

    t = _gelu_tanh(y_ref[...] + d_ref[...] * u_ref[...].astype(F32))
    o_ref[...] = (t * _sigmoid(_dot(t.astype(BF16), wbf_ref[...]))).astype(o_ref.dtype)


def s5_post(y, z, s5_d_l, w_glu, layer, *, u_col):
    ntok, c = y.shape
    row = pl.BlockSpec((ROW_TILE, c), lambda i: (i, 0))
    return pl.pallas_call(
        _s5_post_kernel,
        grid=(ntok // ROW_TILE,),
        in_specs=[row, pl.BlockSpec((ROW_TILE, c), lambda i: (i, u_col)),
                  pl.BlockSpec((1, c), lambda i: (0, 0)),
                  pl.BlockSpec((None, c, c), lambda i: (layer, 0, 0))],
        out_specs=row,
        out_shape=jax.ShapeDtypeStruct((ntok, c), BF16),
        scratch_shapes=[pltpu.VMEM((c, c), BF16)],
        compiler_params=_cparams(("arbitrary",), V7X_VMEM_LIMIT),
        name="s5_post",
    )(y, z, s5_d_l, w_glu)


def _merge_kernel(a_ref, b_ref, c_ref, s0_ref, s1_ref, s2_ref, wa_ref, wb_ref, wc_ref, o_ref,
                  wa_bf, wb_bf, wc_bf):
    @pl.when(pl.program_id(1) == 0)
    def _():
        wa_bf[...] = wa_ref[...].astype(BF16)
        wb_bf[...] = wb_ref[...].astype(BF16)
        wc_bf[...] = wc_ref[...].astype(BF16)

    m = _sigmoid(s0_ref[...].astype(F32)) * _dot(a_ref[...], wa_bf[...])
    m = m + _sigmoid(s1_ref[...].astype(F32)) * _dot(b_ref[...], wb_bf[...])
    m = m + _sigmoid(s2_ref[...].astype(F32)) * _dot(c_ref[...], wc_bf[...])
    o_ref[...] = m.astype(o_ref.dtype)


def merge_branches(act_a, act_b, act_c, z, conv_proj, ret_proj, s5_proj, layer, *, s_col0, tm, tn):
    m = act_a.shape[0]
    d = conv_proj.shape[2]
    ka, kb, kc = act_a.shape[1], act_b.shape[1], act_c.shape[1]
    nblk = d // tn

    def aspec(k):
        return pl.BlockSpec((tm, k), lambda j, i: (i, 0))

    def sspec(br):
        return pl.BlockSpec((tm, tn), lambda j, i: (i, s_col0 // tn + br * nblk + j))

    def wspec(k):
        return pl.BlockSpec((None, k, tn), lambda j, i: (layer, 0, j))

    return pl.pallas_call(
        _merge_kernel,
        grid=(nblk, m // tm),
        in_specs=[aspec(ka), aspec(kb), aspec(kc), sspec(0), sspec(1), sspec(2),
                  wspec(ka), wspec(kb), wspec(kc)],
        out_specs=pl.BlockSpec((tm, tn), lambda j, i: (i, j)),
        out_shape=jax.ShapeDtypeStruct((m, d), BF16),
        scratch_shapes=[pltpu.VMEM((ka, tn), BF16), pltpu.VMEM((kb, tn), BF16), pltpu.VMEM((kc, tn), BF16)],
        compiler_params=_cparams(("arbitrary", "arbitrary"), V7X_VMEM_LIMIT),
        name="merge_branches",
    )(act_a, act_b, act_c, z, z, z, conv_proj, ret_proj, s5_proj)


def _ffn_kernel(x_ref, wg_ref, wu_ref, wd_ref, o_ref):
    @pl.when(pl.program_id(1) == 0)
    def _():
        o_ref[...] = jnp.zeros(o_ref.shape, F32)

    x = x_ref[...]
    hg = _dot(x, wg_ref[...].astype(BF16))
    hu = _dot(x, wu_ref[...].astype(BF16))
    o_ref[...] += _dot((_silu(hg) * hu).astype(BF16), wd_ref[...].astype(BF16))


def ffn_blocks(x, wg, wu, wd, layer, ff_block, *, tm):
    m, d = x.shape
    ff = wg.shape[2]
    up = pl.BlockSpec((None, d, ff_block), lambda i, e: (layer, 0, e))
    return pl.pallas_call(
        _ffn_kernel,
        grid=(m // tm, ff // ff_block),
        in_specs=[pl.BlockSpec((tm, d), lambda i, e: (i, 0)), up, up,
                  pl.BlockSpec((None, ff_block, d), lambda i, e: (layer, e, 0))],
        out_specs=pl.BlockSpec((tm, d), lambda i, e: (i, 0)),
        out_shape=jax.ShapeDtypeStruct((m, d), F32),
        compiler_params=_cparams(("arbitrary", "arbitrary"), V7X_VMEM_LIMIT),
        name="ffn_blocks",
    )(x, wg, wu, wd)


def _moe_positions_kernel(eidx_ref, off_ref, pos_ref, run_ref):
    @pl.when(pl.program_id(0) == 0)
    def _():
        run_ref[...] = jnp.zeros(run_ref.shape, F32)

    n_e = off_ref.shape[0]
    top_k, tm = eidx_ref.shape
    expert = lax.broadcasted_iota(jnp.int32, (n_e, tm), 0)
    eidx = eidx_ref[...]
    member = jnp.zeros((n_e, tm), F32)
    for k in range(top_k):
        member = member + jnp.where(eidx[k:k + 1, :] == expert, 1.0, 0.0)
    r = lax.broadcasted_iota(jnp.int32, (tm, tm), 0)
    c = lax.broadcasted_iota(jnp.int32, (tm, tm), 1)
    upper = jnp.where(r <= c, 1.0, 0.0).astype(BF16)
    incl = _dot(member.astype(BF16), upper)
    row_of = off_ref[...] + run_ref[...] + incl - member
    rows = [jnp.sum(jnp.where(eidx[k:k + 1, :] == expert, row_of, 0.0), axis=0, keepdims=True)
            for k in range(top_k)]
    pos_ref[...] = jnp.concatenate(rows, axis=0).astype(jnp.int32)
    run_ref[...] = run_ref[...] + jnp.sum(member, axis=1, keepdims=True)


def moe_positions(eidx, offsets):
    top_k, ntok = eidx.shape
    n_e = offsets.shape[0]
    blk = pl.BlockSpec((top_k, ROW_TILE), lambda i: (0, i))
    return pl.pallas_call(
        _moe_positions_kernel,
        grid=(ntok // ROW_TILE,),
        in_specs=[blk, pl.BlockSpec((n_e, 1), lambda i: (0, 0))],
        out_specs=blk,
        out_shape=jax.ShapeDtypeStruct((top_k, ntok), jnp.int32),
        scratch_shapes=[pltpu.VMEM((n_e, 1), F32)],
        compiler_params=_cparams(("arbitrary",), V7X_VMEM_LIMIT),
        name="moe_positions",
    )(eidx, offsets)


def _row_copy(src_hbm, src_row, buf, slot, dst_row, sem):
    return pltpu.make_async_copy(src_hbm.at[pl.ds(src_row, 1), :],
                                 buf.at[slot, pl.ds(dst_row, 1), :], sem.at[slot])


def _gather_rows_start(idx_ref, src_hbm, buf, slot, sem, n_rows):
    def body(r, carry):
        _row_copy(src_hbm, idx_ref[0, r], buf, slot, r, sem).start()
        return carry

    lax.fori_loop(0, n_rows, body, 0)


def _gather_rows_wait(src_hbm, buf, slot, sem, n_rows):
    def body(r, carry):
        _row_copy(src_hbm, 0, buf, slot, r, sem).wait()
        return carry

    lax.fori_loop(0, n_rows, body, 0)


def _moe_ffn_kernel(te_ref, nv_ref, idx_ref, idx_next_ref, h_hbm, wg_ref, wu_ref, wd_ref, o_ref,
                    xbuf, sem, wg_bf, wu_bf, wd_bf):
    j = pl.program_id(0)
    n_valid = nv_ref[0]
    slot = j % 2
    tm = xbuf.shape[1]

    @pl.when(j == 0)
    def _():
        _gather_rows_start(idx_ref, h_hbm, xbuf, 0, sem, tm)

    @pl.when(j + 1 < n_valid)
    def _():
        _gather_rows_start(idx_next_ref, h_hbm, xbuf, 1 - slot, sem, tm)

    @pl.when(jnp.logical_or(j == 0, te_ref[j] != te_ref[jnp.maximum(j - 1, 0)]))
    def _():
        wg_bf[...] = wg_ref[...].astype(BF16)
        wu_bf[...] = wu_ref[...].astype(BF16)
        wd_bf[...] = wd_ref[...].astype(BF16)

    @pl.when(j < n_valid)
    def _():
        _gather_rows_wait(h_hbm, xbuf, slot, sem, tm)
        x = xbuf[slot].astype(BF16)
        hg = _dot(x, wg_bf[...])
        hu = _dot(x, wu_bf[...])
        o_ref[...] = _dot((_silu(hg) * hu).astype(BF16), wd_bf[...])

    @pl.when(j >= n_valid)
    def _():
        o_ref[...] = jnp.zeros(o_ref.shape, F32)


def moe_ffn_sorted(h, src_tok, tile_expert, n_valid, wg, wu, wd, layer, *, tm):
    n_tiles = src_tok.shape[0]
    d = h.shape[1]
    ff = wg.shape[3]
    smem_rows = functools.partial(pl.BlockSpec, (None, 1, tm), memory_space=pltpu.SMEM)
    grid_spec = pltpu.PrefetchScalarGridSpec(
        num_scalar_prefetch=2,
        grid=(n_tiles,),
        in_specs=[
            smem_rows(lambda j, te, nv: (j, 0, 0)),
            smem_rows(lambda j, te, nv: (jnp.minimum(j + 1, n_tiles - 1), 0, 0)),
            pl.BlockSpec(memory_space=pl.ANY),
            pl.BlockSpec((None, None, d, ff), lambda j, te, nv: (layer, te[j], 0, 0)),
            pl.BlockSpec((None, None, d, ff), lambda j, te, nv: (layer, te[j], 0, 0)),
            pl.BlockSpec((None, None, ff, d), lambda j, te, nv: (layer, te[j], 0, 0)),
        ],
        out_specs=pl.BlockSpec((tm, d), lambda j, te, nv: (j, 0)),
        scratch_shapes=[pltpu.VMEM((2, tm, d), F32), pltpu.SemaphoreType.DMA((2,)),
                        pltpu.VMEM((d, ff), BF16), pltpu.VMEM((d, ff), BF16), pltpu.VMEM((ff, d), BF16)],
    )
    return pl.pallas_call(
        _moe_ffn_kernel,
        grid_spec=grid_spec,
        out_shape=jax.ShapeDtypeStruct((n_tiles * tm, d), F32),
        compiler_params=_cparams(("arbitrary",), V7X_VMEM_LIMIT),
        name="moe_ffn_sorted",
    )(tile_expert, n_valid, src_tok, src_tok, h, wg, wu, wd)


def _moe_combine_kernel(pos_ref, pos_next_ref, w_ref, ys_hbm, o_ref, gbuf, sem):
    i = pl.program_id(0)
    n = pl.num_programs(0)
    slot = i % 2
    top_k, tc = gbuf.shape[1], gbuf.shape[2]

    def start(p_ref, s):
        for k in range(top_k):
            def body(r, carry, k=k):
                pltpu.make_async_copy(ys_hbm.at[pl.ds(p_ref[k, r], 1), :],
                                      gbuf.at[s, k, pl.ds(r, 1), :], sem.at[s]).start()
                return carry

            lax.fori_loop(0, tc, body, 0)

    @pl.when(i == 0)
    def _():
        start(pos_ref, 0)

    @pl.when(i + 1 < n)
    def _():
        start(pos_next_ref, 1 - slot)

    def wait_body(r, carry):
        pltpu.make_async_copy(ys_hbm.at[pl.ds(0, 1), :], gbuf.at[slot, 0, pl.ds(0, 1), :], sem.at[slot]).wait()
        return carry

    lax.fori_loop(0, top_k * tc, wait_body, 0)
    w = w_ref[...]
    acc = w[:, 0:1] * gbuf[slot, 0]
    for k in range(1, top_k):
        acc = acc + w[:, k:k + 1] * gbuf[slot, k]
    o_ref[...] = acc


def moe_combine(ys, pos_tiles, w_tok, *, tc):
    n_tiles, top_k, _ = pos_tiles.shape
    d = ys.shape[1]
    smem_pos = functools.partial(pl.BlockSpec, (None, top_k, tc), memory_space=pltpu.SMEM)
    return pl.pallas_call(
        _moe_combine_kernel,
        grid=(n_tiles,),
        in_specs=[smem_pos(lambda i: (i, 0, 0)),
                  smem_pos(lambda i: (jnp.minimum(i + 1, n_tiles - 1), 0, 0)),
                  pl.BlockSpec((tc, top_k), lambda i: (i, 0)),
                  pl.BlockSpec(memory_space=pl.ANY)],
        out_specs=pl.BlockSpec((tc, d), lambda i: (i, 0)),
        out_shape=jax.ShapeDtypeStruct((n_tiles * tc, d), F32),
        scratch_shapes=[pltpu.VMEM((2, top_k, tc, d), F32), pltpu.SemaphoreType.DMA((2,))],
        compiler_params=_cparams(("arbitrary",), V7X_VMEM_LIMIT),
        name="moe_combine",
    )(pos_tiles, pos_tiles, w_tok, ys)


_HI = lax.Precision.HIGHEST


def _cmul(ar, ai, br, bi):
    return ar * br - ai * bi, ar * bi + ai * br


def s5_operators(a_re, a_im, log_dt, b_re, b_im, c_re, c_im):
    t = S5_T
    dt = jnp.exp(log_dt)[..., None]
    adt_re, adt_im = a_re * dt, a_im * dt
    tau = jnp.arange(t + 1, dtype=F32)[None, None, :, None]
    mag = jnp.exp(adt_re[:, :, None, :] * tau)
    ang = adt_im[:, :, None, :] * tau
    pw_re, pw_im = mag * jnp.cos(ang), mag * jnp.sin(ang)
    ab_re, ab_im = pw_re[:, :, 1], pw_im[:, :, 1]
    den = a_re * a_re + a_im * a_im
    nr, ni = ab_re - 1.0, ab_im
    f_re = (nr * a_re + ni * a_im) / den
    f_im = (ni * a_re - nr * a_im) / den
    bb_re, bb_im = _cmul(f_re[..., None], f_im[..., None], b_re, b_im)
    m1_re, m1_im = _cmul(pw_re[..., None], pw_im[..., None], bb_re[:, :, None], bb_im[:, :, None])
    kk = (jnp.einsum('dgpn,dgtnq->dgtpq', c_re, m1_re[:, :, :t], precision=_HI)
          - jnp.einsum('dgpn,dgtnq->dgtpq', c_im, m1_im[:, :, :t], precision=_HI))
    ti = jnp.arange(t)
    lag = ti[:, None] - ti[None, :]
    kg = kk[:, :, jnp.clip(lag, 0, t - 1)]
    kg = jnp.where((lag >= 0)[None, None, :, :, None, None], kg, 0.0)
    nd, g = a_re.shape[0], a_re.shape[1]
    p = b_re.shape[-1]
    n = a_re.shape[-1]
    inc = jnp.stack([m1_re[:, :, t - 1 - ti], m1_im[:, :, t - 1 - ti]], axis=3)
    w_re, w_im = _cmul(c_re[:, :, None], c_im[:, :, None],
                       pw_re[:, :, 1:, None, :], pw_im[:, :, 1:, None, :])
    mst = jnp.stack([w_re, -w_im], axis=2)

    def reverse_backward(v, axes):
        return jnp.stack([v[0], jnp.flip(v[1], axes)], axis=0)

    kg = reverse_backward(kg, (1, 2))
    inc = reverse_backward(inc, (1,))
    mst = reverse_backward(mst, (2,))
    gpb = S5_LANES // p
    gb = g // gpb
    eye = jnp.eye(gpb, dtype=BF16)
    fw = t * S5_LANES
    toep = jnp.einsum('dbgtspq,gh->dbsgqthp', kg.astype(BF16).reshape(nd, gb, gpb, t, t, p, p), eye)
    minc = jnp.einsum('dbgsknq,gh->dbsgqkhn', inc.astype(BF16).reshape(nd, gb, gpb, t, 2, n, p), eye)
    mstate = jnp.einsum('dbgktpn,gh->dbkgnthp', mst.astype(BF16).reshape(nd, gb, gpb, 2, t, p, n), eye)
    a_t_re = pw_re[:, :, t].reshape(nd, 1, g * n)
    a_t_im = pw_im[:, :, t].reshape(nd, 1, g * n)
    return (toep.reshape(nd, gb, fw, fw), minc.reshape(nd, gb, fw, 2 * gpb * n),
            mstate.reshape(nd, gb, 2 * gpb * n, fw), a_t_re, a_t_im)


def rotary_tables(n_pos, half):
    freq = ROPE_BASE ** (-jnp.arange(half, dtype=F32) / half)
    ang = jnp.arange(n_pos, dtype=F32)[:, None] * freq[None, :]
    return jnp.cos(ang), jnp.sin(ang)


def kernel(x, c, ctx, c_ctx, emb_ln_g, emb_ln_b, ada_w, ada_b, w_in, conv_w, conv_b, conv_ln_g, conv_ln_b, conv_proj, ret_decay_logit, ret_proj, s5_a_re, s5_a_im, s5_log_dt, s5_b_re, s5_b_im, s5_c_re, s5_c_im, s5_d, s5_w_glu, s5_proj, w_out, ln1_g, ln1_b, ln2_g, ln2_b, router_w, router_bias, exp_w_gate, exp_w_up, exp_w_down, sh_w_gate, sh_w_up, sh_w_down):
    n_batch, seq, d = x.shape
    lc = ctx.shape[1]
    depth = w_in.shape[0]
    conv_c = conv_w.shape[2]
    ret_w = ret_proj.shape[1]
    s5_c = s5_d.shape[1]
    s5_g = s5_c // S5_P
    n_exp, _, exp_ff = exp_w_gate.shape[1:]
    sh_ff = sh_w_gate.shape[2]
    rows = seq // GRID_W
    assert lc == ROW_TILE and seq % ROW_TILE == 0 and n_batch + 1 <= MOD_ROWS
    assert RET_HEADS * RET_DK == ret_w and S5_T * S5_LANES == 2 * (S5_LANES // S5_P) * S5_N
    assert s5_c % S5_LANES == 0 and lc % S5_T == 0 and rows % S5_T == 0
    tiles_per_batch = seq // ROW_TILE
    n_lat_tiles = n_batch * tiles_per_batch
    n_tiles = n_lat_tiles + n_batch
    n_lat = n_batch * seq
    alpha = (2.0 * depth) ** 0.25
    ntok = n_tiles * ROW_TILE
    tm_mm = _largest_row_tile(ntok, 2)
    tm_ffn = _largest_row_tile(ntok, 2) if ntok % 544 else 544
    col_a, col_q = 0, 2 * conv_c
    col_g = col_q + 3 * ret_w
    col_u = col_g + ret_w
    col_s = col_u + s5_c

    group_fn = functools.partial(_group_of_tile, n_latent_tiles=n_lat_tiles,
                                 tiles_per_batch=tiles_per_batch, n_batch=n_batch)

    tokens = jnp.concatenate([x.reshape(n_lat, d), ctx.reshape(n_batch * lc, d)], axis=0)
    cvec = jnp.concatenate([c, c_ctx[None, :], jnp.zeros((MOD_ROWS - n_batch - 1, d), F32)], axis=0)
    mods = [ada_modulation(cvec, ada_w, ada_b[i][None, :], i) for i in range(depth)]
    cos_tab, sin_tab = rotary_tables(lc + seq, RET_DK // 2)
    log_g = jax.nn.log_sigmoid(ret_decay_logit.astype(F32))

    xl, h = resid_ln_mod(tokens, [], None, 0, emb_ln_g[None, :], emb_ln_b[None, :], mods[0], 0, 1,
                         alpha=1.0, group_fn=group_fn, n_tiles=n_tiles)

    for i in range(depth):
        last = i == depth - 1
        mod = mods[i]
        z = matmul_stacked_w(h, w_in, i, BF16, tm_mm, 512)
        act_a = jnp.concatenate([
            conformer_conv_act(z, conv_w[i], conv_b[i][None, :], conv_ln_g[i][None, :], conv_ln_b[i][None, :],
                               row_tile0=0, n_tiles=n_lat_tiles, seg=GRID_W),
            conformer_conv_act(z, conv_w[i], conv_b[i][None, :], conv_ln_g[i][None, :], conv_ln_b[i][None, :],
                               row_tile0=n_lat_tiles, n_tiles=n_batch, seg=lc)], axis=0)
        o_f, o_b = retention_scan(z, log_g[i], cos_tab, sin_tab, n_batch=n_batch,
                                  tiles_per_batch=tiles_per_batch, q_col0=col_q // RET_DK)
        act_b = retention_post(o_f, o_b, z, g_col=col_g // ret_w)
        u = z[:, col_u:col_u + s5_c].astype(BF16)
        gb = s5_c // S5_LANES
        fw = S5_T * S5_LANES
        u_lat = u[:n_lat].reshape(n_batch, rows, GRID_W, gb, S5_LANES).transpose(0, 3, 2, 1, 4)
        u_ctx = u[n_lat:].reshape(n_batch, lc, gb, S5_LANES).transpose(0, 2, 1, 3)
        u_fold = jnp.concatenate([u_ctx.reshape(n_batch, gb, lc // S5_T, fw),
                                  u_lat.reshape(n_batch, gb, seq // S5_T, fw)], axis=2)
        ops = s5_operators(s5_a_re[i], s5_a_im[i], s5_log_dt[i], s5_b_re[i], s5_b_im[i], s5_c_re[i], s5_c_im[i])
        y_fold = s5_chunked(u_fold, *ops, n_ctx_chunk=lc // S5_T)
        y_ctx = y_fold[:, :, :lc // S5_T].reshape(n_batch, gb, lc, S5_LANES).transpose(0, 2, 1, 3)
        y_lat = y_fold[:, :, lc // S5_T:].reshape(n_batch, gb, GRID_W, rows, S5_LANES).transpose(0, 3, 2, 1, 4)
        y_s5 = jnp.concatenate([y_lat.reshape(n_lat, s5_c), y_ctx.reshape(n_batch * lc, s5_c)], axis=0)
        act_c = s5_post(y_s5, z, s5_d[i][None, :], s5_w_glu, i, u_col=col_u // s5_c)
        merged = merge_branches(act_a, act_b, act_c, z, conv_proj, ret_proj, s5_proj, i,
                                s_col0=col_s, tm=tm_mm, tn=512)
        y_mix = matmul_stacked_w(merged, w_out, i, F32, tm_mm, 512)
        xl, h2, h2_f32, eidx, wk = resid_ln_mod(
            xl, [y_mix], mod, 2, ln1_g[i][None, :], ln1_b[i][None, :], mod, 3, 4,
            alpha=alpha, group_fn=group_fn, n_tiles=n_tiles,
            router=(router_w[i].T, router_bias[i][:, None]))
        counts = jnp.sum((eidx[None, :, :] == jnp.arange(n_exp, dtype=jnp.int32)[:, None, None]).astype(jnp.int32),
                         axis=(1, 2))
        padded = ((counts + MOE_ROW_TILE - 1) // MOE_ROW_TILE) * MOE_ROW_TILE
        ends = jnp.cumsum(padded)
        n_sorted_tiles = (ntok * TOP_K) // MOE_ROW_TILE + n_exp
        tile_expert = jnp.minimum(jnp.searchsorted(ends, jnp.arange(n_sorted_tiles, dtype=jnp.int32) * MOE_ROW_TILE,
                                                   side='right'), n_exp - 1).astype(jnp.int32)
        n_valid = (ends[-1:] // MOE_ROW_TILE).astype(jnp.int32)
        pos = moe_positions(eidx, (ends - padded).astype(F32)[:, None])
        src_tok = jnp.zeros((n_sorted_tiles * MOE_ROW_TILE,), jnp.int32).at[pos.reshape(-1)].set(
            jnp.tile(jnp.arange(ntok, dtype=jnp.int32), TOP_K))
        y_sorted = moe_ffn_sorted(h2_f32, src_tok.reshape(n_sorted_tiles, 1, MOE_ROW_TILE), tile_expert, n_valid,
                                  exp_w_gate, exp_w_up, exp_w_down, i, tm=MOE_ROW_TILE)
        pos_tiles = pos.reshape(TOP_K, ntok // MOE_COMBINE_TILE, MOE_COMBINE_TILE).transpose(1, 0, 2)
        y_routed = moe_combine(y_sorted, pos_tiles, wk.T, tc=MOE_COMBINE_TILE)
        y_shared = ffn_blocks(h2, sh_w_gate, sh_w_up, sh_w_down, i, exp_ff, tm=tm_ffn)
        if last:
            (xl,) = resid_ln_mod(xl, [y_routed, y_shared], mod, 5, ln2_g[i][None, :], ln2_b[i][None, :],
                                 None, 0, 0, alpha=alpha, group_fn=group_fn, n_tiles=n_lat_tiles)
        else:
            xl, h = resid_ln_mod(xl, [y_routed, y_shared], mod, 5, ln2_g[i][None, :], ln2_b[i][None, :],
                                 mods[i + 1], 0, 1, alpha=alpha, group_fn=group_fn, n_tiles=n_tiles)
    return xl.reshape(n_batch, seq, d)
```

```python
import functools
import math

import jax
import jax.numpy as jnp
from jax import lax
from jax.experimental import pallas as pl
from jax.experimental.pallas import tpu as pltpu

F32 = jnp.float32
BF16 = jnp.bfloat16

GRID_W = 64
RET_HEADS = 8
RET_DK = 256
S5_P = 16
S5_N = 64
ROPE_BASE = 10000.0
N_GROUPS = 8
TOPK_GROUPS = 4
TOP_K = 8
ROUTED_SCALE = 2.5
LN_EPS = 1e-5
HEAD_NORM_EPS = 1e-5
NEG_BIG = -1e30
N_BRANCH = 3

ROW_TILE = 256
S5_LANES = 128
S5_T = 8
MOD_ROWS = 8
MOE_ROW_TILE = 256
MOE_COMBINE_TILE = 64
LANES = 128
ROW_PITCH = 40
V7X_VMEM_LIMIT = 56 * 1024 * 1024


def _largest_row_tile(ntok, max_tiles):
    n = ntok // ROW_TILE
    k = max(t for t in range(1, max_tiles + 1) if n % t == 0)
    return k * ROW_TILE


def _cparams(sem, vmem=None):
    return pltpu.CompilerParams(dimension_semantics=sem, vmem_limit_bytes=vmem)


def _split_bf16(v):
    hi = v.astype(BF16)
    lo = (v - hi.astype(F32)).astype(BF16)
    return hi, lo


def _dot(a, b):
    return jnp.dot(a, b, preferred_element_type=F32)


def _dot_nt(a, b):
    return lax.dot_general(a, b, (((1,), (1,)), ((), ())), preferred_element_type=F32)


def _dot_tn(a, b):
    return lax.dot_general(a, b, (((0,), (0,)), ((), ())), preferred_element_type=F32)


def _sigmoid(v):
    return 1.0 / (1.0 + jnp.exp(-v))


def _silu(v):
    return v * _sigmoid(v)


def _store_pitched(ref, v):
    n, width = v.shape
    k = width // LANES
    for c in range(k):
        ref[pl.ds(c, n, stride=ROW_PITCH), :] = v[:, c * LANES:(c + 1) * LANES]
    for c in range(k, ROW_PITCH):
        ref[pl.ds(c, n, stride=ROW_PITCH), :] = jnp.zeros((n, LANES), v.dtype)


def _load_pitched(ref, n, k):
    return jnp.concatenate([ref[pl.ds(c, n, stride=ROW_PITCH), :] for c in range(k)], axis=1)


def _layer_norm_rows(v, g, b, eps):
    mu = jnp.mean(v, axis=-1, keepdims=True)
    vc = v - mu
    var = jnp.mean(vc * vc, axis=-1, keepdims=True)
    return vc * lax.rsqrt(var + eps) * g + b


def _ada_kernel(c_ref, w_ref, b_ref, o_ref):
    c = _silu(c_ref[...])
    ch, cl = _split_bf16(c)
    wh, wl = _split_bf16(w_ref[...])
    o_ref[...] = _dot(ch, wh) + _dot(ch, wl) + _dot(cl, wh) + b_ref[...]


def ada_modulation(cvec, ada_w, ada_b_l, layer):
    _, d, n = ada_w.shape
    tn = 512
    return pl.pallas_call(
        _ada_kernel,
        grid=(n // tn,),
        in_specs=[
            pl.BlockSpec((MOD_ROWS, d), lambda j: (0, 0)),
            pl.BlockSpec((None, d, tn), lambda j: (layer, 0, j)),
            pl.BlockSpec((1, tn), lambda j: (0, j)),
        ],
        out_specs=pl.BlockSpec((MOD_ROWS, tn), lambda j: (0, j)),
        out_shape=jax.ShapeDtypeStruct((MOD_ROWS, n), F32),
        compiler_params=_cparams(("arbitrary",), V7X_VMEM_LIMIT),
        name="ada_modulation",
    )(cvec, ada_w, ada_b_l)


def _group_of_tile(i, n_latent_tiles, tiles_per_batch, n_batch):
    return jnp.where(i < n_latent_tiles, i // tiles_per_batch, n_batch)


def _route(h, rwt_ref, rb_ref):
    hh, hl = _split_bf16(h)
    wh, wl = _split_bf16(rwt_ref[...])
    logits = _dot_nt(wh, hh) + _dot_nt(wh, hl) + _dot_nt(wl, hh)
    scores = _sigmoid(logits)
    sel = scores + rb_ref[...]
    n_e, tm = sel.shape
    per = n_e // N_GROUPS
    shape3 = (N_GROUPS, per, tm)
    sel3 = sel.reshape(shape3)
    io_e = lax.broadcasted_iota(jnp.int32, shape3, 1)
    io_g = lax.broadcasted_iota(jnp.int32, shape3, 0)
    m1 = jnp.max(sel3, axis=1, keepdims=True)
    first = jnp.min(jnp.where(sel3 == m1, io_e, per), axis=1, keepdims=True)
    m2 = jnp.max(jnp.where(io_e == first, -jnp.inf, sel3), axis=1, keepdims=True)
    work = m1 + m2
    iog1 = lax.broadcasted_iota(jnp.int32, work.shape, 0)
    gsel = jnp.zeros(work.shape, F32)
    for _ in range(TOPK_GROUPS):
        m = jnp.max(work, axis=0, keepdims=True)
        fi = jnp.min(jnp.where(work == m, iog1, N_GROUPS), axis=0, keepdims=True)
        hit = iog1 == fi
        gsel = jnp.where(hit, 1.0, gsel)
        work = jnp.where(hit, -jnp.inf, work)
    work = jnp.where(jnp.broadcast_to(gsel, shape3) > 0.0, sel3, NEG_BIG)
    flat = io_g * per + io_e
    scores3 = scores.reshape(shape3)
    picked, picked_score = [], []
    for _ in range(TOP_K):
        m = jnp.max(jnp.max(work, axis=1, keepdims=True), axis=0, keepdims=True)
        cand = jnp.where(work == m, flat, n_e)
        fi = jnp.min(jnp.min(cand, axis=1, keepdims=True), axis=0, keepdims=True)
        hit = flat == fi
        sk = jnp.sum(jnp.sum(jnp.where(hit, scores3, 0.0), axis=1, keepdims=True), axis=0, keepdims=True)
        picked.append(fi.reshape(1, tm))
        picked_score.append(sk.reshape(1, tm))
        work = jnp.where(hit, -jnp.inf, work)
    eidx = jnp.concatenate(picked, axis=0)
    w = jnp.concatenate(picked_score, axis=0)
    return eidx, ROUTED_SCALE * w / jnp.sum(w, axis=0, keepdims=True)


def _resid_ln_mod_kernel(*refs, alpha, n_y, has_mod, has_router, group_fn):
    it = iter(refs)
    x_ref = next(it)
    y_refs = [next(it) for _ in range(n_y)]
    gate_ref = next(it) if n_y else None
    g_ref, b_ref = next(it), next(it)
    shift_ref = scale_ref = rwt_ref = rb_ref = None
    if has_mod:
        shift_ref, scale_ref = next(it), next(it)
    if has_router:
        rwt_ref, rb_ref = next(it), next(it)
    xl_ref = next(it)
    h_ref = next(it) if has_mod else None
    hf_ref, eidx_ref, wk_ref = (next(it), next(it), next(it)) if has_router else (None, None, None)

    grp = group_fn(pl.program_id(0))
    v = x_ref[...]
    if n_y:
        y = y_refs[0][...]
        for r in y_refs[1:]:
            y = y + r[...]
        v = alpha * v + gate_ref[pl.ds(grp, 1), :] * y
    xl = _layer_norm_rows(v, g_ref[...], b_ref[...], LN_EPS)
    xl_ref[...] = xl
    if has_mod:
        h = xl * (1.0 + scale_ref[pl.ds(grp, 1), :]) + shift_ref[pl.ds(grp, 1), :]
        h_ref[...] = h.astype(h_ref.dtype)
        if has_router:
            _store_pitched(hf_ref, h)
            eidx_ref[...], wk_ref[...] = _route(h, rwt_ref, rb_ref)


def resid_ln_mod(x, ys, gate_mod, gate_col, ln_g, ln_b, mod, shift_col, scale_col, *,
                 alpha, group_fn, n_tiles, router=None):
    d = x.shape[1]
    n_y = len(ys)
    has_mod = mod is not None
    has_router = router is not None
    row = pl.BlockSpec((ROW_TILE, d), lambda i: (i, 0))
    vec = pl.BlockSpec((1, d), lambda i: (0, 0))
    args, specs = [x], [row]
    for y in ys:
        args.append(y)
        specs.append(row)
    if n_y:
        args.append(gate_mod)
        specs.append(pl.BlockSpec((MOD_ROWS, d), lambda i: (0, gate_col)))
    args += [ln_g, ln_b]
    specs += [vec, vec]
    if has_mod:
        args += [mod, mod]
        specs += [pl.BlockSpec((MOD_ROWS, d), lambda i: (0, shift_col)),
                  pl.BlockSpec((MOD_ROWS, d), lambda i: (0, scale_col))]
    out_shapes = [jax.ShapeDtypeStruct((n_tiles * ROW_TILE, d), F32)]
    out_specs = [row]
    if has_mod:
        out_shapes.append(jax.ShapeDtypeStruct((n_tiles * ROW_TILE, d), BF16))
        out_specs.append(row)
    if has_router:
        rwt, rb = router
        n_e = rwt.shape[0]
        args += [rwt, rb]
        specs += [pl.BlockSpec((n_e, d), lambda i: (0, 0)), pl.BlockSpec((n_e, 1), lambda i: (0, 0))]
        assert d <= LANES * ROW_PITCH
        out_shapes += [jax.ShapeDtypeStruct((n_tiles * ROW_TILE * ROW_PITCH, LANES), F32),
                       jax.ShapeDtypeStruct((TOP_K, n_tiles * ROW_TILE), jnp.int32),
                       jax.ShapeDtypeStruct((TOP_K, n_tiles * ROW_TILE), F32)]
        out_specs += [pl.BlockSpec((ROW_TILE * ROW_PITCH, LANES), lambda i: (i, 0)),
                      pl.BlockSpec((TOP_K, ROW_TILE), lambda i: (0, i)),
                      pl.BlockSpec((TOP_K, ROW_TILE), lambda i: (0, i))]
    kern = functools.partial(_resid_ln_mod_kernel, alpha=alpha, n_y=n_y, has_mod=has_mod,
                             has_router=has_router, group_fn=group_fn)
    return pl.pallas_call(
        kern, grid=(n_tiles,), in_specs=specs, out_specs=out_specs, out_shape=out_shapes,
        compiler_params=_cparams(("arbitrary",), V7X_VMEM_LIMIT),
        name="resid_ln_mod",
    )(*args)


def _mm_kernel(x_ref, w_ref, o_ref, wbf_ref):
    @pl.when(pl.program_id(1) == 0)
    def _():
        wbf_ref[...] = w_ref[...].astype(BF16)

    o_ref[...] = _dot(x_ref[...], wbf_ref[...]).astype(o_ref.dtype)


def matmul_stacked_w(x, w, layer, out_dtype, tm, tn):
    m, k = x.shape
    n = w.shape[2]
    return pl.pallas_call(
        _mm_kernel,
        grid=(n // tn, m // tm),
        in_specs=[
            pl.BlockSpec((tm, k), lambda j, i: (i, 0)),
            pl.BlockSpec((None, k, tn), lambda j, i: (layer, 0, j)),
        ],
        out_specs=pl.BlockSpec((tm, tn), lambda j, i: (i, j)),
        out_shape=jax.ShapeDtypeStruct((m, n), out_dtype),
        scratch_shapes=[pltpu.VMEM((k, tn), BF16)],
        compiler_params=_cparams(("arbitrary", "arbitrary"), V7X_VMEM_LIMIT),
        name="matmul",
    )(x, w)


def _conv_kernel(a1_ref, a2_ref, w_ref, cb_ref, g_ref, b_ref, o_ref, pad_ref, y_ref, *, seg, n_tap):
    half = n_tap // 2
    front = ((half + 7) // 8) * 8
    nseg = ROW_TILE // seg
    c = a1_ref.shape[1]
    u = a1_ref[...].astype(F32) * _sigmoid(a2_ref[...].astype(F32))
    pad_ref[...] = jnp.zeros(pad_ref.shape, F32)
    for s in range(nseg):
        pad_ref[s, front:front + seg, :] = u[s * seg:(s + 1) * seg, :]
    lanes = 128

    def chunk(ci, carry):
        c0 = pl.multiple_of(ci * lanes, lanes)
        acc = jnp.zeros((nseg, seg, lanes), F32)
        for k in range(n_tap):
            off = front - half + k
            acc = acc + w_ref[k:k + 1, pl.ds(c0, lanes)] * pad_ref[:, off:off + seg, pl.ds(c0, lanes)]
        y_ref[:, pl.ds(c0, lanes)] = acc.reshape(ROW_TILE, lanes)
        return carry

    lax.fori_loop(0, c // lanes, chunk, 0)
    y = y_ref[...] + cb_ref[...]
    o_ref[...] = _silu(_layer_norm_rows(y, g_ref[...], b_ref[...], LN_EPS)).astype(o_ref.dtype)


def conformer_conv_act(z, conv_w_l, conv_b_l, ln_g_l, ln_b_l, *, row_tile0, n_tiles, seg):
    n_tap, c = conv_w_l.shape
    half = n_tap // 2
    front = ((half + 7) // 8) * 8
    nseg = ROW_TILE // seg
    vec = pl.BlockSpec((1, c), lambda i: (0, 0))
    kern = functools.partial(_conv_kernel, seg=seg, n_tap=n_tap)
    return pl.pallas_call(
        kern,
        grid=(n_tiles,),
        in_specs=[
            pl.BlockSpec((ROW_TILE, c), lambda i: (row_tile0 + i, 0)),
            pl.BlockSpec((ROW_TILE, c), lambda i: (row_tile0 + i, 1)),
            pl.BlockSpec((n_tap, c), lambda i: (0, 0)),
            vec, vec, vec,
        ],
        out_specs=pl.BlockSpec((ROW_TILE, c), lambda i: (i, 0)),
        out_shape=jax.ShapeDtypeStruct((n_tiles * ROW_TILE, c), BF16),
        scratch_shapes=[pltpu.VMEM((nseg, seg + 2 * front, c), F32), pltpu.VMEM((ROW_TILE, c), F32)],
        compiler_params=_cparams(("arbitrary",), V7X_VMEM_LIMIT),
        name="conformer_conv",
    )(z, z, conv_w_l, conv_b_l, ln_g_l, ln_b_l)


def _rotary(t, cos, sin):
    half = t.shape[1] // 2
    t1, t2 = t[:, :half], t[:, half:]
    return jnp.concatenate([t1 * cos - t2 * sin, t1 * sin + t2 * cos], axis=1)


def _retention_kernel(logg_ref, qf_ref, kf_ref, vf_ref, cf_ref, sf_ref,
                      qb_ref, kb_ref, vb_ref, cb_ref, sb_ref,
                      of_ref, ob_ref, state_ref):
    h = pl.program_id(1)
    n = pl.program_id(2)
    c = ROW_TILE
    k_scale = RET_DK ** -0.5

    @pl.when(n == 0)
    def _():
        state_ref[...] = jnp.zeros(state_ref.shape, F32)

    row = lax.broadcasted_iota(jnp.int32, (c, c), 0)
    col = lax.broadcasted_iota(jnp.int32, (c, c), 1)
    ridx = lax.broadcasted_iota(jnp.int32, (c, 1), 0).astype(F32)

    def one_direction(d, q_ref, k_ref, v_ref, cos_ref, sin_ref, o_ref):
        lg = logg_ref[d, h]
        cos, sin = cos_ref[...], sin_ref[...]
        q = _rotary(q_ref[...].astype(F32), cos, sin)
        k = _rotary(k_ref[...].astype(F32), cos, sin) * k_scale
        v = v_ref[...].astype(BF16)
        dist = (row - col) if d == 0 else (col - row)
        decay = jnp.where(dist >= 0, jnp.exp(lg * jnp.maximum(dist, 0).astype(F32)), 0.0)
        qb = q.astype(BF16)
        scores = _dot_nt(qb, k.astype(BF16)) * decay
        inner = _dot(scores.astype(BF16), v)
        to_prev = (ridx + 1.0) if d == 0 else (c - ridx)
        to_end = (c - 1.0 - ridx) if d == 0 else ridx
        s_prev = state_ref[d]
        cross = _dot(qb, s_prev.astype(BF16)) * jnp.exp(lg * to_prev)
        o_ref[...] = inner + cross
        kw = (k * jnp.exp(lg * to_end)).astype(BF16)
        state_ref[d] = jnp.exp(lg * jnp.full((1, 1), float(c), F32)) * s_prev + _dot_tn(kw, v)

    one_direction(0, qf_ref, kf_ref, vf_ref, cf_ref, sf_ref, of_ref)
    one_direction(1, qb_ref, kb_ref, vb_ref, cb_ref, sb_ref, ob_ref)


def retention_scan(z, log_g, cos_tab, sin_tab, *, n_batch, tiles_per_batch, q_col0):
    ntok = z.shape[0]
    n_lat = n_batch * tiles_per_batch
    n_steps = tiles_per_batch + 1
    dk = RET_DK
    hh = RET_HEADS

    def row_f(b, n):
        return jnp.where(n == 0, n_lat + b, b * tiles_per_batch + n - 1)

    def row_b(b, n):
        return jnp.where(n == 0, n_lat + b, b * tiles_per_batch + tiles_per_batch - n)

    def pos_f(n):
        return n

    def pos_b(n):
        return jnp.where(n == 0, 0, tiles_per_batch + 1 - n)

    def zspec(rowfn, sec):
        return pl.BlockSpec((ROW_TILE, dk), lambda b, h, n: (rowfn(b, n), q_col0 + sec * hh + h))

    def tspec(posfn):
        return pl.BlockSpec((ROW_TILE, dk // 2), lambda b, h, n: (posfn(n), 0))

    def ospec(rowfn):
        return pl.BlockSpec((ROW_TILE, dk), lambda b, h, n: (rowfn(b, n), h))

    smem = pl.BlockSpec(memory_space=pltpu.SMEM)
    return pl.pallas_call(
        _retention_kernel,
        grid=(n_batch, hh, n_steps),
        in_specs=[smem,
                  zspec(row_f, 0), zspec(row_f, 1), zspec(row_f, 2), tspec(pos_f), tspec(pos_f),
                  zspec(row_b, 0), zspec(row_b, 1), zspec(row_b, 2), tspec(pos_b), tspec(pos_b)],
        out_specs=[ospec(row_f), ospec(row_b)],
        out_shape=[jax.ShapeDtypeStruct((ntok, hh * dk), F32)] * 2,
        scratch_shapes=[pltpu.VMEM((2, dk, dk), F32)],
        compiler_params=_cparams(("arbitrary", "arbitrary", "arbitrary"), V7X_VMEM_LIMIT),
        name="retention_scan",
    )(log_g, z, z, z, cos_tab, sin_tab, z, z, z, cos_tab, sin_tab)


def _ret_post_kernel(of_ref, ob_ref, g_ref, o_ref):
    dk = RET_DK
    for h in range(RET_HEADS):
        sl = slice(h * dk, (h + 1) * dk)
        o = of_ref[:, sl] + ob_ref[:, sl]
        mu = jnp.mean(o, axis=-1, keepdims=True)
        oc = o - mu
        var = jnp.mean(oc * oc, axis=-1, keepdims=True)
        on = oc * lax.rsqrt(var + HEAD_NORM_EPS)
        o_ref[:, sl] = (on * _silu(g_ref[:, sl].astype(F32))).astype(o_ref.dtype)


def retention_post(o_f, o_b, z, *, g_col):
    ntok, w = o_f.shape
    row = pl.BlockSpec((ROW_TILE, w), lambda i: (i, 0))
    return pl.pallas_call(
        _ret_post_kernel,
        grid=(ntok // ROW_TILE,),
        in_specs=[row, row, pl.BlockSpec((ROW_TILE, w), lambda i: (i, g_col))],
        out_specs=row,
        out_shape=jax.ShapeDtypeStruct((ntok, w), BF16),
        compiler_params=_cparams(("arbitrary",), V7X_VMEM_LIMIT),
        name="retention_post",
    )(o_f, o_b, z)


def _s5_local_kernel(u_ref, toep_ref, minc_ref, yl_ref, xre_ref, xim_ref):
    u = u_ref[...]
    yl_ref[...] = _dot(u, toep_ref[...])
    xi = _dot(u, minc_ref[...])
    half = xi.shape[1] // 2
    xre_ref[...] = xi[:, :half]
    xim_ref[...] = xi[:, half:]


def _s5_scan_kernel(xre_ref, xim_ref, are_ref, aim_ref, ore_ref, oim_ref, *, n_ctx_chunk):
    n_chunk = xre_ref.shape[0]
    backward = pl.program_id(1) == 1
    ar, ai = are_ref[...], aim_ref[...]

    def step(i, carry):
        sr, si = carry
        rev = jnp.where(i < n_ctx_chunk, n_ctx_chunk - 1 - i, n_chunk + n_ctx_chunk - 1 - i)
        c = jnp.where(backward, rev, i)
        ore_ref[pl.ds(c, 1), :] = sr
        oim_ref[pl.ds(c, 1), :] = si
        nr = ar * sr - ai * si + xre_ref[pl.ds(c, 1), :]
        ni = ar * si + ai * sr + xim_ref[pl.ds(c, 1), :]
        return nr, ni

    zero = jnp.zeros(ar.shape, F32)
    lax.fori_loop(0, n_chunk, step, (zero, zero))


def _s5_state_kernel(xre_ref, xim_ref, mst_ref, yl_ref, y_ref):
    y = yl_ref[0] + yl_ref[1]
    for d in range(2):
        x0 = jnp.concatenate([xre_ref[d], xim_ref[d]], axis=1).astype(BF16)
        y = y + _dot(x0, mst_ref[d])
    y_ref[...] = y


def s5_chunked(u_fold, toep, minc, mstate, a_re, a_im, *, n_ctx_chunk):
    nb, gb, nch, fw = u_fold.shape
    sw = fw // 2
    op = pl.BlockSpec((None, None, fw, fw), lambda g, d, b: (d, g, 0, 0))
    yl, xre, xim = pl.pallas_call(
        _s5_local_kernel,
        grid=(gb, 2, nb),
        in_specs=[pl.BlockSpec((None, None, nch, fw), lambda g, d, b: (b, g, 0, 0)), op, op],
        out_specs=[pl.BlockSpec((None, None, None, nch, fw), lambda g, d, b: (b, d, g, 0, 0)),
                   pl.BlockSpec((None, None, nch, sw), lambda g, d, b: (b, d, 0, g)),
                   pl.BlockSpec((None, None, nch, sw), lambda g, d, b: (b, d, 0, g))],
        out_shape=[jax.ShapeDtypeStruct((nb, 2, gb, nch, fw), F32),
                   jax.ShapeDtypeStruct((nb, 2, nch, gb * sw), F32),
                   jax.ShapeDtypeStruct((nb, 2, nch, gb * sw), F32)],
        compiler_params=_cparams(("arbitrary", "arbitrary", "arbitrary"), V7X_VMEM_LIMIT),
        name="s5_local",
    )(u_fold, toep, minc)
    scan_w = 2 * sw
    full = pl.BlockSpec((None, None, nch, scan_w), lambda b, d, j: (b, d, 0, j))
    avec = pl.BlockSpec((None, 1, scan_w), lambda b, d, j: (d, 0, j))
    x0re, x0im = pl.pallas_call(
        functools.partial(_s5_scan_kernel, n_ctx_chunk=n_ctx_chunk),
        grid=(nb, 2, gb * sw // scan_w),
        in_specs=[full, full, avec, avec],
        out_specs=[full, full],
        out_shape=[jax.ShapeDtypeStruct((nb, 2, nch, gb * sw), F32)] * 2,
        compiler_params=_cparams(("arbitrary", "arbitrary", "arbitrary"), V7X_VMEM_LIMIT),
        name="s5_scan",
    )(xre, xim, a_re, a_im)
    xcol = pl.BlockSpec((None, 2, nch, sw), lambda g, b: (b, 0, 0, g))
    return pl.pallas_call(
        _s5_state_kernel,
        grid=(gb, nb),
        in_specs=[xcol, xcol,
                  pl.BlockSpec((2, None, fw, fw), lambda g, b: (0, g, 0, 0)),
                  pl.BlockSpec((None, 2, None, nch, fw), lambda g, b: (b, 0, g, 0, 0))],
        out_specs=pl.BlockSpec((None, None, nch, fw), lambda g, b: (b, g, 0, 0)),
        out_shape=jax.ShapeDtypeStruct((nb, gb, nch, fw), F32),
        compiler_params=_cparams(("arbitrary", "arbitrary"), V7X_VMEM_LIMIT),
        name="s5_state",
    )(x0re, x0im, mstate, yl)


def _gelu_tanh(v):
    return 0.5 * v * (1.0 + jnp.tanh(math.sqrt(2.0 / math.pi) * (v + 0.044715 * v * v * v)))


def _s5_post_kernel(y_ref, u_ref, d_ref, w_ref, o_ref, wbf_ref):
    @pl.when(pl.program_id(0) == 0)
    def _():
        wbf_ref[...] = w_ref[...].astype(BF16)

    t = _gelu_tanh(y_ref[...] + d_ref[...] * u_ref[...].astype(F32))
    o_ref[...] = (t * _sigmoid(_dot(t.astype(BF16), wbf_ref[...]))).astype(o_ref.dtype)


def s5_post(y, z, s5_d_l, w_glu, layer, *, u_col):
    ntok, c = y.shape
    row = pl.BlockSpec((ROW_TILE, c), lambda i: (i, 0))
    return pl.pallas_call(
        _s5_post_kernel,
        grid=(ntok // ROW_TILE,),
        in_specs=[row, pl.BlockSpec((ROW_TILE, c), lambda i: (i, u_col)),
                  pl.BlockSpec((1, c), lambda i: (0, 0)),
                  pl.BlockSpec((None, c, c), lambda i: (layer, 0, 0))],
        out_specs=row,
        out_shape=jax.ShapeDtypeStruct((ntok, c), BF16),
        scratch_shapes=[pltpu.VMEM((c, c), BF16)],
        compiler_params=_cparams(("arbitrary",), V7X_VMEM_LIMIT),
        name="s5_post",
    )(y, z, s5_d_l, w_glu)


def _merge_kernel(a_ref, b_ref, c_ref, s0_ref, s1_ref, s2_ref, wa_ref, wb_ref, wc_ref, o_ref,
                  wa_bf, wb_bf, wc_bf):
    @pl.when(pl.program_id(1) == 0)
    def _():
        wa_bf[...] = wa_ref[...].astype(BF16)
        wb_bf[...] = wb_ref[...].astype(BF16)
        wc_bf[...] = wc_ref[...].astype(BF16)

    m = _sigmoid(s0_ref[...].astype(F32)) * _dot(a_ref[...], wa_bf[...])
    m = m + _sigmoid(s1_ref[...].astype(F32)) * _dot(b_ref[...], wb_bf[...])
    m = m + _sigmoid(s2_ref[...].astype(F32)) * _dot(c_ref[...], wc_bf[...])
    o_ref[...] = m.astype(o_ref.dtype)


def merge_branches(act_a, act_b, act_c, z, conv_proj, ret_proj, s5_proj, layer, *, s_col0, tm, tn):
    m = act_a.shape[0]
    d = conv_proj.shape[2]
    ka, kb, kc = act_a.shape[1], act_b.shape[1], act_c.shape[1]
    nblk = d // tn

    def aspec(k):
        return pl.BlockSpec((tm, k), lambda j, i: (i, 0))

    def sspec(br):
        return pl.BlockSpec((tm, tn), lambda j, i: (i, s_col0 // tn + br * nblk + j))

    def wspec(k):
        return pl.BlockSpec((None, k, tn), lambda j, i: (layer, 0, j))

    return pl.pallas_call(
        _merge_kernel,
        grid=(nblk, m // tm),
        in_specs=[aspec(ka), aspec(kb), aspec(kc), sspec(0), sspec(1), sspec(2),
                  wspec(ka), wspec(kb), wspec(kc)],
        out_specs=pl.BlockSpec((tm, tn), lambda j, i: (i, j)),
        out_shape=jax.ShapeDtypeStruct((m, d), BF16),
        scratch_shapes=[pltpu.VMEM((ka, tn), BF16), pltpu.VMEM((kb, tn), BF16), pltpu.VMEM((kc, tn), BF16)],
        compiler_params=_cparams(("arbitrary", "arbitrary"), V7X_VMEM_LIMIT),
        name="merge_branches",
    )(act_a, act_b, act_c, z, z, z, conv_proj, ret_proj, s5_proj)


def _ffn_kernel(x_ref, wg_ref, wu_ref, wd_ref, o_ref):
    @pl.when(pl.program_id(1) == 0)
    def _():
        o_ref[...] = jnp.zeros(o_ref.shape, F32)

    x = x_ref[...]
    hg = _dot(x, wg_ref[...].astype(BF16))
    hu = _dot(x, wu_ref[...].astype(BF16))
    o_ref[...] += _dot((_silu(hg) * hu).astype(BF16), wd_ref[...].astype(BF16))


def ffn_blocks(x, wg, wu, wd, layer, ff_block, *, tm):
    m, d = x.shape
    ff = wg.shape[2]
    up = pl.BlockSpec((None, d, ff_block), lambda i, e: (layer, 0, e))
    return pl.pallas_call(
        _ffn_kernel,
        grid=(m // tm, ff // ff_block),
        in_specs=[pl.BlockSpec((tm, d), lambda i, e: (i, 0)), up, up,
                  pl.BlockSpec((None, ff_block, d), lambda i, e: (layer, e, 0))],
        out_specs=pl.BlockSpec((tm, d), lambda i, e: (i, 0)),
        out_shape=jax.ShapeDtypeStruct((m, d), F32),
        compiler_params=_cparams(("arbitrary", "arbitrary"), V7X_VMEM_LIMIT),
        name="ffn_blocks",
    )(x, wg, wu, wd)


def _moe_positions_kernel(eidx_ref, off_ref, pos_ref, run_ref):
    @pl.when(pl.program_id(0) == 0)
    def _():
        run_ref[...] = jnp.zeros(run_ref.shape, F32)

    n_e = off_ref.shape[0]
    top_k, tm = eidx_ref.shape
    expert = lax.broadcasted_iota(jnp.int32, (n_e, tm), 0)
    eidx = eidx_ref[...]
    member = jnp.zeros((n_e, tm), F32)
    for k in range(top_k):
        member = member + jnp.where(eidx[k:k + 1, :] == expert, 1.0, 0.0)
    r = lax.broadcasted_iota(jnp.int32, (tm, tm), 0)
    c = lax.broadcasted_iota(jnp.int32, (tm, tm), 1)
    upper = jnp.where(r <= c, 1.0, 0.0).astype(BF16)
    incl = _dot(member.astype(BF16), upper)
    row_of = off_ref[...] + run_ref[...] + incl - member
    rows = [jnp.sum(jnp.where(eidx[k:k + 1, :] == expert, row_of, 0.0), axis=0, keepdims=True)
            for k in range(top_k)]
    pos_ref[...] = jnp.concatenate(rows, axis=0).astype(jnp.int32)
    run_ref[...] = run_ref[...] + jnp.sum(member, axis=1, keepdims=True)


def moe_positions(eidx, offsets):
    top_k, ntok = eidx.shape
    n_e = offsets.shape[0]
    blk = pl.BlockSpec((top_k, ROW_TILE), lambda i: (0, i))
    return pl.pallas_call(
        _moe_positions_kernel,
        grid=(ntok // ROW_TILE,),
        in_specs=[blk, pl.BlockSpec((n_e, 1), lambda i: (0, 0))],
        out_specs=blk,
        out_shape=jax.ShapeDtypeStruct((top_k, ntok), jnp.int32),
        scratch_shapes=[pltpu.VMEM((n_e, 1), F32)],
        compiler_params=_cparams(("arbitrary",), V7X_VMEM_LIMIT),
        name="moe_positions",
    )(eidx, offsets)


def _token_copy(src_hbm, src_tok, buf, slot, dst_tok, sem, n_data_rows):
    src0 = pl.multiple_of(src_tok * ROW_PITCH, 8)
    dst0 = pl.multiple_of(dst_tok * ROW_PITCH, 8)
    return pltpu.make_async_copy(src_hbm.at[pl.ds(src0, n_data_rows), :],
                                 buf.at[slot, pl.ds(dst0, n_data_rows), :], sem.at[slot])


def _gather_rows_start(idx_ref, src_hbm, buf, slot, sem, n_tok, n_data_rows):
    def body(r, carry):
        _token_copy(src_hbm, idx_ref[0, r], buf, slot, r, sem, n_data_rows).start()
        return carry

    lax.fori_loop(0, n_tok, body, 0)


def _gather_rows_wait(src_hbm, buf, slot, sem, n_tok, n_data_rows):
    def body(r, carry):
        _token_copy(src_hbm, 0, buf, slot, r, sem, n_data_rows).wait()
        return carry

    lax.fori_loop(0, n_tok, body, 0)


def _moe_ffn_kernel(te_ref, nv_ref, idx_ref, idx_next_ref, h_hbm, wg_ref, wu_ref, wd_ref, o_ref,
                    xbuf, sem, wg_bf, wu_bf, wd_bf):
    j = pl.program_id(0)
    n_valid = nv_ref[0]
    slot = j % 2
    tm = xbuf.shape[1] // ROW_PITCH
    k = wg_ref.shape[0] // LANES

    @pl.when(j == 0)
    def _():
        _gather_rows_start(idx_ref, h_hbm, xbuf, 0, sem, tm, k)

    @pl.when(j + 1 < n_valid)
    def _():
        _gather_rows_start(idx_next_ref, h_hbm, xbuf, 1 - slot, sem, tm, k)

    @pl.when(jnp.logical_or(j == 0, te_ref[j] != te_ref[jnp.maximum(j - 1, 0)]))
    def _():
        wg_bf[...] = wg_ref[...].astype(BF16)
        wu_bf[...] = wu_ref[...].astype(BF16)
        wd_bf[...] = wd_ref[...].astype(BF16)

    @pl.when(j < n_valid)
    def _():
        _gather_rows_wait(h_hbm, xbuf, slot, sem, tm, k)
        x = _load_pitched(xbuf.at[slot], tm, k).astype(BF16)
        hg = _dot(x, wg_bf[...])
        hu = _dot(x, wu_bf[...])
        _store_pitched(o_ref, _dot((_silu(hg) * hu).astype(BF16), wd_bf[...]))

    @pl.when(j >= n_valid)
    def _():
        o_ref[...] = jnp.zeros(o_ref.shape, F32)


def moe_ffn_sorted(h, src_tok, tile_expert, n_valid, wg, wu, wd, layer, *, tm):
    n_tiles = src_tok.shape[0]
    d, ff = wg.shape[2], wg.shape[3]
    smem_rows = functools.partial(pl.BlockSpec, (None, 1, tm), memory_space=pltpu.SMEM)
    grid_spec = pltpu.PrefetchScalarGridSpec(
        num_scalar_prefetch=2,
        grid=(n_tiles,),
        in_specs=[
            smem_rows(lambda j, te, nv: (j, 0, 0)),
            smem_rows(lambda j, te, nv: (jnp.minimum(j + 1, n_tiles - 1), 0, 0)),
            pl.BlockSpec(memory_space=pl.ANY),
            pl.BlockSpec((None, None, d, ff), lambda j, te, nv: (layer, te[j], 0, 0)),
            pl.BlockSpec((None, None, d, ff), lambda j, te, nv: (layer, te[j], 0, 0)),
            pl.BlockSpec((None, None, ff, d), lambda j, te, nv: (layer, te[j], 0, 0)),
        ],
        out_specs=pl.BlockSpec((tm * ROW_PITCH, LANES), lambda j, te, nv: (j, 0)),
        scratch_shapes=[pltpu.VMEM((2, tm * ROW_PITCH, LANES), F32), pltpu.SemaphoreType.DMA((2,)),
                        pltpu.VMEM((d, ff), BF16), pltpu.VMEM((d, ff), BF16), pltpu.VMEM((ff, d), BF16)],
    )
    return pl.pallas_call(
        _moe_ffn_kernel,
        grid_spec=grid_spec,
        out_shape=jax.ShapeDtypeStruct((n_tiles * tm * ROW_PITCH, LANES), F32),
        compiler_params=_cparams(("arbitrary",), V7X_VMEM_LIMIT),
        name="moe_ffn_sorted",
    )(tile_expert, n_valid, src_tok, src_tok, h, wg, wu, wd)


def _moe_combine_kernel(pos_ref, pos_next_ref, w_ref, ys_hbm, o_ref, gbuf, sem):
    i = pl.program_id(0)
    n = pl.num_programs(0)
    slot = i % 2
    top_k = gbuf.shape[0] // 2
    tc = gbuf.shape[1] // ROW_PITCH
    n_data_rows = o_ref.shape[1] // LANES

    def copy(p_ref, s, k, r):
        src0 = pl.multiple_of(p_ref[k, r] * ROW_PITCH, 8)
        dst0 = pl.multiple_of(r * ROW_PITCH, 8)
        return pltpu.make_async_copy(ys_hbm.at[pl.ds(src0, n_data_rows), :],
                                     gbuf.at[s * top_k + k, pl.ds(dst0, n_data_rows), :], sem.at[s])

    def start(p_ref, s):
        for k in range(top_k):
            def body(r, carry, k=k):
                copy(p_ref, s, k, r).start()
                return carry

            lax.fori_loop(0, tc, body, 0)

    @pl.when(i == 0)
    def _():
        start(pos_ref, 0)

    @pl.when(i + 1 < n)
    def _():
        start(pos_next_ref, 1 - slot)

    def wait_body(r, carry):
        copy(pos_ref, slot, 0, 0).wait()
        return carry

    lax.fori_loop(0, top_k * tc, wait_body, 0)
    w = w_ref[...]
    cols = []
    for c in range(n_data_rows):
        acc = None
        for k in range(top_k):
            term = w[:, k:k + 1] * gbuf[slot * top_k + k, pl.ds(c, tc, stride=ROW_PITCH), :]
            acc = term if acc is None else acc + term
        cols.append(acc)
    o_ref[...] = jnp.concatenate(cols, axis=1)


def moe_combine(ys, pos_tiles, w_tok, d, *, tc):
    n_tiles, top_k, _ = pos_tiles.shape
    smem_pos = functools.partial(pl.BlockSpec, (None, top_k, tc), memory_space=pltpu.SMEM)
    return pl.pallas_call(
        _moe_combine_kernel,
        grid=(n_tiles,),
        in_specs=[smem_pos(lambda i: (i, 0, 0)),
                  smem_pos(lambda i: (jnp.minimum(i + 1, n_tiles - 1), 0, 0)),
                  pl.BlockSpec((tc, top_k), lambda i: (i, 0)),
                  pl.BlockSpec(memory_space=pl.ANY)],
        out_specs=pl.BlockSpec((tc, d), lambda i: (i, 0)),
        out_shape=jax.ShapeDtypeStruct((n_tiles * tc, d), F32),
        scratch_shapes=[pltpu.VMEM((2 * top_k, tc * ROW_PITCH, LANES), F32), pltpu.SemaphoreType.DMA((2,))],
        compiler_params=_cparams(("arbitrary",), V7X_VMEM_LIMIT),
        name="moe_combine",
    )(pos_tiles, pos_tiles, w_tok, ys)


_HI = lax.Precision.HIGHEST


def _cmul(ar, ai, br, bi):
    return ar * br - ai * bi, ar * bi + ai * br


def s5_operators(a_re, a_im, log_dt, b_re, b_im, c_re, c_im):
    t = S5_T
    dt = jnp.exp(log_dt)[..., None]
    adt_re, adt_im = a_re * dt, a_im * dt
    tau = jnp.arange(t + 1, dtype=F32)[None, None, :, None]
    mag = jnp.exp(adt_re[:, :, None, :] * tau)
    ang = adt_im[:, :, None, :] * tau
    pw_re, pw_im = mag * jnp.cos(ang), mag * jnp.sin(ang)
    ab_re, ab_im = pw_re[:, :, 1], pw_im[:, :, 1]
    den = a_re * a_re + a_im * a_im
    nr, ni = ab_re - 1.0, ab_im
    f_re = (nr * a_re + ni * a_im) / den
    f_im = (ni * a_re - nr * a_im) / den
    bb_re, bb_im = _cmul(f_re[..., None], f_im[..., None], b_re, b_im)
    m1_re, m1_im = _cmul(pw_re[..., None], pw_im[..., None], bb_re[:, :, None], bb_im[:, :, None])
    kk = (jnp.einsum('dgpn,dgtnq->dgtpq', c_re, m1_re[:, :, :t], precision=_HI)
          - jnp.einsum('dgpn,dgtnq->dgtpq', c_im, m1_im[:, :, :t], precision=_HI))
    ti = jnp.arange(t)
    lag = ti[:, None] - ti[None, :]
    kg = kk[:, :, jnp.clip(lag, 0, t - 1)]
    kg = jnp.where((lag >= 0)[None, None, :, :, None, None], kg, 0.0)
    nd, g = a_re.shape[0], a_re.shape[1]
    p = b_re.shape[-1]
    n = a_re.shape[-1]
    inc = jnp.stack([m1_re[:, :, t - 1 - ti], m1_im[:, :, t - 1 - ti]], axis=3)
    w_re, w_im = _cmul(c_re[:, :, None], c_im[:, :, None],
                       pw_re[:, :, 1:, None, :], pw_im[:, :, 1:, None, :])
    mst = jnp.stack([w_re, -w_im], axis=2)

    def reverse_backward(v, axes):
        return jnp.stack([v[0], jnp.flip(v[1], axes)], axis=0)

    kg = reverse_backward(kg, (1, 2))
    inc = reverse_backward(inc, (1,))
    mst = reverse_backward(mst, (2,))
    gpb = S5_LANES // p
    gb = g // gpb
    eye = jnp.eye(gpb, dtype=BF16)
    fw = t * S5_LANES
    toep = jnp.einsum('dbgtspq,gh->dbsgqthp', kg.astype(BF16).reshape(nd, gb, gpb, t, t, p, p), eye)
    minc = jnp.einsum('dbgsknq,gh->dbsgqkhn', inc.astype(BF16).reshape(nd, gb, gpb, t, 2, n, p), eye)
    mstate = jnp.einsum('dbgktpn,gh->dbkgnthp', mst.astype(BF16).reshape(nd, gb, gpb, 2, t, p, n), eye)
    a_t_re = pw_re[:, :, t].reshape(nd, 1, g * n)
    a_t_im = pw_im[:, :, t].reshape(nd, 1, g * n)
    return (toep.reshape(nd, gb, fw, fw), minc.reshape(nd, gb, fw, 2 * gpb * n),
            mstate.reshape(nd, gb, 2 * gpb * n, fw), a_t_re, a_t_im)


def rotary_tables(n_pos, half):
    freq = ROPE_BASE ** (-jnp.arange(half, dtype=F32) / half)
    ang = jnp.arange(n_pos, dtype=F32)[:, None] * freq[None, :]
    return jnp.cos(ang), jnp.sin(ang)


def kernel(x, c, ctx, c_ctx, emb_ln_g, emb_ln_b, ada_w, ada_b, w_in, conv_w, conv_b, conv_ln_g, conv_ln_b, conv_proj, ret_decay_logit, ret_proj, s5_a_re, s5_a_im, s5_log_dt, s5_b_re, s5_b_im, s5_c_re, s5_c_im, s5_d, s5_w_glu, s5_proj, w_out, ln1_g, ln1_b, ln2_g, ln2_b, router_w, router_bias, exp_w_gate, exp_w_up, exp_w_down, sh_w_gate, sh_w_up, sh_w_down):
    n_batch, seq, d = x.shape
    lc = ctx.shape[1]
    depth = w_in.shape[0]
    conv_c = conv_w.shape[2]
    ret_w = ret_proj.shape[1]
    s5_c = s5_d.shape[1]
    s5_g = s5_c // S5_P
    n_exp, _, exp_ff = exp_w_gate.shape[1:]
    sh_ff = sh_w_gate.shape[2]
    rows = seq // GRID_W
    assert lc == ROW_TILE and seq % ROW_TILE == 0 and n_batch + 1 <= MOD_ROWS
    assert RET_HEADS * RET_DK == ret_w and S5_T * S5_LANES == 2 * (S5_LANES // S5_P) * S5_N
    assert s5_c % S5_LANES == 0 and lc % S5_T == 0 and rows % S5_T == 0
    tiles_per_batch = seq // ROW_TILE
    n_lat_tiles = n_batch * tiles_per_batch
    n_tiles = n_lat_tiles + n_batch
    n_lat = n_batch * seq
    alpha = (2.0 * depth) ** 0.25
    ntok = n_tiles * ROW_TILE
    tm_mm = _largest_row_tile(ntok, 2)
    tm_ffn = _largest_row_tile(ntok, 2) if ntok % 544 else 544
    col_a, col_q = 0, 2 * conv_c
    col_g = col_q + 3 * ret_w
    col_u = col_g + ret_w
    col_s = col_u + s5_c

    group_fn = functools.partial(_group_of_tile, n_latent_tiles=n_lat_tiles,
                                 tiles_per_batch=tiles_per_batch, n_batch=n_batch)

    tokens = jnp.concatenate([x.reshape(n_lat, d), ctx.reshape(n_batch * lc, d)], axis=0)
    cvec = jnp.concatenate([c, c_ctx[None, :], jnp.zeros((MOD_ROWS - n_batch - 1, d), F32)], axis=0)
    mods = [ada_modulation(cvec, ada_w, ada_b[i][None, :], i) for i in range(depth)]
    cos_tab, sin_tab = rotary_tables(lc + seq, RET_DK // 2)
    log_g = jax.nn.log_sigmoid(ret_decay_logit.astype(F32))

    xl, h = resid_ln_mod(tokens, [], None, 0, emb_ln_g[None, :], emb_ln_b[None, :], mods[0], 0, 1,
                         alpha=1.0, group_fn=group_fn, n_tiles=n_tiles)

    for i in range(depth):
        last = i == depth - 1
        mod = mods[i]
        z = matmul_stacked_w(h, w_in, i, BF16, tm_mm, 512)
        act_a = jnp.concatenate([
            conformer_conv_act(z, conv_w[i], conv_b[i][None, :], conv_ln_g[i][None, :], conv_ln_b[i][None, :],
                               row_tile0=0, n_tiles=n_lat_tiles, seg=GRID_W),
            conformer_conv_act(z, conv_w[i], conv_b[i][None, :], conv_ln_g[i][None, :], conv_ln_b[i][None, :],
                               row_tile0=n_lat_tiles, n_tiles=n_batch, seg=lc)], axis=0)
        o_f, o_b = retention_scan(z, log_g[i], cos_tab, sin_tab, n_batch=n_batch,
                                  tiles_per_batch=tiles_per_batch, q_col0=col_q // RET_DK)
        act_b = retention_post(o_f, o_b, z, g_col=col_g // ret_w)
        u = z[:, col_u:col_u + s5_c].astype(BF16)
        gb = s5_c // S5_LANES
        fw = S5_T * S5_LANES
        u_lat = u[:n_lat].reshape(n_batch, rows, GRID_W, gb, S5_LANES).transpose(0, 3, 2, 1, 4)
        u_ctx = u[n_lat:].reshape(n_batch, lc, gb, S5_LANES).transpose(0, 2, 1, 3)
        u_fold = jnp.concatenate([u_ctx.reshape(n_batch, gb, lc // S5_T, fw),
                                  u_lat.reshape(n_batch, gb, seq // S5_T, fw)], axis=2)
        ops = s5_operators(s5_a_re[i], s5_a_im[i], s5_log_dt[i], s5_b_re[i], s5_b_im[i], s5_c_re[i], s5_c_im[i])
        y_fold = s5_chunked(u_fold, *ops, n_ctx_chunk=lc // S5_T)
        y_ctx = y_fold[:, :, :lc // S5_T].reshape(n_batch, gb, lc, S5_LANES).transpose(0, 2, 1, 3)
        y_lat = y_fold[:, :, lc // S5_T:].reshape(n_batch, gb, GRID_W, rows, S5_LANES).transpose(0, 3, 2, 1, 4)
        y_s5 = jnp.concatenate([y_lat.reshape(n_lat, s5_c), y_ctx.reshape(n_batch * lc, s5_c)], axis=0)
        act_c = s5_post(y_s5, z, s5_d[i][None, :], s5_w_glu, i, u_col=col_u // s5_c)
        merged = merge_branches(act_a, act_b, act_c, z, conv_proj, ret_proj, s5_proj, i,
                                s_col0=col_s, tm=tm_mm, tn=512)
        y_mix = matmul_stacked_w(merged, w_out, i, F32, tm_mm, 512)
        xl, h2, h2_f32, eidx, wk = resid_ln_mod(
            xl, [y_mix], mod, 2, ln1_g[i][None, :], ln1_b[i][None, :], mod, 3, 4,
            alpha=alpha, group_fn=group_fn, n_tiles=n_tiles,
            router=(router_w[i].T, router_bias[i][:, None]))
        counts = jnp.sum((eidx[None, :, :] == jnp.arange(n_exp, dtype=jnp.int32)[:, None, None]).astype(jnp.int32),
                         axis=(1, 2))
        padded = ((counts + MOE_ROW_TILE - 1) // MOE_ROW_TILE) * MOE_ROW_TILE
        ends = jnp.cumsum(padded)
        n_sorted_tiles = (ntok * TOP_K) // MOE_ROW_TILE + n_exp
        tile_start = jnp.arange(n_sorted_tiles, dtype=jnp.int32) * MOE_ROW_TILE
        tile_expert = jnp.minimum(jnp.sum((ends[None, :] <= tile_start[:, None]).astype(jnp.int32), axis=1),
                                  n_exp - 1)
        n_valid = (ends[-1:] // MOE_ROW_TILE).astype(jnp.int32)
        pos = moe_positions(eidx, (ends - padded).astype(F32)[:, None])
        src_tok = jnp.zeros((n_sorted_tiles * MOE_ROW_TILE,), jnp.int32).at[pos.reshape(-1)].set(
            jnp.tile(jnp.arange(ntok, dtype=jnp.int32), TOP_K))
        y_sorted = moe_ffn_sorted(h2_f32, src_tok.reshape(n_sorted_tiles, 1, MOE_ROW_TILE), tile_expert, n_valid,
                                  exp_w_gate, exp_w_up, exp_w_down, i, tm=MOE_ROW_TILE)
        pos_tiles = pos.reshape(TOP_K, ntok // MOE_COMBINE_TILE, MOE_COMBINE_TILE).transpose(1, 0, 2)
        y_routed = moe_combine(y_sorted, pos_tiles, wk.T, d, tc=MOE_COMBINE_TILE)
        y_shared = ffn_blocks(h2, sh_w_gate, sh_w_up, sh_w_down, i, exp_ff, tm=tm_ffn)
        if last:
            (xl,) = resid_ln_mod(xl, [y_routed, y_shared], mod, 5, ln2_g[i][None, :], ln2_b[i][None, :],
                                 None, 0, 0, alpha=alpha, group_fn=group_fn, n_tiles=n_lat_tiles)
        else:
            xl, h = resid_ln_mod(xl, [y_routed, y_shared], mod, 5, ln2_g[i][None, :], ln2_b[i][None, :],
                                 mods[i + 1], 0, 1, alpha=alpha, group_fn=group_fn, n_tiles=n_tiles)
    return xl.reshape(n_batch, seq, d)
```

```python
import functools
import math

import jax
import jax.numpy as jnp
from jax import lax
from jax.experimental import pallas as pl
from jax.experimental.pallas import tpu as pltpu

F32 = jnp.float32
BF16 = jnp.bfloat16

GRID_W = 64
RET_HEADS = 8
RET_DK = 256
S5_P = 16
S5_N = 64
ROPE_BASE = 10000.0
N_GROUPS = 8
TOPK_GROUPS = 4
TOP_K = 8
ROUTED_SCALE = 2.5
LN_EPS = 1e-5
HEAD_NORM_EPS = 1e-5
NEG_BIG = -1e30
N_BRANCH = 3

ROW_TILE = 256
S5_LANES = 128
S5_T = 8
MOD_ROWS = 8
MOE_ROW_TILE = 256
MOE_COMBINE_TILE = 64
LANES = 128
ROW_PITCH = 40
V7X_VMEM_LIMIT = 56 * 1024 * 1024


def _largest_row_tile(ntok, max_tiles):
    n = ntok // ROW_TILE
    k = max(t for t in range(1, max_tiles + 1) if n % t == 0)
    return k * ROW_TILE


def _cparams(sem, vmem=None):
    return pltpu.CompilerParams(dimension_semantics=sem, vmem_limit_bytes=vmem)


def _split_bf16(v):
    hi = v.astype(BF16)
    lo = (v - hi.astype(F32)).astype(BF16)
    return hi, lo


def _dot(a, b):
    return jnp.dot(a, b, preferred_element_type=F32)


def _dot_nt(a, b):
    return lax.dot_general(a, b, (((1,), (1,)), ((), ())), preferred_element_type=F32)


def _dot_tn(a, b):
    return lax.dot_general(a, b, (((0,), (0,)), ((), ())), preferred_element_type=F32)


def _sigmoid(v):
    return 1.0 / (1.0 + jnp.exp(-v))


def _silu(v):
    return v * _sigmoid(v)


def _store_pitched(ref, v):
    n, width = v.shape
    k = width // LANES
    for c in range(k):
        ref[pl.ds(c, n, stride=ROW_PITCH), :] = v[:, c * LANES:(c + 1) * LANES]
    for c in range(k, ROW_PITCH):
        ref[pl.ds(c, n, stride=ROW_PITCH), :] = jnp.zeros((n, LANES), v.dtype)


def _load_pitched(ref, n, k):
    return jnp.concatenate([ref[pl.ds(c, n, stride=ROW_PITCH), :] for c in range(k)], axis=1)


def _layer_norm_rows(v, g, b, eps):
    mu = jnp.mean(v, axis=-1, keepdims=True)
    vc = v - mu
    var = jnp.mean(vc * vc, axis=-1, keepdims=True)
    return vc * lax.rsqrt(var + eps) * g + b


def _ada_kernel(c_ref, w_ref, b_ref, o_ref):
    c = _silu(c_ref[...])
    ch, cl = _split_bf16(c)
    wh, wl = _split_bf16(w_ref[...])
    o_ref[...] = _dot(ch, wh) + _dot(ch, wl) + _dot(cl, wh) + b_ref[...]


def ada_modulation(cvec, ada_w, ada_b_l, layer):
    _, d, n = ada_w.shape
    tn = 512
    return pl.pallas_call(
        _ada_kernel,
        grid=(n // tn,),
        in_specs=[
            pl.BlockSpec((MOD_ROWS, d), lambda j: (0, 0)),
            pl.BlockSpec((None, d, tn), lambda j: (layer, 0, j)),
            pl.BlockSpec((1, tn), lambda j: (0, j)),
        ],
        out_specs=pl.BlockSpec((MOD_ROWS, tn), lambda j: (0, j)),
        out_shape=jax.ShapeDtypeStruct((MOD_ROWS, n), F32),
        compiler_params=_cparams(("arbitrary",), V7X_VMEM_LIMIT),
        name="ada_modulation",
    )(cvec, ada_w, ada_b_l)


def _group_of_tile(i, n_latent_tiles, tiles_per_batch, n_batch):
    return jnp.where(i < n_latent_tiles, i // tiles_per_batch, n_batch)


def _route(h, rwt_ref, rb_ref):
    hh, hl = _split_bf16(h)
    wh, wl = _split_bf16(rwt_ref[...])
    logits = _dot_nt(wh, hh) + _dot_nt(wh, hl) + _dot_nt(wl, hh)
    scores = _sigmoid(logits)
    sel = scores + rb_ref[...]
    n_e, tm = sel.shape
    per = n_e // N_GROUPS
    shape3 = (N_GROUPS, per, tm)
    sel3 = sel.reshape(shape3)
    io_e = lax.broadcasted_iota(jnp.int32, shape3, 1)
    io_g = lax.broadcasted_iota(jnp.int32, shape3, 0)
    m1 = jnp.max(sel3, axis=1, keepdims=True)
    first = jnp.min(jnp.where(sel3 == m1, io_e, per), axis=1, keepdims=True)
    m2 = jnp.max(jnp.where(io_e == first, -jnp.inf, sel3), axis=1, keepdims=True)
    work = m1 + m2
    iog1 = lax.broadcasted_iota(jnp.int32, work.shape, 0)
    gsel = jnp.zeros(work.shape, F32)
    for _ in range(TOPK_GROUPS):
        m = jnp.max(work, axis=0, keepdims=True)
        fi = jnp.min(jnp.where(work == m, iog1, N_GROUPS), axis=0, keepdims=True)
        hit = iog1 == fi
        gsel = jnp.where(hit, 1.0, gsel)
        work = jnp.where(hit, -jnp.inf, work)
    work = jnp.where(jnp.broadcast_to(gsel, shape3) > 0.0, sel3, NEG_BIG)
    flat = io_g * per + io_e
    scores3 = scores.reshape(shape3)
    picked, picked_score = [], []
    for _ in range(TOP_K):
        m = jnp.max(jnp.max(work, axis=1, keepdims=True), axis=0, keepdims=True)
        cand = jnp.where(work == m, flat, n_e)
        fi = jnp.min(jnp.min(cand, axis=1, keepdims=True), axis=0, keepdims=True)
        hit = flat == fi
        sk = jnp.sum(jnp.sum(jnp.where(hit, scores3, 0.0), axis=1, keepdims=True), axis=0, keepdims=True)
        picked.append(fi.reshape(1, tm))
        picked_score.append(sk.reshape(1, tm))
        work = jnp.where(hit, -jnp.inf, work)
    eidx = jnp.concatenate(picked, axis=0)
    w = jnp.concatenate(picked_score, axis=0)
    return eidx, ROUTED_SCALE * w / jnp.sum(w, axis=0, keepdims=True)


def _resid_ln_mod_kernel(*refs, alpha, n_y, has_mod, has_router, group_fn):
    it = iter(refs)
    x_ref = next(it)
    y_refs = [next(it) for _ in range(n_y)]
    gate_ref = next(it) if n_y else None
    g_ref, b_ref = next(it), next(it)
    shift_ref = scale_ref = rwt_ref = rb_ref = None
    if has_mod:
        shift_ref, scale_ref = next(it), next(it)
    if has_router:
        rwt_ref, rb_ref = next(it), next(it)
    xl_ref = next(it)
    h_ref = next(it) if has_mod else None
    hf_ref, eidx_ref, wk_ref = (next(it), next(it), next(it)) if has_router else (None, None, None)

    grp = group_fn(pl.program_id(0))
    v = x_ref[...]
    if n_y:
        y = y_refs[0][...]
        for r in y_refs[1:]:
            y = y + r[...]
        v = alpha * v + gate_ref[pl.ds(grp, 1), :] * y
    xl = _layer_norm_rows(v, g_ref[...], b_ref[...], LN_EPS)
    xl_ref[...] = xl
    if has_mod:
        h = xl * (1.0 + scale_ref[pl.ds(grp, 1), :]) + shift_ref[pl.ds(grp, 1), :]
        h_ref[...] = h.astype(h_ref.dtype)
        if has_router:
            _store_pitched(hf_ref, h)
            eidx_ref[...], wk_ref[...] = _route(h, rwt_ref, rb_ref)


def resid_ln_mod(x, ys, gate_mod, gate_col, ln_g, ln_b, mod, shift_col, scale_col, *,
                 alpha, group_fn, n_tiles, router=None):
    d = x.shape[1]
    n_y = len(ys)
    has_mod = mod is not None
    has_router = router is not None
    row = pl.BlockSpec((ROW_TILE, d), lambda i: (i, 0))
    vec = pl.BlockSpec((1, d), lambda i: (0, 0))
    args, specs = [x], [row]
    for y in ys:
        args.append(y)
        specs.append(row)
    if n_y:
        args.append(gate_mod)
        specs.append(pl.BlockSpec((MOD_ROWS, d), lambda i: (0, gate_col)))
    args += [ln_g, ln_b]
    specs += [vec, vec]
    if has_mod:
        args += [mod, mod]
        specs += [pl.BlockSpec((MOD_ROWS, d), lambda i: (0, shift_col)),
                  pl.BlockSpec((MOD_ROWS, d), lambda i: (0, scale_col))]
    out_shapes = [jax.ShapeDtypeStruct((n_tiles * ROW_TILE, d), F32)]
    out_specs = [row]
    if has_mod:
        out_shapes.append(jax.ShapeDtypeStruct((n_tiles * ROW_TILE, d), BF16))
        out_specs.append(row)
    if has_router:
        rwt, rb = router
        n_e = rwt.shape[0]
        args += [rwt, rb]
        specs += [pl.BlockSpec((n_e, d), lambda i: (0, 0)), pl.BlockSpec((n_e, 1), lambda i: (0, 0))]
        assert d <= LANES * ROW_PITCH
        out_shapes += [jax.ShapeDtypeStruct((n_tiles * ROW_TILE * ROW_PITCH, LANES), F32),
                       jax.ShapeDtypeStruct((TOP_K, n_tiles * ROW_TILE), jnp.int32),
                       jax.ShapeDtypeStruct((TOP_K, n_tiles * ROW_TILE), F32)]
        out_specs += [pl.BlockSpec((ROW_TILE * ROW_PITCH, LANES), lambda i: (i, 0)),
                      pl.BlockSpec((TOP_K, ROW_TILE), lambda i: (0, i)),
                      pl.BlockSpec((TOP_K, ROW_TILE), lambda i: (0, i))]
    kern = functools.partial(_resid_ln_mod_kernel, alpha=alpha, n_y=n_y, has_mod=has_mod,
                             has_router=has_router, group_fn=group_fn)
    return pl.pallas_call(
        kern, grid=(n_tiles,), in_specs=specs, out_specs=out_specs, out_shape=out_shapes,
        compiler_params=_cparams(("arbitrary",), V7X_VMEM_LIMIT),
        name="resid_ln_mod",
    )(*args)


def _mm_kernel(x_ref, w_ref, o_ref, wbf_ref):
    @pl.when(pl.program_id(1) == 0)
    def _():
        wbf_ref[...] = w_ref[...].astype(BF16)

    o_ref[...] = _dot(x_ref[...], wbf_ref[...]).astype(o_ref.dtype)


def matmul_stacked_w(x, w, layer, out_dtype, tm, tn):
    m, k = x.shape
    n = w.shape[2]
    return pl.pallas_call(
        _mm_kernel,
        grid=(n // tn, m // tm),
        in_specs=[
            pl.BlockSpec((tm, k), lambda j, i: (i, 0)),
            pl.BlockSpec((None, k, tn), lambda j, i: (layer, 0, j)),
        ],
        out_specs=pl.BlockSpec((tm, tn), lambda j, i: (i, j)),
        out_shape=jax.ShapeDtypeStruct((m, n), out_dtype),
        scratch_shapes=[pltpu.VMEM((k, tn), BF16)],
        compiler_params=_cparams(("arbitrary", "arbitrary"), V7X_VMEM_LIMIT),
        name="matmul",
    )(x, w)


def _conv_kernel(a1_ref, a2_ref, w_ref, cb_ref, g_ref, b_ref, o_ref, pad_ref, y_ref, *, seg, n_tap):
    half = n_tap // 2
    front = ((half + 7) // 8) * 8
    nseg = ROW_TILE // seg
    c = a1_ref.shape[1]
    u = a1_ref[...].astype(F32) * _sigmoid(a2_ref[...].astype(F32))
    pad_ref[...] = jnp.zeros(pad_ref.shape, F32)
    for s in range(nseg):
        pad_ref[s, front:front + seg, :] = u[s * seg:(s + 1) * seg, :]
    lanes = 128

    def chunk(ci, carry):
        c0 = pl.multiple_of(ci * lanes, lanes)
        acc = jnp.zeros((nseg, seg, lanes), F32)
        for k in range(n_tap):
            off = front - half + k
            acc = acc + w_ref[k:k + 1, pl.ds(c0, lanes)] * pad_ref[:, off:off + seg, pl.ds(c0, lanes)]
        y_ref[:, pl.ds(c0, lanes)] = acc.reshape(ROW_TILE, lanes)
        return carry

    lax.fori_loop(0, c // lanes, chunk, 0)
    y = y_ref[...] + cb_ref[...]
    o_ref[...] = _silu(_layer_norm_rows(y, g_ref[...], b_ref[...], LN_EPS)).astype(o_ref.dtype)


def conformer_conv_act(z, conv_w_l, conv_b_l, ln_g_l, ln_b_l, *, row_tile0, n_tiles, seg):
    n_tap, c = conv_w_l.shape
    half = n_tap // 2
    front = ((half + 7) // 8) * 8
    nseg = ROW_TILE // seg
    vec = pl.BlockSpec((1, c), lambda i: (0, 0))
    kern = functools.partial(_conv_kernel, seg=seg, n_tap=n_tap)
    return pl.pallas_call(
        kern,
        grid=(n_tiles,),
        in_specs=[
            pl.BlockSpec((ROW_TILE, c), lambda i: (row_tile0 + i, 0)),
            pl.BlockSpec((ROW_TILE, c), lambda i: (row_tile0 + i, 1)),
            pl.BlockSpec((n_tap, c), lambda i: (0, 0)),
            vec, vec, vec,
        ],
        out_specs=pl.BlockSpec((ROW_TILE, c), lambda i: (i, 0)),
        out_shape=jax.ShapeDtypeStruct((n_tiles * ROW_TILE, c), BF16),
        scratch_shapes=[pltpu.VMEM((nseg, seg + 2 * front, c), F32), pltpu.VMEM((ROW_TILE, c), F32)],
        compiler_params=_cparams(("arbitrary",), V7X_VMEM_LIMIT),
        name="conformer_conv",
    )(z, z, conv_w_l, conv_b_l, ln_g_l, ln_b_l)


def _rotary(t, cos, sin):
    half = t.shape[1] // 2
    t1, t2 = t[:, :half], t[:, half:]
    return jnp.concatenate([t1 * cos - t2 * sin, t1 * sin + t2 * cos], axis=1)


def _retention_kernel(logg_ref, qf_ref, kf_ref, vf_ref, cf_ref, sf_ref,
                      qb_ref, kb_ref, vb_ref, cb_ref, sb_ref,
                      of_ref, ob_ref, state_ref):
    h = pl.program_id(1)
    n = pl.program_id(2)
    c = ROW_TILE
    k_scale = RET_DK ** -0.5

    @pl.when(n == 0)
    def _():
        state_ref[...] = jnp.zeros(state_ref.shape, F32)

    row = lax.broadcasted_iota(jnp.int32, (c, c), 0)
    col = lax.broadcasted_iota(jnp.int32, (c, c), 1)
    ridx = lax.broadcasted_iota(jnp.int32, (c, 1), 0).astype(F32)

    def one_direction(d, q_ref, k_ref, v_ref, cos_ref, sin_ref, o_ref):
        lg = logg_ref[d, h]
        cos, sin = cos_ref[...], sin_ref[...]
        q = _rotary(q_ref[...].astype(F32), cos, sin)
        k = _rotary(k_ref[...].astype(F32), cos, sin) * k_scale
        v = v_ref[...].astype(BF16)
        dist = (row - col) if d == 0 else (col - row)
        decay = jnp.where(dist >= 0, jnp.exp(lg * jnp.maximum(dist, 0).astype(F32)), 0.0)
        qb = q.astype(BF16)
        scores = _dot_nt(qb, k.astype(BF16)) * decay
        inner = _dot(scores.astype(BF16), v)
        to_prev = (ridx + 1.0) if d == 0 else (c - ridx)
        to_end = (c - 1.0 - ridx) if d == 0 else ridx
        s_prev = state_ref[d]
        cross = _dot(qb, s_prev.astype(BF16)) * jnp.exp(lg * to_prev)
        o_ref[...] = inner + cross
        kw = (k * jnp.exp(lg * to_end)).astype(BF16)
        state_ref[d] = jnp.exp(lg * jnp.full((1, 1), float(c), F32)) * s_prev + _dot_tn(kw, v)

    one_direction(0, qf_ref, kf_ref, vf_ref, cf_ref, sf_ref, of_ref)
    one_direction(1, qb_ref, kb_ref, vb_ref, cb_ref, sb_ref, ob_ref)


def retention_scan(z, log_g, cos_tab, sin_tab, *, n_batch, tiles_per_batch, q_col0):
    ntok = z.shape[0]
    n_lat = n_batch * tiles_per_batch
    n_steps = tiles_per_batch + 1
    dk = RET_DK
    hh = RET_HEADS

    def row_f(b, n):
        return jnp.where(n == 0, n_lat + b, b * tiles_per_batch + n - 1)

    def row_b(b, n):
        return jnp.where(n == 0, n_lat + b, b * tiles_per_batch + tiles_per_batch - n)

    def pos_f(n):
        return n

    def pos_b(n):
        return jnp.where(n == 0, 0, tiles_per_batch + 1 - n)

    def zspec(rowfn, sec):
        return pl.BlockSpec((ROW_TILE, dk), lambda b, h, n: (rowfn(b, n), q_col0 + sec * hh + h))

    def tspec(posfn):
        return pl.BlockSpec((ROW_TILE, dk // 2), lambda b, h, n: (posfn(n), 0))

    def ospec(rowfn):
        return pl.BlockSpec((ROW_TILE, dk), lambda b, h, n: (rowfn(b, n), h))

    smem = pl.BlockSpec(memory_space=pltpu.SMEM)
    return pl.pallas_call(
        _retention_kernel,
        grid=(n_batch, hh, n_steps),
        in_specs=[smem,
                  zspec(row_f, 0), zspec(row_f, 1), zspec(row_f, 2), tspec(pos_f), tspec(pos_f),
                  zspec(row_b, 0), zspec(row_b, 1), zspec(row_b, 2), tspec(pos_b), tspec(pos_b)],
        out_specs=[ospec(row_f), ospec(row_b)],
        out_shape=[jax.ShapeDtypeStruct((ntok, hh * dk), F32)] * 2,
        scratch_shapes=[pltpu.VMEM((2, dk, dk), F32)],
        compiler_params=_cparams(("arbitrary", "arbitrary", "arbitrary"), V7X_VMEM_LIMIT),
        name="retention_scan",
    )(log_g, z, z, z, cos_tab, sin_tab, z, z, z, cos_tab, sin_tab)


def _ret_post_kernel(of_ref, ob_ref, g_ref, o_ref):
    dk = RET_DK
    for h in range(RET_HEADS):
        sl = slice(h * dk, (h + 1) * dk)
        o = of_ref[:, sl] + ob_ref[:, sl]
        mu = jnp.mean(o, axis=-1, keepdims=True)
        oc = o - mu
        var = jnp.mean(oc * oc, axis=-1, keepdims=True)
        on = oc * lax.rsqrt(var + HEAD_NORM_EPS)
        o_ref[:, sl] = (on * _silu(g_ref[:, sl].astype(F32))).astype(o_ref.dtype)


def retention_post(o_f, o_b, z, *, g_col):
    ntok, w = o_f.shape
    row = pl.BlockSpec((ROW_TILE, w), lambda i: (i, 0))
    return pl.pallas_call(
        _ret_post_kernel,
        grid=(ntok // ROW_TILE,),
        in_specs=[row, row, pl.BlockSpec((ROW_TILE, w), lambda i: (i, g_col))],
        out_specs=row,
        out_shape=jax.ShapeDtypeStruct((ntok, w), BF16),
        compiler_params=_cparams(("arbitrary",), V7X_VMEM_LIMIT),
        name="retention_post",
    )(o_f, o_b, z)


def _expand_block_diag(compact, w, rows_per_group):
    gpb = S5_LANES // S5_P
    n_rows, k = compact.shape
    n_cols = k * gpb
    lw, lg, lr = w.bit_length() - 1, gpb.bit_length() - 1, rows_per_group.bit_length() - 1
    i = lax.broadcasted_iota(jnp.int32, (k, n_cols), 0)
    c = lax.broadcasted_iota(jnp.int32, (k, n_cols), 1)
    src = ((c >> (lw + lg)) << lw) + (c & (w - 1))
    rep = jnp.where(i == src, 1.0, 0.0).astype(BF16)
    full = _dot(compact, rep)
    r = lax.broadcasted_iota(jnp.int32, (n_rows, n_cols), 0)
    c2 = lax.broadcasted_iota(jnp.int32, (n_rows, n_cols), 1)
    keep = ((r >> lr) & (gpb - 1)) == ((c2 >> lw) & (gpb - 1))
    return jnp.where(keep, full, 0.0).astype(BF16)


def _s5_local_kernel(u_ref, kc_ref, ic_ref, yl_ref, xre_ref, xim_ref, toep_ref, minc_ref):
    @pl.when(pl.program_id(2) == 0)
    def _():
        toep_ref[...] = _expand_block_diag(kc_ref[...], S5_P, S5_P)
        minc_ref[...] = _expand_block_diag(ic_ref[...], S5_N, S5_P)

    u = u_ref[...]
    yl_ref[...] = _dot(u, toep_ref[...])
    xi = _dot(u, minc_ref[...])
    half = xi.shape[1] // 2
    xre_ref[...] = xi[:, :half]
    xim_ref[...] = xi[:, half:]


def _s5_scan_kernel(xre_ref, xim_ref, are_ref, aim_ref, ore_ref, oim_ref, *, n_ctx_chunk):
    n_chunk = xre_ref.shape[0]
    backward = pl.program_id(1) == 1
    ar, ai = are_ref[...], aim_ref[...]

    def step(i, carry):
        sr, si = carry
        rev = jnp.where(i < n_ctx_chunk, n_ctx_chunk - 1 - i, n_chunk + n_ctx_chunk - 1 - i)
        c = jnp.where(backward, rev, i)
        ore_ref[pl.ds(c, 1), :] = sr
        oim_ref[pl.ds(c, 1), :] = si
        nr = ar * sr - ai * si + xre_ref[pl.ds(c, 1), :]
        ni = ar * si + ai * sr + xim_ref[pl.ds(c, 1), :]
        return nr, ni

    zero = jnp.zeros(ar.shape, F32)
    lax.fori_loop(0, n_chunk, step, (zero, zero))


def _s5_state_kernel(xre_ref, xim_ref, mc_ref, yl_ref, y_ref, mst_ref):
    @pl.when(pl.program_id(1) == 0)
    def _():
        for d in range(2):
            mst_ref[d] = _expand_block_diag(mc_ref[d], S5_P, S5_N)

    y = yl_ref[0] + yl_ref[1]
    for d in range(2):
        x0 = jnp.concatenate([xre_ref[d], xim_ref[d]], axis=1).astype(BF16)
        y = y + _dot(x0, mst_ref[d])
    y_ref[...] = y


def s5_chunked(u_fold, toep, minc, mstate, a_re, a_im, *, n_ctx_chunk):
    nb, gb, nch, fw = u_fold.shape
    sw = fw // 2
    op = pl.BlockSpec((None, None, fw, S5_LANES), lambda g, d, b: (d, g, 0, 0))
    yl, xre, xim = pl.pallas_call(
        _s5_local_kernel,
        grid=(gb, 2, nb),
        in_specs=[pl.BlockSpec((None, None, nch, fw), lambda g, d, b: (b, g, 0, 0)), op, op],
        out_specs=[pl.BlockSpec((None, None, None, nch, fw), lambda g, d, b: (b, d, g, 0, 0)),
                   pl.BlockSpec((None, None, nch, sw), lambda g, d, b: (b, d, 0, g)),
                   pl.BlockSpec((None, None, nch, sw), lambda g, d, b: (b, d, 0, g))],
        out_shape=[jax.ShapeDtypeStruct((nb, 2, gb, nch, fw), F32),
                   jax.ShapeDtypeStruct((nb, 2, nch, gb * sw), F32),
                   jax.ShapeDtypeStruct((nb, 2, nch, gb * sw), F32)],
        scratch_shapes=[pltpu.VMEM((fw, fw), BF16), pltpu.VMEM((fw, fw), BF16)],
        compiler_params=_cparams(("arbitrary", "arbitrary", "arbitrary"), V7X_VMEM_LIMIT),
        name="s5_local",
    )(u_fold, toep, minc)
    scan_w = 2 * sw
    full = pl.BlockSpec((None, None, nch, scan_w), lambda b, d, j: (b, d, 0, j))
    avec = pl.BlockSpec((None, 1, scan_w), lambda b, d, j: (d, 0, j))
    x0re, x0im = pl.pallas_call(
        functools.partial(_s5_scan_kernel, n_ctx_chunk=n_ctx_chunk),
        grid=(nb, 2, gb * sw // scan_w),
        in_specs=[full, full, avec, avec],
        out_specs=[full, full],
        out_shape=[jax.ShapeDtypeStruct((nb, 2, nch, gb * sw), F32)] * 2,
        compiler_params=_cparams(("arbitrary", "arbitrary", "arbitrary"), V7X_VMEM_LIMIT),
        name="s5_scan",
    )(xre, xim, a_re, a_im)
    xcol = pl.BlockSpec((None, 2, nch, sw), lambda g, b: (b, 0, 0, g))
    return pl.pallas_call(
        _s5_state_kernel,
        grid=(gb, nb),
        in_specs=[xcol, xcol,
                  pl.BlockSpec((2, None, fw, S5_LANES), lambda g, b: (0, g, 0, 0)),
                  pl.BlockSpec((None, 2, None, nch, fw), lambda g, b: (b, 0, g, 0, 0))],
        out_specs=pl.BlockSpec((None, None, nch, fw), lambda g, b: (b, g, 0, 0)),
        out_shape=jax.ShapeDtypeStruct((nb, gb, nch, fw), F32),
        scratch_shapes=[pltpu.VMEM((2, fw, fw), BF16)],
        compiler_params=_cparams(("arbitrary", "arbitrary"), V7X_VMEM_LIMIT),
        name="s5_state",
    )(x0re, x0im, mstate, yl)


def _gelu_tanh(v):
    return 0.5 * v * (1.0 + jnp.tanh(math.sqrt(2.0 / math.pi) * (v + 0.044715 * v * v * v)))


def _s5_post_kernel(y_ref, u_ref, d_ref, w_ref, o_ref, wbf_ref):
    @pl.when(pl.program_id(0) == 0)
    def _():
        wbf_ref[...] = w_ref[...].astype(BF16)

    t = _gelu_tanh(y_ref[...] + d_ref[...] * u_ref[...].astype(F32))
    o_ref[...] = (t * _sigmoid(_dot(t.astype(BF16), wbf_ref[...]))).astype(o_ref.dtype)


def s5_post(y, z, s5_d_l, w_glu, layer, *, u_col):
    ntok, c = y.shape
    row = pl.BlockSpec((ROW_TILE, c), lambda i: (i, 0))
    return pl.pallas_call(
        _s5_post_kernel,
        grid=(ntok // ROW_TILE,),
        in_specs=[row, pl.BlockSpec((ROW_TILE, c), lambda i: (i, u_col)),
                  pl.BlockSpec((1, c), lambda i: (0, 0)),
                  pl.BlockSpec((None, c, c), lambda i: (layer, 0, 0))],
        out_specs=row,
        out_shape=jax.ShapeDtypeStruct((ntok, c), BF16),
        scratch_shapes=[pltpu.VMEM((c, c), BF16)],
        compiler_params=_cparams(("arbitrary",), V7X_VMEM_LIMIT),
        name="s5_post",
    )(y, z, s5_d_l, w_glu)


def _merge_kernel(a_ref, b_ref, c_ref, s0_ref, s1_ref, s2_ref, wa_ref, wb_ref, wc_ref, o_ref,
                  wa_bf, wb_bf, wc_bf):
    @pl.when(pl.program_id(1) == 0)
    def _():
        wa_bf[...] = wa_ref[...].astype(BF16)
        wb_bf[...] = wb_ref[...].astype(BF16)
        wc_bf[...] = wc_ref[...].astype(BF16)

    m = _sigmoid(s0_ref[...].astype(F32)) * _dot(a_ref[...], wa_bf[...])
    m = m + _sigmoid(s1_ref[...].astype(F32)) * _dot(b_ref[...], wb_bf[...])
    m = m + _sigmoid(s2_ref[...].astype(F32)) * _dot(c_ref[...], wc_bf[...])
    o_ref[...] = m.astype(o_ref.dtype)


def merge_branches(act_a, act_b, act_c, z, conv_proj, ret_proj, s5_proj, layer, *, s_col0, tm, tn):
    m = act_a.shape[0]
    d = conv_proj.shape[2]
    ka, kb, kc = act_a.shape[1], act_b.shape[1], act_c.shape[1]
    nblk = d // tn

    def aspec(k):
        return pl.BlockSpec((tm, k), lambda j, i: (i, 0))

    def sspec(br):
        return pl.BlockSpec((tm, tn), lambda j, i: (i, s_col0 // tn + br * nblk + j))

    def wspec(k):
        return pl.BlockSpec((None, k, tn), lambda j, i: (layer, 0, j))

    return pl.pallas_call(
        _merge_kernel,
        grid=(nblk, m // tm),
        in_specs=[aspec(ka), aspec(kb), aspec(kc), sspec(0), sspec(1), sspec(2),
                  wspec(ka), wspec(kb), wspec(kc)],
        out_specs=pl.BlockSpec((tm, tn), lambda j, i: (i, j)),
        out_shape=jax.ShapeDtypeStruct((m, d), BF16),
        scratch_shapes=[pltpu.VMEM((ka, tn), BF16), pltpu.VMEM((kb, tn), BF16), pltpu.VMEM((kc, tn), BF16)],
        compiler_params=_cparams(("arbitrary", "arbitrary"), V7X_VMEM_LIMIT),
        name="merge_branches",
    )(act_a, act_b, act_c, z, z, z, conv_proj, ret_proj, s5_proj)


def _ffn_kernel(x_ref, wg_ref, wu_ref, wd_ref, o_ref):
    @pl.when(pl.program_id(1) == 0)
    def _():
        o_ref[...] = jnp.zeros(o_ref.shape, F32)

    x = x_ref[...]
    hg = _dot(x, wg_ref[...].astype(BF16))
    hu = _dot(x, wu_ref[...].astype(BF16))
    o_ref[...] += _dot((_silu(hg) * hu).astype(BF16), wd_ref[...].astype(BF16))


def ffn_blocks(x, wg, wu, wd, layer, ff_block, *, tm):
    m, d = x.shape
    ff = wg.shape[2]
    up = pl.BlockSpec((None, d, ff_block), lambda i, e: (layer, 0, e))
    return pl.pallas_call(
        _ffn_kernel,
        grid=(m // tm, ff // ff_block),
        in_specs=[pl.BlockSpec((tm, d), lambda i, e: (i, 0)), up, up,
                  pl.BlockSpec((None, ff_block, d), lambda i, e: (layer, e, 0))],
        out_specs=pl.BlockSpec((tm, d), lambda i, e: (i, 0)),
        out_shape=jax.ShapeDtypeStruct((m, d), F32),
        compiler_params=_cparams(("arbitrary", "arbitrary"), V7X_VMEM_LIMIT),
        name="ffn_blocks",
    )(x, wg, wu, wd)


def _moe_positions_kernel(eidx_ref, off_ref, pos_ref, run_ref):
    @pl.when(pl.program_id(0) == 0)
    def _():
        run_ref[...] = jnp.zeros(run_ref.shape, F32)

    n_e = off_ref.shape[0]
    top_k, tm = eidx_ref.shape
    expert = lax.broadcasted_iota(jnp.int32, (n_e, tm), 0)
    eidx = eidx_ref[...]
    member = jnp.zeros((n_e, tm), F32)
    for k in range(top_k):
        member = member + jnp.where(eidx[k:k + 1, :] == expert, 1.0, 0.0)
    r = lax.broadcasted_iota(jnp.int32, (tm, tm), 0)
    c = lax.broadcasted_iota(jnp.int32, (tm, tm), 1)
    upper = jnp.where(r <= c, 1.0, 0.0).astype(BF16)
    incl = _dot(member.astype(BF16), upper)
    row_of = off_ref[...] + run_ref[...] + incl - member
    rows = [jnp.sum(jnp.where(eidx[k:k + 1, :] == expert, row_of, 0.0), axis=0, keepdims=True)
            for k in range(top_k)]
    pos_ref[...] = jnp.concatenate(rows, axis=0).astype(jnp.int32)
    run_ref[...] = run_ref[...] + jnp.sum(member, axis=1, keepdims=True)


def moe_positions(eidx, offsets):
    top_k, ntok = eidx.shape
    n_e = offsets.shape[0]
    blk = pl.BlockSpec((top_k, ROW_TILE), lambda i: (0, i))
    return pl.pallas_call(
        _moe_positions_kernel,
        grid=(ntok // ROW_TILE,),
        in_specs=[blk, pl.BlockSpec((n_e, 1), lambda i: (0, 0))],
        out_specs=blk,
        out_shape=jax.ShapeDtypeStruct((top_k, ntok), jnp.int32),
        scratch_shapes=[pltpu.VMEM((n_e, 1), F32)],
        compiler_params=_cparams(("arbitrary",), V7X_VMEM_LIMIT),
        name="moe_positions",
    )(eidx, offsets)


def _token_copy(src_hbm, src_tok, buf, slot, dst_tok, sem, n_data_rows):
    src0 = pl.multiple_of(src_tok * ROW_PITCH, 8)
    dst0 = pl.multiple_of(dst_tok * ROW_PITCH, 8)
    return pltpu.make_async_copy(src_hbm.at[pl.ds(src0, n_data_rows), :],
                                 buf.at[slot, pl.ds(dst0, n_data_rows), :], sem.at[slot])


def _gather_rows_start(idx_ref, src_hbm, buf, slot, sem, n_tok, n_data_rows):
    def body(r, carry):
        _token_copy(src_hbm, idx_ref[0, r], buf, slot, r, sem, n_data_rows).start()
        return carry

    lax.fori_loop(0, n_tok, body, 0)


def _gather_rows_wait(src_hbm, buf, slot, sem, n_tok, n_data_rows):
    def body(r, carry):
        _token_copy(src_hbm, 0, buf, slot, r, sem, n_data_rows).wait()
        return carry

    lax.fori_loop(0, n_tok, body, 0)


def _moe_ffn_kernel(te_ref, nv_ref, idx_ref, idx_next_ref, h_hbm, wg_ref, wu_ref, wd_ref, o_ref,
                    xbuf, sem, wg_bf, wu_bf, wd_bf):
    j = pl.program_id(0)
    n_valid = nv_ref[0]
    slot = j % 2
    tm = xbuf.shape[1] // ROW_PITCH
    k = wg_ref.shape[0] // LANES

    @pl.when(j == 0)
    def _():
        _gather_rows_start(idx_ref, h_hbm, xbuf, 0, sem, tm, k)

    @pl.when(j + 1 < n_valid)
    def _():
        _gather_rows_start(idx_next_ref, h_hbm, xbuf, 1 - slot, sem, tm, k)

    @pl.when(jnp.logical_or(j == 0, te_ref[j] != te_ref[jnp.maximum(j - 1, 0)]))
    def _():
        wg_bf[...] = wg_ref[...].astype(BF16)
        wu_bf[...] = wu_ref[...].astype(BF16)
        wd_bf[...] = wd_ref[...].astype(BF16)

    @pl.when(j < n_valid)
    def _():
        _gather_rows_wait(h_hbm, xbuf, slot, sem, tm, k)
        x = _load_pitched(xbuf.at[slot], tm, k).astype(BF16)
        hg = _dot(x, wg_bf[...])
        hu = _dot(x, wu_bf[...])
        _store_pitched(o_ref, _dot((_silu(hg) * hu).astype(BF16), wd_bf[...]))

    @pl.when(j >= n_valid)
    def _():
        o_ref[...] = jnp.zeros(o_ref.shape, F32)


def moe_ffn_sorted(h, src_tok, tile_expert, n_valid, wg, wu, wd, layer, *, tm):
    n_tiles = src_tok.shape[0]
    d, ff = wg.shape[2], wg.shape[3]
    smem_rows = functools.partial(pl.BlockSpec, (None, 1, tm), memory_space=pltpu.SMEM)
    grid_spec = pltpu.PrefetchScalarGridSpec(
        num_scalar_prefetch=2,
        grid=(n_tiles,),
        in_specs=[
            smem_rows(lambda j, te, nv: (j, 0, 0)),
            smem_rows(lambda j, te, nv: (jnp.minimum(j + 1, n_tiles - 1), 0, 0)),
            pl.BlockSpec(memory_space=pl.ANY),
            pl.BlockSpec((None, None, d, ff), lambda j, te, nv: (layer, te[j], 0, 0)),
            pl.BlockSpec((None, None, d, ff), lambda j, te, nv: (layer, te[j], 0, 0)),
            pl.BlockSpec((None, None, ff, d), lambda j, te, nv: (layer, te[j], 0, 0)),
        ],
        out_specs=pl.BlockSpec((tm * ROW_PITCH, LANES), lambda j, te, nv: (j, 0)),
        scratch_shapes=[pltpu.VMEM((2, tm * ROW_PITCH, LANES), F32), pltpu.SemaphoreType.DMA((2,)),
                        pltpu.VMEM((d, ff), BF16), pltpu.VMEM((d, ff), BF16), pltpu.VMEM((ff, d), BF16)],
    )
    return pl.pallas_call(
        _moe_ffn_kernel,
        grid_spec=grid_spec,
        out_shape=jax.ShapeDtypeStruct((n_tiles * tm * ROW_PITCH, LANES), F32),
        compiler_params=_cparams(("arbitrary",), V7X_VMEM_LIMIT),
        name="moe_ffn_sorted",
    )(tile_expert, n_valid, src_tok, src_tok, h, wg, wu, wd)


def _moe_combine_kernel(pos_ref, pos_next_ref, w_ref, ys_hbm, o_ref, gbuf, sem):
    i = pl.program_id(0)
    n = pl.num_programs(0)
    slot = i % 2
    top_k = gbuf.shape[0] // 2
    tc = gbuf.shape[1] // ROW_PITCH
    n_data_rows = o_ref.shape[1] // LANES

    def copy(p_ref, s, k, r):
        src0 = pl.multiple_of(p_ref[k, r] * ROW_PITCH, 8)
        dst0 = pl.multiple_of(r * ROW_PITCH, 8)
        return pltpu.make_async_copy(ys_hbm.at[pl.ds(src0, n_data_rows), :],
                                     gbuf.at[s * top_k + k, pl.ds(dst0, n_data_rows), :], sem.at[s])

    def start(p_ref, s):
        for k in range(top_k):
            def body(r2, carry, k=k):
                for prio in range(2):
                    copy(p_ref, s, k, 2 * r2 + prio).start(priority=prio)
                return carry

            lax.fori_loop(0, tc // 2, body, 0)

    @pl.when(i == 0)
    def _():
        start(pos_ref, 0)

    @pl.when(i + 1 < n)
    def _():
        start(pos_next_ref, 1 - slot)

    def wait_body(r, carry):
        copy(pos_ref, slot, 0, 0).wait()
        return carry

    lax.fori_loop(0, top_k * tc, wait_body, 0)
    w = w_ref[...]
    cols = []
    for c in range(n_data_rows):
        acc = None
        for k in range(top_k):
            term = w[:, k:k + 1] * gbuf[slot * top_k + k, pl.ds(c, tc, stride=ROW_PITCH), :]
            acc = term if acc is None else acc + term
        cols.append(acc)
    o_ref[...] = jnp.concatenate(cols, axis=1)


def moe_combine(ys, pos_tiles, w_tok, d, *, tc):
    n_tiles, top_k, _ = pos_tiles.shape
    smem_pos = functools.partial(pl.BlockSpec, (None, top_k, tc), memory_space=pltpu.SMEM)
    return pl.pallas_call(
        _moe_combine_kernel,
        grid=(n_tiles,),
        in_specs=[smem_pos(lambda i: (i, 0, 0)),
                  smem_pos(lambda i: (jnp.minimum(i + 1, n_tiles - 1), 0, 0)),
                  pl.BlockSpec((tc, top_k), lambda i: (i, 0)),
                  pl.BlockSpec(memory_space=pl.ANY)],
        out_specs=pl.BlockSpec((tc, d), lambda i: (i, 0)),
        out_shape=jax.ShapeDtypeStruct((n_tiles * tc, d), F32),
        scratch_shapes=[pltpu.VMEM((2 * top_k, tc * ROW_PITCH, LANES), F32), pltpu.SemaphoreType.DMA((2,))],
        compiler_params=_cparams(("arbitrary",), V7X_VMEM_LIMIT),
        name="moe_combine",
    )(pos_tiles, pos_tiles, w_tok, ys)


_HI = lax.Precision.HIGHEST


def _cmul(ar, ai, br, bi):
    return ar * br - ai * bi, ar * bi + ai * br


def s5_operators(a_re, a_im, log_dt, b_re, b_im, c_re, c_im):
    t = S5_T
    dt = jnp.exp(log_dt)[..., None]
    adt_re, adt_im = a_re * dt, a_im * dt
    tau = jnp.arange(t + 1, dtype=F32)[None, None, :, None]
    mag = jnp.exp(adt_re[:, :, None, :] * tau)
    ang = adt_im[:, :, None, :] * tau
    pw_re, pw_im = mag * jnp.cos(ang), mag * jnp.sin(ang)
    ab_re, ab_im = pw_re[:, :, 1], pw_im[:, :, 1]
    den = a_re * a_re + a_im * a_im
    nr, ni = ab_re - 1.0, ab_im
    f_re = (nr * a_re + ni * a_im) / den
    f_im = (ni * a_re - nr * a_im) / den
    bb_re, bb_im = _cmul(f_re[..., None], f_im[..., None], b_re, b_im)
    m1_re, m1_im = _cmul(pw_re[..., None], pw_im[..., None], bb_re[:, :, None], bb_im[:, :, None])
    kk = (jnp.einsum('dgpn,dgtnq->dgtpq', c_re, m1_re[:, :, :t], precision=_HI)
          - jnp.einsum('dgpn,dgtnq->dgtpq', c_im, m1_im[:, :, :t], precision=_HI))
    ti = jnp.arange(t)
    lag = ti[:, None] - ti[None, :]
    kg = kk[:, :, jnp.clip(lag, 0, t - 1)]
    kg = jnp.where((lag >= 0)[None, None, :, :, None, None], kg, 0.0)
    nd, g = a_re.shape[0], a_re.shape[1]
    p = b_re.shape[-1]
    n = a_re.shape[-1]
    inc = jnp.stack([m1_re[:, :, t - 1 - ti], m1_im[:, :, t - 1 - ti]], axis=3)
    w_re, w_im = _cmul(c_re[:, :, None], c_im[:, :, None],
                       pw_re[:, :, 1:, None, :], pw_im[:, :, 1:, None, :])
    mst = jnp.stack([w_re, -w_im], axis=2)

    def reverse_backward(v, axes):
        return jnp.stack([v[0], jnp.flip(v[1], axes)], axis=0)

    kg = reverse_backward(kg, (1, 2))
    inc = reverse_backward(inc, (1,))
    mst = reverse_backward(mst, (2,))
    gpb = S5_LANES // p
    gb = g // gpb
    fw = t * S5_LANES
    assert t * p == S5_LANES and 2 * n == S5_LANES
    toep_c = kg.reshape(nd, gb, gpb, t, t, p, p).transpose(0, 1, 4, 2, 6, 3, 5)
    minc_c = inc.reshape(nd, gb, gpb, t, 2, n, p).transpose(0, 1, 3, 2, 6, 4, 5)
    mstate_c = mst.reshape(nd, gb, gpb, 2, t, p, n).transpose(0, 1, 3, 2, 6, 4, 5)
    a_t_re = pw_re[:, :, t].reshape(nd, 1, g * n)
    a_t_im = pw_im[:, :, t].reshape(nd, 1, g * n)
    return (toep_c.reshape(nd, gb, fw, S5_LANES).astype(BF16), minc_c.reshape(nd, gb, fw, S5_LANES).astype(BF16),
            mstate_c.reshape(nd, gb, fw, S5_LANES).astype(BF16), a_t_re, a_t_im)


def rotary_tables(n_pos, half):
    freq = ROPE_BASE ** (-jnp.arange(half, dtype=F32) / half)
    ang = jnp.arange(n_pos, dtype=F32)[:, None] * freq[None, :]
    return jnp.cos(ang), jnp.sin(ang)


def kernel(x, c, ctx, c_ctx, emb_ln_g, emb_ln_b, ada_w, ada_b, w_in, conv_w, conv_b, conv_ln_g, conv_ln_b, conv_proj, ret_decay_logit, ret_proj, s5_a_re, s5_a_im, s5_log_dt, s5_b_re, s5_b_im, s5_c_re, s5_c_im, s5_d, s5_w_glu, s5_proj, w_out, ln1_g, ln1_b, ln2_g, ln2_b, router_w, router_bias, exp_w_gate, exp_w_up, exp_w_down, sh_w_gate, sh_w_up, sh_w_down):
    n_batch, seq, d = x.shape
    lc = ctx.shape[1]
    depth = w_in.shape[0]
    conv_c = conv_w.shape[2]
    ret_w = ret_proj.shape[1]
    s5_c = s5_d.shape[1]
    s5_g = s5_c // S5_P
    n_exp, _, exp_ff = exp_w_gate.shape[1:]
    sh_ff = sh_w_gate.shape[2]
    rows = seq // GRID_W
    assert lc == ROW_TILE and seq % ROW_TILE == 0 and n_batch + 1 <= MOD_ROWS
    assert RET_HEADS * RET_DK == ret_w and S5_T * S5_LANES == 2 * (S5_LANES // S5_P) * S5_N
    assert s5_c % S5_LANES == 0 and lc % S5_T == 0 and rows % S5_T == 0
    tiles_per_batch = seq // ROW_TILE
    n_lat_tiles = n_batch * tiles_per_batch
    n_tiles = n_lat_tiles + n_batch
    n_lat = n_batch * seq
    alpha = (2.0 * depth) ** 0.25
    ntok = n_tiles * ROW_TILE
    tm_mm = _largest_row_tile(ntok, 2)
    tm_ffn = _largest_row_tile(ntok, 2) if ntok % 544 else 544
    col_a, col_q = 0, 2 * conv_c
    col_g = col_q + 3 * ret_w
    col_u = col_g + ret_w
    col_s = col_u + s5_c

    group_fn = functools.partial(_group_of_tile, n_latent_tiles=n_lat_tiles,
                                 tiles_per_batch=tiles_per_batch, n_batch=n_batch)

    tokens = jnp.concatenate([x.reshape(n_lat, d), ctx.reshape(n_batch * lc, d)], axis=0)
    cvec = jnp.concatenate([c, c_ctx[None, :], jnp.zeros((MOD_ROWS - n_batch - 1, d), F32)], axis=0)
    mods = [ada_modulation(cvec, ada_w, ada_b[i][None, :], i) for i in range(depth)]
    cos_tab, sin_tab = rotary_tables(lc + seq, RET_DK // 2)
    log_g = jax.nn.log_sigmoid(ret_decay_logit.astype(F32))

    xl, h = resid_ln_mod(tokens, [], None, 0, emb_ln_g[None, :], emb_ln_b[None, :], mods[0], 0, 1,
                         alpha=1.0, group_fn=group_fn, n_tiles=n_tiles)

    for i in range(depth):
        last = i == depth - 1
        mod = mods[i]
        z = matmul_stacked_w(h, w_in, i, BF16, tm_mm, 512)
        act_a = jnp.concatenate([
            conformer_conv_act(z, conv_w[i], conv_b[i][None, :], conv_ln_g[i][None, :], conv_ln_b[i][None, :],
                               row_tile0=0, n_tiles=n_lat_tiles, seg=GRID_W),
            conformer_conv_act(z, conv_w[i], conv_b[i][None, :], conv_ln_g[i][None, :], conv_ln_b[i][None, :],
                               row_tile0=n_lat_tiles, n_tiles=n_batch, seg=lc)], axis=0)
        o_f, o_b = retention_scan(z, log_g[i], cos_tab, sin_tab, n_batch=n_batch,
                                  tiles_per_batch=tiles_per_batch, q_col0=col_q // RET_DK)
        act_b = retention_post(o_f, o_b, z, g_col=col_g // ret_w)
        u = z[:, col_u:col_u + s5_c].astype(BF16)
        gb = s5_c // S5_LANES
        fw = S5_T * S5_LANES
        u_lat = u[:n_lat].reshape(n_batch, rows, GRID_W, gb, S5_LANES).transpose(0, 3, 2, 1, 4)
        u_ctx = u[n_lat:].reshape(n_batch, lc, gb, S5_LANES).transpose(0, 2, 1, 3)
        u_fold = jnp.concatenate([u_ctx.reshape(n_batch, gb, lc // S5_T, fw),
                                  u_lat.reshape(n_batch, gb, seq // S5_T, fw)], axis=2)
        ops = s5_operators(s5_a_re[i], s5_a_im[i], s5_log_dt[i], s5_b_re[i], s5_b_im[i], s5_c_re[i], s5_c_im[i])
        y_fold = s5_chunked(u_fold, *ops, n_ctx_chunk=lc // S5_T)
        y_ctx = y_fold[:, :, :lc // S5_T].reshape(n_batch, gb, lc, S5_LANES).transpose(0, 2, 1, 3)
        y_lat = y_fold[:, :, lc // S5_T:].reshape(n_batch, gb, GRID_W, rows, S5_LANES).transpose(0, 3, 2, 1, 4)
        y_s5 = jnp.concatenate([y_lat.reshape(n_lat, s5_c), y_ctx.reshape(n_batch * lc, s5_c)], axis=0)
        act_c = s5_post(y_s5, z, s5_d[i][None, :], s5_w_glu, i, u_col=col_u // s5_c)
        merged = merge_branches(act_a, act_b, act_c, z, conv_proj, ret_proj, s5_proj, i,
                                s_col0=col_s, tm=tm_mm, tn=512)
        y_mix = matmul_stacked_w(merged, w_out, i, F32, tm_mm, 512)
        xl, h2, h2_f32, eidx, wk = resid_ln_mod(
            xl, [y_mix], mod, 2, ln1_g[i][None, :], ln1_b[i][None, :], mod, 3, 4,
            alpha=alpha, group_fn=group_fn, n_tiles=n_tiles,
            router=(router_w[i].T, router_bias[i][:, None]))
        counts = jnp.sum((eidx[None, :, :] == jnp.arange(n_exp, dtype=jnp.int32)[:, None, None]).astype(jnp.int32),
                         axis=(1, 2))
        padded = ((counts + MOE_ROW_TILE - 1) // MOE_ROW_TILE) * MOE_ROW_TILE
        ends = jnp.cumsum(padded)
        n_sorted_tiles = (ntok * TOP_K) // MOE_ROW_TILE + n_exp
        tile_start = jnp.arange(n_sorted_tiles, dtype=jnp.int32) * MOE_ROW_TILE
        tile_expert = jnp.minimum(jnp.sum((ends[None, :] <= tile_start[:, None]).astype(jnp.int32), axis=1),
                                  n_exp - 1)
        n_valid = (ends[-1:] // MOE_ROW_TILE).astype(jnp.int32)
        pos = moe_positions(eidx, (ends - padded).astype(F32)[:, None])
        src_tok = jnp.zeros((n_sorted_tiles * MOE_ROW_TILE,), jnp.int32).at[pos.reshape(-1)].set(
            jnp.tile(jnp.arange(ntok, dtype=jnp.int32), TOP_K))
        y_sorted = moe_ffn_sorted(h2_f32, src_tok.reshape(n_sorted_tiles, 1, MOE_ROW_TILE), tile_expert, n_valid,
                                  exp_w_gate, exp_w_up, exp_w_down, i, tm=MOE_ROW_TILE)
        pos_tiles = pos.reshape(TOP_K, ntok // MOE_COMBINE_TILE, MOE_COMBINE_TILE).transpose(1, 0, 2)
        y_routed = moe_combine(y_sorted, pos_tiles, wk.T, d, tc=MOE_COMBINE_TILE)
        y_shared = ffn_blocks(h2, sh_w_gate, sh_w_up, sh_w_down, i, exp_ff, tm=tm_ffn)
        if last:
            (xl,) = resid_ln_mod(xl, [y_routed, y_shared], mod, 5, ln2_g[i][None, :], ln2_b[i][None, :],
                                 None, 0, 0, alpha=alpha, group_fn=group_fn, n_tiles=n_lat_tiles)
        else:
            xl, h = resid_ln_mod(xl, [y_routed, y_shared], mod, 5, ln2_g[i][None, :], ln2_b[i][None, :],
                                 mods[i + 1], 0, 1, alpha=alpha, group_fn=group_fn, n_tiles=n_tiles)
    return xl.reshape(n_batch, seq, d)
```

```python
import functools
import math

import jax
import jax.numpy as jnp
from jax import lax
from jax.experimental import pallas as pl
from jax.experimental.pallas import tpu as pltpu

F32 = jnp.float32
BF16 = jnp.bfloat16

GRID_W = 64
RET_HEADS = 8
RET_DK = 256
S5_P = 16
S5_N = 64
ROPE_BASE = 10000.0
N_GROUPS = 8
TOPK_GROUPS = 4
TOP_K = 8
ROUTED_SCALE = 2.5
LN_EPS = 1e-5
HEAD_NORM_EPS = 1e-5
NEG_BIG = -1e30
N_BRANCH = 3

ROW_TILE = 256
S5_LANES = 128
S5_T = 8
MOD_ROWS = 8
MOE_ROW_TILE = 256
MOE_COMBINE_TILE = 64
LANES = 128
ROW_PITCH = 24
V7X_VMEM_LIMIT = 56 * 1024 * 1024


def _largest_row_tile(ntok, max_tiles):
    n = ntok // ROW_TILE
    k = max(t for t in range(1, max_tiles + 1) if n % t == 0)
    return k * ROW_TILE


def _cparams(sem, vmem=None):
    return pltpu.CompilerParams(dimension_semantics=sem, vmem_limit_bytes=vmem)


def _split_bf16(v):
    hi = v.astype(BF16)
    lo = (v - hi.astype(F32)).astype(BF16)
    return hi, lo


def _dot(a, b):
    return jnp.dot(a, b, preferred_element_type=F32)


def _dot_nt(a, b):
    return lax.dot_general(a, b, (((1,), (1,)), ((), ())), preferred_element_type=F32)


def _dot_tn(a, b):
    return lax.dot_general(a, b, (((0,), (0,)), ((), ())), preferred_element_type=F32)


def _sigmoid(v):
    return 1.0 / (1.0 + jnp.exp(-v))


def _silu(v):
    return v * _sigmoid(v)


_HIGH_HALF = 0xFFFF0000


def _pack_bf16_pair(lo, hi):
    lo_bits = pltpu.bitcast(lo.astype(BF16).astype(F32), jnp.uint32) >> 16
    hi_bits = pltpu.bitcast(hi.astype(BF16).astype(F32), jnp.uint32) & jnp.uint32(_HIGH_HALF)
    return lo_bits | hi_bits


def _unpack_bf16_pair(words):
    lo = pltpu.bitcast(words << 16, F32)
    hi = pltpu.bitcast(words & jnp.uint32(_HIGH_HALF), F32)
    return lo, hi


def _token_data_rows(d):
    rows = d // (2 * LANES)
    assert rows * 2 * LANES == d and rows <= ROW_PITCH
    return rows


def _store_packed(ref, v):
    n, d = v.shape
    k = _token_data_rows(d)
    for c in range(k):
        lo = v[:, c * LANES:(c + 1) * LANES]
        hi = v[:, (k + c) * LANES:(k + c + 1) * LANES]
        ref[pl.ds(c, n, stride=ROW_PITCH), :] = _pack_bf16_pair(lo, hi)
    for c in range(k, ROW_PITCH):
        ref[pl.ds(c, n, stride=ROW_PITCH), :] = jnp.zeros((n, LANES), jnp.uint32)


def _load_packed(ref, n, d):
    k = _token_data_rows(d)
    parts = [_unpack_bf16_pair(ref[pl.ds(c, n, stride=ROW_PITCH), :]) for c in range(k)]
    return jnp.concatenate([p[0] for p in parts] + [p[1] for p in parts], axis=1)


def _layer_norm_rows(v, g, b, eps):
    mu = jnp.mean(v, axis=-1, keepdims=True)
    vc = v - mu
    var = jnp.mean(vc * vc, axis=-1, keepdims=True)
    return vc * lax.rsqrt(var + eps) * g + b


def _ada_kernel(c_ref, w_ref, b_ref, o_ref):
    c = _silu(c_ref[...])
    ch, cl = _split_bf16(c)
    wh, wl = _split_bf16(w_ref[...])
    o_ref[...] = _dot(ch, wh) + _dot(ch, wl) + _dot(cl, wh) + b_ref[...]


def ada_modulation(cvec, ada_w, ada_b_l, layer):
    _, d, n = ada_w.shape
    tn = 512
    return pl.pallas_call(
        _ada_kernel,
        grid=(n // tn,),
        in_specs=[
            pl.BlockSpec((MOD_ROWS, d), lambda j: (0, 0)),
            pl.BlockSpec((None, d, tn), lambda j: (layer, 0, j)),
            pl.BlockSpec((1, tn), lambda j: (0, j)),
        ],
        out_specs=pl.BlockSpec((MOD_ROWS, tn), lambda j: (0, j)),
        out_shape=jax.ShapeDtypeStruct((MOD_ROWS, n), F32),
        compiler_params=_cparams(("arbitrary",), V7X_VMEM_LIMIT),
        name="ada_modulation",
    )(cvec, ada_w, ada_b_l)


def _group_of_tile(i, n_latent_tiles, tiles_per_batch, n_batch):
    return jnp.where(i < n_latent_tiles, i // tiles_per_batch, n_batch)


def _route(h, rwt_ref, rb_ref):
    hh, hl = _split_bf16(h)
    wh, wl = _split_bf16(rwt_ref[...])
    logits = _dot_nt(wh, hh) + _dot_nt(wh, hl) + _dot_nt(wl, hh)
    scores = _sigmoid(logits)
    sel = scores + rb_ref[...]
    n_e, tm = sel.shape
    per = n_e // N_GROUPS
    shape3 = (N_GROUPS, per, tm)
    sel3 = sel.reshape(shape3)
    io_e = lax.broadcasted_iota(jnp.int32, shape3, 1)
    io_g = lax.broadcasted_iota(jnp.int32, shape3, 0)
    m1 = jnp.max(sel3, axis=1, keepdims=True)
    first = jnp.min(jnp.where(sel3 == m1, io_e, per), axis=1, keepdims=True)
    m2 = jnp.max(jnp.where(io_e == first, -jnp.inf, sel3), axis=1, keepdims=True)
    work = m1 + m2
    iog1 = lax.broadcasted_iota(jnp.int32, work.shape, 0)
    gsel = jnp.zeros(work.shape, F32)
    for _ in range(TOPK_GROUPS):
        m = jnp.max(work, axis=0, keepdims=True)
        fi = jnp.min(jnp.where(work == m, iog1, N_GROUPS), axis=0, keepdims=True)
        hit = iog1 == fi
        gsel = jnp.where(hit, 1.0, gsel)
        work = jnp.where(hit, -jnp.inf, work)
    work = jnp.where(jnp.broadcast_to(gsel, shape3) > 0.0, sel3, NEG_BIG)
    flat = io_g * per + io_e
    scores3 = scores.reshape(shape3)
    picked, picked_score = [], []
    for _ in range(TOP_K):
        m = jnp.max(jnp.max(work, axis=1, keepdims=True), axis=0, keepdims=True)
        cand = jnp.where(work == m, flat, n_e)
        fi = jnp.min(jnp.min(cand, axis=1, keepdims=True), axis=0, keepdims=True)
        hit = flat == fi
        sk = jnp.sum(jnp.sum(jnp.where(hit, scores3, 0.0), axis=1, keepdims=True), axis=0, keepdims=True)
        picked.append(fi.reshape(1, tm))
        picked_score.append(sk.reshape(1, tm))
        work = jnp.where(hit, -jnp.inf, work)
    eidx = jnp.concatenate(picked, axis=0)
    w = jnp.concatenate(picked_score, axis=0)
    return eidx, ROUTED_SCALE * w / jnp.sum(w, axis=0, keepdims=True)


def _resid_ln_mod_kernel(*refs, alpha, n_y, has_mod, has_router, group_fn):
    it = iter(refs)
    x_ref = next(it)
    y_refs = [next(it) for _ in range(n_y)]
    gate_ref = next(it) if n_y else None
    g_ref, b_ref = next(it), next(it)
    shift_ref = scale_ref = rwt_ref = rb_ref = None
    if has_mod:
        shift_ref, scale_ref = next(it), next(it)
    if has_router:
        rwt_ref, rb_ref = next(it), next(it)
    xl_ref = next(it)
    h_ref = next(it) if has_mod else None
    hf_ref, eidx_ref, wk_ref = (next(it), next(it), next(it)) if has_router else (None, None, None)

    grp = group_fn(pl.program_id(0))
    v = x_ref[...]
    if n_y:
        y = y_refs[0][...]
        for r in y_refs[1:]:
            y = y + r[...]
        v = alpha * v + gate_ref[pl.ds(grp, 1), :] * y
    xl = _layer_norm_rows(v, g_ref[...], b_ref[...], LN_EPS)
    xl_ref[...] = xl
    if has_mod:
        h = xl * (1.0 + scale_ref[pl.ds(grp, 1), :]) + shift_ref[pl.ds(grp, 1), :]
        h_ref[...] = h.astype(h_ref.dtype)
        if has_router:
            _store_packed(hf_ref, h)
            eidx_ref[...], wk_ref[...] = _route(h, rwt_ref, rb_ref)


def resid_ln_mod(x, ys, gate_mod, gate_col, ln_g, ln_b, mod, shift_col, scale_col, *,
                 alpha, group_fn, n_tiles, router=None):
    d = x.shape[1]
    n_y = len(ys)
    has_mod = mod is not None
    has_router = router is not None
    row = pl.BlockSpec((ROW_TILE, d), lambda i: (i, 0))
    vec = pl.BlockSpec((1, d), lambda i: (0, 0))
    args, specs = [x], [row]
    for y in ys:
        args.append(y)
        specs.append(row)
    if n_y:
        args.append(gate_mod)
        specs.append(pl.BlockSpec((MOD_ROWS, d), lambda i: (0, gate_col)))
    args += [ln_g, ln_b]
    specs += [vec, vec]
    if has_mod:
        args += [mod, mod]
        specs += [pl.BlockSpec((MOD_ROWS, d), lambda i: (0, shift_col)),
                  pl.BlockSpec((MOD_ROWS, d), lambda i: (0, scale_col))]
    out_shapes = [jax.ShapeDtypeStruct((n_tiles * ROW_TILE, d), F32)]
    out_specs = [row]
    if has_mod:
        out_shapes.append(jax.ShapeDtypeStruct((n_tiles * ROW_TILE, d), BF16))
        out_specs.append(row)
    if has_router:
        rwt, rb = router
        n_e = rwt.shape[0]
        args += [rwt, rb]
        specs += [pl.BlockSpec((n_e, d), lambda i: (0, 0)), pl.BlockSpec((n_e, 1), lambda i: (0, 0))]
        out_shapes += [jax.ShapeDtypeStruct((n_tiles * ROW_TILE * ROW_PITCH, LANES), jnp.uint32),
                       jax.ShapeDtypeStruct((TOP_K, n_tiles * ROW_TILE), jnp.int32),
                       jax.ShapeDtypeStruct((TOP_K, n_tiles * ROW_TILE), F32)]
        out_specs += [pl.BlockSpec((ROW_TILE * ROW_PITCH, LANES), lambda i: (i, 0)),
                      pl.BlockSpec((TOP_K, ROW_TILE), lambda i: (0, i)),
                      pl.BlockSpec((TOP_K, ROW_TILE), lambda i: (0, i))]
    kern = functools.partial(_resid_ln_mod_kernel, alpha=alpha, n_y=n_y, has_mod=has_mod,
                             has_router=has_router, group_fn=group_fn)
    return pl.pallas_call(
        kern, grid=(n_tiles,), in_specs=specs, out_specs=out_specs, out_shape=out_shapes,
        compiler_params=_cparams(("arbitrary",), V7X_VMEM_LIMIT),
        name="resid_ln_mod",
    )(*args)


def _mm_kernel(x_ref, w_ref, o_ref, wbf_ref):
    @pl.when(pl.program_id(1) == 0)
    def _():
        wbf_ref[...] = w_ref[...].astype(BF16)

    o_ref[...] = _dot(x_ref[...], wbf_ref[...]).astype(o_ref.dtype)


def matmul_stacked_w(x, w, layer, out_dtype, tm, tn):
    m, k = x.shape
    n = w.shape[2]
    return pl.pallas_call(
        _mm_kernel,
        grid=(n // tn, m // tm),
        in_specs=[
            pl.BlockSpec((tm, k), lambda j, i: (i, 0)),
            pl.BlockSpec((None, k, tn), lambda j, i: (layer, 0, j)),
        ],
        out_specs=pl.BlockSpec((tm, tn), lambda j, i: (i, j)),
        out_shape=jax.ShapeDtypeStruct((m, n), out_dtype),
        scratch_shapes=[pltpu.VMEM((k, tn), BF16)],
        compiler_params=_cparams(("arbitrary", "arbitrary"), V7X_VMEM_LIMIT),
        name="matmul",
    )(x, w)


def _conv_kernel(a1_ref, a2_ref, w_ref, cb_ref, g_ref, b_ref, o_ref, pad_ref, y_ref, *, seg, n_tap):
    half = n_tap // 2
    front = ((half + 7) // 8) * 8
    nseg = ROW_TILE // seg
    c = a1_ref.shape[1]
    u = a1_ref[...].astype(F32) * _sigmoid(a2_ref[...].astype(F32))
    pad_ref[...] = jnp.zeros(pad_ref.shape, F32)
    for s in range(nseg):
        pad_ref[s, front:front + seg, :] = u[s * seg:(s + 1) * seg, :]
    lanes = 128

    def chunk(ci, carry):
        c0 = pl.multiple_of(ci * lanes, lanes)
        acc = jnp.zeros((nseg, seg, lanes), F32)
        for k in range(n_tap):
            off = front - half + k
            acc = acc + w_ref[k:k + 1, pl.ds(c0, lanes)] * pad_ref[:, off:off + seg, pl.ds(c0, lanes)]
        y_ref[:, pl.ds(c0, lanes)] = acc.reshape(ROW_TILE, lanes)
        return carry

    lax.fori_loop(0, c // lanes, chunk, 0)
    y = y_ref[...] + cb_ref[...]
    o_ref[...] = _silu(_layer_norm_rows(y, g_ref[...], b_ref[...], LN_EPS)).astype(o_ref.dtype)


def conformer_conv_act(z, conv_w_l, conv_b_l, ln_g_l, ln_b_l, *, row_tile0, n_tiles, seg):
    n_tap, c = conv_w_l.shape
    half = n_tap // 2
    front = ((half + 7) // 8) * 8
    nseg = ROW_TILE // seg
    vec = pl.BlockSpec((1, c), lambda i: (0, 0))
    kern = functools.partial(_conv_kernel, seg=seg, n_tap=n_tap)
    return pl.pallas_call(
        kern,
        grid=(n_tiles,),
        in_specs=[
            pl.BlockSpec((ROW_TILE, c), lambda i: (row_tile0 + i, 0)),
            pl.BlockSpec((ROW_TILE, c), lambda i: (row_tile0 + i, 1)),
            pl.BlockSpec((n_tap, c), lambda i: (0, 0)),
            vec, vec, vec,
        ],
        out_specs=pl.BlockSpec((ROW_TILE, c), lambda i: (i, 0)),
        out_shape=jax.ShapeDtypeStruct((n_tiles * ROW_TILE, c), BF16),
        scratch_shapes=[pltpu.VMEM((nseg, seg + 2 * front, c), F32), pltpu.VMEM((ROW_TILE, c), F32)],
        compiler_params=_cparams(("arbitrary",), V7X_VMEM_LIMIT),
        name="conformer_conv",
    )(z, z, conv_w_l, conv_b_l, ln_g_l, ln_b_l)


def _rotary(t, cos, sin):
    half = t.shape[1] // 2
    t1, t2 = t[:, :half], t[:, half:]
    return jnp.concatenate([t1 * cos - t2 * sin, t1 * sin + t2 * cos], axis=1)


def _retention_kernel(logg_ref, qf_ref, kf_ref, vf_ref, cf_ref, sf_ref,
                      qb_ref, kb_ref, vb_ref, cb_ref, sb_ref,
                      of_ref, ob_ref, state_ref):
    h = pl.program_id(1)
    n = pl.program_id(2)
    c = ROW_TILE
    k_scale = RET_DK ** -0.5

    @pl.when(n == 0)
    def _():
        state_ref[...] = jnp.zeros(state_ref.shape, F32)

    row = lax.broadcasted_iota(jnp.int32, (c, c), 0)
    col = lax.broadcasted_iota(jnp.int32, (c, c), 1)
    ridx = lax.broadcasted_iota(jnp.int32, (c, 1), 0).astype(F32)

    def one_direction(d, q_ref, k_ref, v_ref, cos_ref, sin_ref, o_ref):
        lg = logg_ref[d, h]
        cos, sin = cos_ref[...], sin_ref[...]
        q = _rotary(q_ref[...].astype(F32), cos, sin)
        k = _rotary(k_ref[...].astype(F32), cos, sin) * k_scale
        v = v_ref[...].astype(BF16)
        dist = (row - col) if d == 0 else (col - row)
        decay = jnp.where(dist >= 0, jnp.exp(lg * jnp.maximum(dist, 0).astype(F32)), 0.0)
        qb = q.astype(BF16)
        scores = _dot_nt(qb, k.astype(BF16)) * decay
        inner = _dot(scores.astype(BF16), v)
        to_prev = (ridx + 1.0) if d == 0 else (c - ridx)
        to_end = (c - 1.0 - ridx) if d == 0 else ridx
        s_prev = state_ref[d]
        cross = _dot(qb, s_prev.astype(BF16)) * jnp.exp(lg * to_prev)
        o_ref[...] = inner + cross
        kw = (k * jnp.exp(lg * to_end)).astype(BF16)
        state_ref[d] = jnp.exp(lg * jnp.full((1, 1), float(c), F32)) * s_prev + _dot_tn(kw, v)

    one_direction(0, qf_ref, kf_ref, vf_ref, cf_ref, sf_ref, of_ref)
    one_direction(1, qb_ref, kb_ref, vb_ref, cb_ref, sb_ref, ob_ref)


def retention_scan(z, log_g, cos_tab, sin_tab, *, n_batch, tiles_per_batch, q_col0):
    ntok = z.shape[0]
    n_lat = n_batch * tiles_per_batch
    n_steps = tiles_per_batch + 1
    dk = RET_DK
    hh = RET_HEADS

    def row_f(b, n):
        return jnp.where(n == 0, n_lat + b, b * tiles_per_batch + n - 1)

    def row_b(b, n):
        return jnp.where(n == 0, n_lat + b, b * tiles_per_batch + tiles_per_batch - n)

    def pos_f(n):
        return n

    def pos_b(n):
        return jnp.where(n == 0, 0, tiles_per_batch + 1 - n)

    def zspec(rowfn, sec):
        return pl.BlockSpec((ROW_TILE, dk), lambda b, h, n: (rowfn(b, n), q_col0 + sec * hh + h))

    def tspec(posfn):
        return pl.BlockSpec((ROW_TILE, dk // 2), lambda b, h, n: (posfn(n), 0))

    def ospec(rowfn):
        return pl.BlockSpec((ROW_TILE, dk), lambda b, h, n: (rowfn(b, n), h))

    smem = pl.BlockSpec(memory_space=pltpu.SMEM)
    return pl.pallas_call(
        _retention_kernel,
        grid=(n_batch, hh, n_steps),
        in_specs=[smem,
                  zspec(row_f, 0), zspec(row_f, 1), zspec(row_f, 2), tspec(pos_f), tspec(pos_f),
                  zspec(row_b, 0), zspec(row_b, 1), zspec(row_b, 2), tspec(pos_b), tspec(pos_b)],
        out_specs=[ospec(row_f), ospec(row_b)],
        out_shape=[jax.ShapeDtypeStruct((ntok, hh * dk), F32)] * 2,
        scratch_shapes=[pltpu.VMEM((2, dk, dk), F32)],
        compiler_params=_cparams(("arbitrary", "arbitrary", "arbitrary"), V7X_VMEM_LIMIT),
        name="retention_scan",
    )(log_g, z, z, z, cos_tab, sin_tab, z, z, z, cos_tab, sin_tab)


def _ret_post_kernel(of_ref, ob_ref, g_ref, o_ref):
    dk = RET_DK
    for h in range(RET_HEADS):
        sl = slice(h * dk, (h + 1) * dk)
        o = of_ref[:, sl] + ob_ref[:, sl]
        mu = jnp.mean(o, axis=-1, keepdims=True)
        oc = o - mu
        var = jnp.mean(oc * oc, axis=-1, keepdims=True)
        on = oc * lax.rsqrt(var + HEAD_NORM_EPS)
        o_ref[:, sl] = (on * _silu(g_ref[:, sl].astype(F32))).astype(o_ref.dtype)


def retention_post(o_f, o_b, z, *, g_col):
    ntok, w = o_f.shape
    row = pl.BlockSpec((ROW_TILE, w), lambda i: (i, 0))
    return pl.pallas_call(
        _ret_post_kernel,
        grid=(ntok // ROW_TILE,),
        in_specs=[row, row, pl.BlockSpec((ROW_TILE, w), lambda i: (i, g_col))],
        out_specs=row,
        out_shape=jax.ShapeDtypeStruct((ntok, w), BF16),
        compiler_params=_cparams(("arbitrary",), V7X_VMEM_LIMIT),
        name="retention_post",
    )(o_f, o_b, z)


def _expand_block_diag(compact, w, rows_per_group):
    gpb = S5_LANES // S5_P
    n_rows, k = compact.shape
    n_cols = k * gpb
    lw, lg, lr = w.bit_length() - 1, gpb.bit_length() - 1, rows_per_group.bit_length() - 1
    i = lax.broadcasted_iota(jnp.int32, (k, n_cols), 0)
    c = lax.broadcasted_iota(jnp.int32, (k, n_cols), 1)
    src = ((c >> (lw + lg)) << lw) + (c & (w - 1))
    rep = jnp.where(i == src, 1.0, 0.0).astype(BF16)
    full = _dot(compact, rep)
    r = lax.broadcasted_iota(jnp.int32, (n_rows, n_cols), 0)
    c2 = lax.broadcasted_iota(jnp.int32, (n_rows, n_cols), 1)
    keep = ((r >> lr) & (gpb - 1)) == ((c2 >> lw) & (gpb - 1))
    return jnp.where(keep, full, 0.0).astype(BF16)


def _s5_local_kernel(u_ref, kc_ref, ic_ref, yl_ref, xre_ref, xim_ref, toep_ref, minc_ref):
    @pl.when(pl.program_id(2) == 0)
    def _():
        toep_ref[...] = _expand_block_diag(kc_ref[...], S5_P, S5_P)
        minc_ref[...] = _expand_block_diag(ic_ref[...], S5_N, S5_P)

    u = u_ref[...]
    yl_ref[...] = _dot(u, toep_ref[...])
    xi = _dot(u, minc_ref[...])
    half = xi.shape[1] // 2
    xre_ref[...] = xi[:, :half]
    xim_ref[...] = xi[:, half:]


def _s5_scan_kernel(xre_ref, xim_ref, are_ref, aim_ref, ore_ref, oim_ref, *, n_ctx_chunk):
    n_chunk = xre_ref.shape[0]
    backward = pl.program_id(1) == 1
    ar, ai = are_ref[...], aim_ref[...]

    def step(i, carry):
        sr, si = carry
        rev = jnp.where(i < n_ctx_chunk, n_ctx_chunk - 1 - i, n_chunk + n_ctx_chunk - 1 - i)
        c = jnp.where(backward, rev, i)
        ore_ref[pl.ds(c, 1), :] = sr
        oim_ref[pl.ds(c, 1), :] = si
        nr = ar * sr - ai * si + xre_ref[pl.ds(c, 1), :]
        ni = ar * si + ai * sr + xim_ref[pl.ds(c, 1), :]
        return nr, ni

    zero = jnp.zeros(ar.shape, F32)
    lax.fori_loop(0, n_chunk, step, (zero, zero))


def _s5_state_kernel(xre_ref, xim_ref, mc_ref, yl_ref, y_ref, mst_ref):
    @pl.when(pl.program_id(1) == 0)
    def _():
        for d in range(2):
            mst_ref[d] = _expand_block_diag(mc_ref[d], S5_P, S5_N)

    y = yl_ref[0] + yl_ref[1]
    for d in range(2):
        x0 = jnp.concatenate([xre_ref[d], xim_ref[d]], axis=1).astype(BF16)
        y = y + _dot(x0, mst_ref[d])
    y_ref[...] = y


def s5_chunked(u_fold, toep, minc, mstate, a_re, a_im, *, n_ctx_chunk):
    nb, gb, nch, fw = u_fold.shape
    sw = fw // 2
    op = pl.BlockSpec((None, None, fw, S5_LANES), lambda g, d, b: (d, g, 0, 0))
    yl, xre, xim = pl.pallas_call(
        _s5_local_kernel,
        grid=(gb, 2, nb),
        in_specs=[pl.BlockSpec((None, None, nch, fw), lambda g, d, b: (b, g, 0, 0)), op, op],
        out_specs=[pl.BlockSpec((None, None, None, nch, fw), lambda g, d, b: (b, d, g, 0, 0)),
                   pl.BlockSpec((None, None, nch, sw), lambda g, d, b: (b, d, 0, g)),
                   pl.BlockSpec((None, None, nch, sw), lambda g, d, b: (b, d, 0, g))],
        out_shape=[jax.ShapeDtypeStruct((nb, 2, gb, nch, fw), F32),
                   jax.ShapeDtypeStruct((nb, 2, nch, gb * sw), F32),
                   jax.ShapeDtypeStruct((nb, 2, nch, gb * sw), F32)],
        scratch_shapes=[pltpu.VMEM((fw, fw), BF16), pltpu.VMEM((fw, fw), BF16)],
        compiler_params=_cparams(("arbitrary", "arbitrary", "arbitrary"), V7X_VMEM_LIMIT),
        name="s5_local",
    )(u_fold, toep, minc)
    scan_w = 2 * sw
    full = pl.BlockSpec((None, None, nch, scan_w), lambda b, d, j: (b, d, 0, j))
    avec = pl.BlockSpec((None, 1, scan_w), lambda b, d, j: (d, 0, j))
    x0re, x0im = pl.pallas_call(
        functools.partial(_s5_scan_kernel, n_ctx_chunk=n_ctx_chunk),
        grid=(nb, 2, gb * sw // scan_w),
        in_specs=[full, full, avec, avec],
        out_specs=[full, full],
        out_shape=[jax.ShapeDtypeStruct((nb, 2, nch, gb * sw), F32)] * 2,
        compiler_params=_cparams(("arbitrary", "arbitrary", "arbitrary"), V7X_VMEM_LIMIT),
        name="s5_scan",
    )(xre, xim, a_re, a_im)
    xcol = pl.BlockSpec((None, 2, nch, sw), lambda g, b: (b, 0, 0, g))
    return pl.pallas_call(
        _s5_state_kernel,
        grid=(gb, nb),
        in_specs=[xcol, xcol,
                  pl.BlockSpec((2, None, fw, S5_LANES), lambda g, b: (0, g, 0, 0)),
                  pl.BlockSpec((None, 2, None, nch, fw), lambda g, b: (b, 0, g, 0, 0))],
        out_specs=pl.BlockSpec((None, None, nch, fw), lambda g, b: (b, g, 0, 0)),
        out_shape=jax.ShapeDtypeStruct((nb, gb, nch, fw), F32),
        scratch_shapes=[pltpu.VMEM((2, fw, fw), BF16)],
        compiler_params=_cparams(("arbitrary", "arbitrary"), V7X_VMEM_LIMIT),
        name="s5_state",
    )(x0re, x0im, mstate, yl)


def _gelu_tanh(v):
    return 0.5 * v * (1.0 + jnp.tanh(math.sqrt(2.0 / math.pi) * (v + 0.044715 * v * v * v)))


def _s5_post_kernel(y_ref, u_ref, d_ref, w_ref, o_ref, wbf_ref):
    @pl.when(pl.program_id(0) == 0)
    def _():
        wbf_ref[...] = w_ref[...].astype(BF16)

    t = _gelu_tanh(y_ref[...] + d_ref[...] * u_ref[...].astype(F32))
    o_ref[...] = (t * _sigmoid(_dot(t.astype(BF16), wbf_ref[...]))).astype(o_ref.dtype)


def s5_post(y, z, s5_d_l, w_glu, layer, *, u_col):
    ntok, c = y.shape
    row = pl.BlockSpec((ROW_TILE, c), lambda i: (i, 0))
    return pl.pallas_call(
        _s5_post_kernel,
        grid=(ntok // ROW_TILE,),
        in_specs=[row, pl.BlockSpec((ROW_TILE, c), lambda i: (i, u_col)),
                  pl.BlockSpec((1, c), lambda i: (0, 0)),
                  pl.BlockSpec((None, c, c), lambda i: (layer, 0, 0))],
        out_specs=row,
        out_shape=jax.ShapeDtypeStruct((ntok, c), BF16),
        scratch_shapes=[pltpu.VMEM((c, c), BF16)],
        compiler_params=_cparams(("arbitrary",), V7X_VMEM_LIMIT),
        name="s5_post",
    )(y, z, s5_d_l, w_glu)


def _merge_kernel(a_ref, b_ref, c_ref, s0_ref, s1_ref, s2_ref, wa_ref, wb_ref, wc_ref, o_ref,
                  wa_bf, wb_bf, wc_bf):
    @pl.when(pl.program_id(1) == 0)
    def _():
        wa_bf[...] = wa_ref[...].astype(BF16)
        wb_bf[...] = wb_ref[...].astype(BF16)
        wc_bf[...] = wc_ref[...].astype(BF16)

    m = _sigmoid(s0_ref[...].astype(F32)) * _dot(a_ref[...], wa_bf[...])
    m = m + _sigmoid(s1_ref[...].astype(F32)) * _dot(b_ref[...], wb_bf[...])
    m = m + _sigmoid(s2_ref[...].astype(F32)) * _dot(c_ref[...], wc_bf[...])
    o_ref[...] = m.astype(o_ref.dtype)


def merge_branches(act_a, act_b, act_c, z, conv_proj, ret_proj, s5_proj, layer, *, s_col0, tm, tn):
    m = act_a.shape[0]
    d = conv_proj.shape[2]
    ka, kb, kc = act_a.shape[1], act_b.shape[1], act_c.shape[1]
    nblk = d // tn

    def aspec(k):
        return pl.BlockSpec((tm, k), lambda j, i: (i, 0))

    def sspec(br):
        return pl.BlockSpec((tm, tn), lambda j, i: (i, s_col0 // tn + br * nblk + j))

    def wspec(k):
        return pl.BlockSpec((None, k, tn), lambda j, i: (layer, 0, j))

    return pl.pallas_call(
        _merge_kernel,
        grid=(nblk, m // tm),
        in_specs=[aspec(ka), aspec(kb), aspec(kc), sspec(0), sspec(1), sspec(2),
                  wspec(ka), wspec(kb), wspec(kc)],
        out_specs=pl.BlockSpec((tm, tn), lambda j, i: (i, j)),
        out_shape=jax.ShapeDtypeStruct((m, d), BF16),
        scratch_shapes=[pltpu.VMEM((ka, tn), BF16), pltpu.VMEM((kb, tn), BF16), pltpu.VMEM((kc, tn), BF16)],
        compiler_params=_cparams(("arbitrary", "arbitrary"), V7X_VMEM_LIMIT),
        name="merge_branches",
    )(act_a, act_b, act_c, z, z, z, conv_proj, ret_proj, s5_proj)


def _ffn_kernel(x_ref, wg_ref, wu_ref, wd_ref, o_ref):
    @pl.when(pl.program_id(1) == 0)
    def _():
        o_ref[...] = jnp.zeros(o_ref.shape, F32)

    x = x_ref[...]
    hg = _dot(x, wg_ref[...].astype(BF16))
    hu = _dot(x, wu_ref[...].astype(BF16))
    o_ref[...] += _dot((_silu(hg) * hu).astype(BF16), wd_ref[...].astype(BF16))


def ffn_blocks(x, wg, wu, wd, layer, ff_block, *, tm):
    m, d = x.shape
    ff = wg.shape[2]
    up = pl.BlockSpec((None, d, ff_block), lambda i, e: (layer, 0, e))
    return pl.pallas_call(
        _ffn_kernel,
        grid=(m // tm, ff // ff_block),
        in_specs=[pl.BlockSpec((tm, d), lambda i, e: (i, 0)), up, up,
                  pl.BlockSpec((None, ff_block, d), lambda i, e: (layer, e, 0))],
        out_specs=pl.BlockSpec((tm, d), lambda i, e: (i, 0)),
        out_shape=jax.ShapeDtypeStruct((m, d), F32),
        compiler_params=_cparams(("arbitrary", "arbitrary"), V7X_VMEM_LIMIT),
        name="ffn_blocks",
    )(x, wg, wu, wd)


def _moe_positions_kernel(eidx_ref, off_ref, pos_ref, run_ref):
    @pl.when(pl.program_id(0) == 0)
    def _():
        run_ref[...] = jnp.zeros(run_ref.shape, F32)

    n_e = off_ref.shape[0]
    top_k, tm = eidx_ref.shape
    expert = lax.broadcasted_iota(jnp.int32, (n_e, tm), 0)
    eidx = eidx_ref[...]
    member = jnp.zeros((n_e, tm), F32)
    for k in range(top_k):
        member = member + jnp.where(eidx[k:k + 1, :] == expert, 1.0, 0.0)
    r = lax.broadcasted_iota(jnp.int32, (tm, tm), 0)
    c = lax.broadcasted_iota(jnp.int32, (tm, tm), 1)
    upper = jnp.where(r <= c, 1.0, 0.0).astype(BF16)
    incl = _dot(member.astype(BF16), upper)
    row_of = off_ref[...] + run_ref[...] + incl - member
    rows = [jnp.sum(jnp.where(eidx[k:k + 1, :] == expert, row_of, 0.0), axis=0, keepdims=True)
            for k in range(top_k)]
    pos_ref[...] = jnp.concatenate(rows, axis=0).astype(jnp.int32)
    run_ref[...] = run_ref[...] + jnp.sum(member, axis=1, keepdims=True)


def moe_positions(eidx, offsets):
    top_k, ntok = eidx.shape
    n_e = offsets.shape[0]
    blk = pl.BlockSpec((top_k, ROW_TILE), lambda i: (0, i))
    return pl.pallas_call(
        _moe_positions_kernel,
        grid=(ntok // ROW_TILE,),
        in_specs=[blk, pl.BlockSpec((n_e, 1), lambda i: (0, 0))],
        out_specs=blk,
        out_shape=jax.ShapeDtypeStruct((top_k, ntok), jnp.int32),
        scratch_shapes=[pltpu.VMEM((n_e, 1), F32)],
        compiler_params=_cparams(("arbitrary",), V7X_VMEM_LIMIT),
        name="moe_positions",
    )(eidx, offsets)


def _token_copy(src_hbm, src_tok, buf, slot, dst_tok, sem, n_data_rows):
    src0 = pl.multiple_of(src_tok * ROW_PITCH, 8)
    dst0 = pl.multiple_of(dst_tok * ROW_PITCH, 8)
    return pltpu.make_async_copy(src_hbm.at[pl.ds(src0, n_data_rows), :],
                                 buf.at[slot, pl.ds(dst0, n_data_rows), :], sem.at[slot])


def _gather_rows_start(idx_ref, src_hbm, buf, slot, sem, n_tok, n_data_rows):
    def body(r, carry):
        _token_copy(src_hbm, idx_ref[0, r], buf, slot, r, sem, n_data_rows).start()
        return carry

    lax.fori_loop(0, n_tok, body, 0)


def _gather_rows_wait(src_hbm, buf, slot, sem, n_tok, n_data_rows):
    def body(r, carry):
        _token_copy(src_hbm, 0, buf, slot, r, sem, n_data_rows).wait()
        return carry

    lax.fori_loop(0, n_tok, body, 0)


def _moe_ffn_kernel(te_ref, nv_ref, idx_ref, idx_next_ref, h_hbm, wg_ref, wu_ref, wd_ref, o_ref,
                    xbuf, sem, wg_bf, wu_bf, wd_bf):
    j = pl.program_id(0)
    n_valid = nv_ref[0]
    slot = j % 2
    tm = xbuf.shape[1] // ROW_PITCH
    d = wg_ref.shape[0]
    k = _token_data_rows(d)

    @pl.when(j == 0)
    def _():
        _gather_rows_start(idx_ref, h_hbm, xbuf, 0, sem, tm, k)

    @pl.when(j + 1 < n_valid)
    def _():
        _gather_rows_start(idx_next_ref, h_hbm, xbuf, 1 - slot, sem, tm, k)

    @pl.when(jnp.logical_or(j == 0, te_ref[j] != te_ref[jnp.maximum(j - 1, 0)]))
    def _():
        wg_bf[...] = wg_ref[...].astype(BF16)
        wu_bf[...] = wu_ref[...].astype(BF16)
        wd_bf[...] = wd_ref[...].astype(BF16)

    @pl.when(j < n_valid)
    def _():
        _gather_rows_wait(h_hbm, xbuf, slot, sem, tm, k)
        x = _load_packed(xbuf.at[slot], tm, d).astype(BF16)
        hg = _dot(x, wg_bf[...])
        hu = _dot(x, wu_bf[...])
        _store_packed(o_ref, _dot((_silu(hg) * hu).astype(BF16), wd_bf[...]))

    @pl.when(j >= n_valid)
    def _():
        o_ref[...] = jnp.zeros(o_ref.shape, jnp.uint32)


def moe_ffn_sorted(h, src_tok, tile_expert, n_valid, wg, wu, wd, layer, *, tm):
    n_tiles = src_tok.shape[0]
    d, ff = wg.shape[2], wg.shape[3]
    smem_rows = functools.partial(pl.BlockSpec, (None, 1, tm), memory_space=pltpu.SMEM)
    grid_spec = pltpu.PrefetchScalarGridSpec(
        num_scalar_prefetch=2,
        grid=(n_tiles,),
        in_specs=[
            smem_rows(lambda j, te, nv: (j, 0, 0)),
            smem_rows(lambda j, te, nv: (jnp.minimum(j + 1, n_tiles - 1), 0, 0)),
            pl.BlockSpec(memory_space=pl.ANY),
            pl.BlockSpec((None, None, d, ff), lambda j, te, nv: (layer, te[j], 0, 0)),
            pl.BlockSpec((None, None, d, ff), lambda j, te, nv: (layer, te[j], 0, 0)),
            pl.BlockSpec((None, None, ff, d), lambda j, te, nv: (layer, te[j], 0, 0)),
        ],
        out_specs=pl.BlockSpec((tm * ROW_PITCH, LANES), lambda j, te, nv: (j, 0)),
        scratch_shapes=[pltpu.VMEM((2, tm * ROW_PITCH, LANES), jnp.uint32), pltpu.SemaphoreType.DMA((2,)),
                        pltpu.VMEM((d, ff), BF16), pltpu.VMEM((d, ff), BF16), pltpu.VMEM((ff, d), BF16)],
    )
    return pl.pallas_call(
        _moe_ffn_kernel,
        grid_spec=grid_spec,
        out_shape=jax.ShapeDtypeStruct((n_tiles * tm * ROW_PITCH, LANES), jnp.uint32),
        compiler_params=_cparams(("arbitrary",), V7X_VMEM_LIMIT),
        name="moe_ffn_sorted",
    )(tile_expert, n_valid, src_tok, src_tok, h, wg, wu, wd)


def _moe_combine_kernel(pos_ref, pos_next_ref, w_ref, ys_hbm, o_ref, gbuf, sem):
    i = pl.program_id(0)
    n = pl.num_programs(0)
    slot = i % 2
    top_k = gbuf.shape[0] // 2
    tc = gbuf.shape[1] // ROW_PITCH
    n_data_rows = _token_data_rows(o_ref.shape[1])

    def copy(p_ref, s, k, r):
        src0 = pl.multiple_of(p_ref[k, r] * ROW_PITCH, 8)
        dst0 = pl.multiple_of(r * ROW_PITCH, 8)
        return pltpu.make_async_copy(ys_hbm.at[pl.ds(src0, n_data_rows), :],
                                     gbuf.at[s * top_k + k, pl.ds(dst0, n_data_rows), :], sem.at[s])

    def start(p_ref, s):
        for k in range(top_k):
            def body(r2, carry, k=k):
                for prio in range(2):
                    copy(p_ref, s, k, 2 * r2 + prio).start(priority=prio)
                return carry

            lax.fori_loop(0, tc // 2, body, 0)

    @pl.when(i == 0)
    def _():
        start(pos_ref, 0)

    @pl.when(i + 1 < n)
    def _():
        start(pos_next_ref, 1 - slot)

    def wait_body(r, carry):
        copy(pos_ref, slot, 0, 0).wait()
        return carry

    lax.fori_loop(0, top_k * tc, wait_body, 0)
    w = w_ref[...]
    for c in range(n_data_rows):
        acc_lo = acc_hi = None
        for k in range(top_k):
            lo, hi = _unpack_bf16_pair(gbuf[slot * top_k + k, pl.ds(c, tc, stride=ROW_PITCH), :])
            wk = w[:, k:k + 1]
            acc_lo = wk * lo if acc_lo is None else acc_lo + wk * lo
            acc_hi = wk * hi if acc_hi is None else acc_hi + wk * hi
        o_ref[:, c * LANES:(c + 1) * LANES] = acc_lo
        o_ref[:, (n_data_rows + c) * LANES:(n_data_rows + c + 1) * LANES] = acc_hi


def moe_combine(ys, pos_tiles, w_tok, d, *, tc):
    n_tiles, top_k, _ = pos_tiles.shape
    smem_pos = functools.partial(pl.BlockSpec, (None, top_k, tc), memory_space=pltpu.SMEM)
    return pl.pallas_call(
        _moe_combine_kernel,
        grid=(n_tiles,),
        in_specs=[smem_pos(lambda i: (i, 0, 0)),
                  smem_pos(lambda i: (jnp.minimum(i + 1, n_tiles - 1), 0, 0)),
                  pl.BlockSpec((tc, top_k), lambda i: (i, 0)),
                  pl.BlockSpec(memory_space=pl.ANY)],
        out_specs=pl.BlockSpec((tc, d), lambda i: (i, 0)),
        out_shape=jax.ShapeDtypeStruct((n_tiles * tc, d), F32),
        scratch_shapes=[pltpu.VMEM((2 * top_k, tc * ROW_PITCH, LANES), jnp.uint32),
                        pltpu.SemaphoreType.DMA((2,))],
        compiler_params=_cparams(("arbitrary",), V7X_VMEM_LIMIT),
        name="moe_combine",
    )(pos_tiles, pos_tiles, w_tok, ys)


_HI = lax.Precision.HIGHEST


def _cmul(ar, ai, br, bi):
    return ar * br - ai * bi, ar * bi + ai * br


def s5_operators(a_re, a_im, log_dt, b_re, b_im, c_re, c_im):
    t = S5_T
    dt = jnp.exp(log_dt)[..., None]
    adt_re, adt_im = a_re * dt, a_im * dt
    tau = jnp.arange(t + 1, dtype=F32)[None, None, :, None]
    mag = jnp.exp(adt_re[:, :, None, :] * tau)
    ang = adt_im[:, :, None, :] * tau
    pw_re, pw_im = mag * jnp.cos(ang), mag * jnp.sin(ang)
    ab_re, ab_im = pw_re[:, :, 1], pw_im[:, :, 1]
    den = a_re * a_re + a_im * a_im
    nr, ni = ab_re - 1.0, ab_im
    f_re = (nr * a_re + ni * a_im) / den
    f_im = (ni * a_re - nr * a_im) / den
    bb_re, bb_im = _cmul(f_re[..., None], f_im[..., None], b_re, b_im)
    m1_re, m1_im = _cmul(pw_re[..., None], pw_im[..., None], bb_re[:, :, None], bb_im[:, :, None])
    kk = (jnp.einsum('dgpn,dgtnq->dgtpq', c_re, m1_re[:, :, :t], precision=_HI)
          - jnp.einsum('dgpn,dgtnq->dgtpq', c_im, m1_im[:, :, :t], precision=_HI))
    ti = jnp.arange(t)
    lag = ti[:, None] - ti[None, :]
    kg = kk[:, :, jnp.clip(lag, 0, t - 1)]
    kg = jnp.where((lag >= 0)[None, None, :, :, None, None], kg, 0.0)
    nd, g = a_re.shape[0], a_re.shape[1]
    p = b_re.shape[-1]
    n = a_re.shape[-1]
    inc = jnp.stack([m1_re[:, :, t - 1 - ti], m1_im[:, :, t - 1 - ti]], axis=3)
    w_re, w_im = _cmul(c_re[:, :, None], c_im[:, :, None],
                       pw_re[:, :, 1:, None, :], pw_im[:, :, 1:, None, :])
    mst = jnp.stack([w_re, -w_im], axis=2)

    def reverse_backward(v, axes):
        return jnp.stack([v[0], jnp.flip(v[1], axes)], axis=0)

    kg = reverse_backward(kg, (1, 2))
    inc = reverse_backward(inc, (1,))
    mst = reverse_backward(mst, (2,))
    gpb = S5_LANES // p
    gb = g // gpb
    fw = t * S5_LANES
    assert t * p == S5_LANES and 2 * n == S5_LANES
    toep_c = kg.reshape(nd, gb, gpb, t, t, p, p).transpose(0, 1, 4, 2, 6, 3, 5)
    minc_c = inc.reshape(nd, gb, gpb, t, 2, n, p).transpose(0, 1, 3, 2, 6, 4, 5)
    mstate_c = mst.reshape(nd, gb, gpb, 2, t, p, n).transpose(0, 1, 3, 2, 6, 4, 5)
    a_t_re = pw_re[:, :, t].reshape(nd, 1, g * n)
    a_t_im = pw_im[:, :, t].reshape(nd, 1, g * n)
    return (toep_c.reshape(nd, gb, fw, S5_LANES).astype(BF16), minc_c.reshape(nd, gb, fw, S5_LANES).astype(BF16),
            mstate_c.reshape(nd, gb, fw, S5_LANES).astype(BF16), a_t_re, a_t_im)


def rotary_tables(n_pos, half):
    freq = ROPE_BASE ** (-jnp.arange(half, dtype=F32) / half)
    ang = jnp.arange(n_pos, dtype=F32)[:, None] * freq[None, :]
    return jnp.cos(ang), jnp.sin(ang)


def kernel(x, c, ctx, c_ctx, emb_ln_g, emb_ln_b, ada_w, ada_b, w_in, conv_w, conv_b, conv_ln_g, conv_ln_b, conv_proj, ret_decay_logit, ret_proj, s5_a_re, s5_a_im, s5_log_dt, s5_b_re, s5_b_im, s5_c_re, s5_c_im, s5_d, s5_w_glu, s5_proj, w_out, ln1_g, ln1_b, ln2_g, ln2_b, router_w, router_bias, exp_w_gate, exp_w_up, exp_w_down, sh_w_gate, sh_w_up, sh_w_down):
    n_batch, seq, d = x.shape
    lc = ctx.shape[1]
    depth = w_in.shape[0]
    conv_c = conv_w.shape[2]
    ret_w = ret_proj.shape[1]
    s5_c = s5_d.shape[1]
    s5_g = s5_c // S5_P
    n_exp, _, exp_ff = exp_w_gate.shape[1:]
    sh_ff = sh_w_gate.shape[2]
    rows = seq // GRID_W
    assert lc == ROW_TILE and seq % ROW_TILE == 0 and n_batch + 1 <= MOD_ROWS
    assert RET_HEADS * RET_DK == ret_w and S5_T * S5_LANES == 2 * (S5_LANES // S5_P) * S5_N
    assert s5_c % S5_LANES == 0 and lc % S5_T == 0 and rows % S5_T == 0
    tiles_per_batch = seq // ROW_TILE
    n_lat_tiles = n_batch * tiles_per_batch
    n_tiles = n_lat_tiles + n_batch
    n_lat = n_batch * seq
    alpha = (2.0 * depth) ** 0.25
    ntok = n_tiles * ROW_TILE
    tm_mm = _largest_row_tile(ntok, 2)
    tm_ffn = _largest_row_tile(ntok, 2) if ntok % 544 else 544
    col_a, col_q = 0, 2 * conv_c
    col_g = col_q + 3 * ret_w
    col_u = col_g + ret_w
    col_s = col_u + s5_c

    group_fn = functools.partial(_group_of_tile, n_latent_tiles=n_lat_tiles,
                                 tiles_per_batch=tiles_per_batch, n_batch=n_batch)

    tokens = jnp.concatenate([x.reshape(n_lat, d), ctx.reshape(n_batch * lc, d)], axis=0)
    cvec = jnp.concatenate([c, c_ctx[None, :], jnp.zeros((MOD_ROWS - n_batch - 1, d), F32)], axis=0)
    mods = [ada_modulation(cvec, ada_w, ada_b[i][None, :], i) for i in range(depth)]
    cos_tab, sin_tab = rotary_tables(lc + seq, RET_DK // 2)
    log_g = jax.nn.log_sigmoid(ret_decay_logit.astype(F32))

    xl, h = resid_ln_mod(tokens, [], None, 0, emb_ln_g[None, :], emb_ln_b[None, :], mods[0], 0, 1,
                         alpha=1.0, group_fn=group_fn, n_tiles=n_tiles)

    for i in range(depth):
        last = i == depth - 1
        mod = mods[i]
        z = matmul_stacked_w(h, w_in, i, BF16, tm_mm, 512)
        act_a = jnp.concatenate([
            conformer_conv_act(z, conv_w[i], conv_b[i][None, :], conv_ln_g[i][None, :], conv_ln_b[i][None, :],
                               row_tile0=0, n_tiles=n_lat_tiles, seg=GRID_W),
            conformer_conv_act(z, conv_w[i], conv_b[i][None, :], conv_ln_g[i][None, :], conv_ln_b[i][None, :],
                               row_tile0=n_lat_tiles, n_tiles=n_batch, seg=lc)], axis=0)
        o_f, o_b = retention_scan(z, log_g[i], cos_tab, sin_tab, n_batch=n_batch,
                                  tiles_per_batch=tiles_per_batch, q_col0=col_q // RET_DK)
        act_b = retention_post(o_f, o_b, z, g_col=col_g // ret_w)
        u = z[:, col_u:col_u + s5_c].astype(BF16)
        gb = s5_c // S5_LANES
        fw = S5_T * S5_LANES
        u_lat = u[:n_lat].reshape(n_batch, rows, GRID_W, gb, S5_LANES).transpose(0, 3, 2, 1, 4)
        u_ctx = u[n_lat:].reshape(n_batch, lc, gb, S5_LANES).transpose(0, 2, 1, 3)
        u_fold = jnp.concatenate([u_ctx.reshape(n_batch, gb, lc // S5_T, fw),
                                  u_lat.reshape(n_batch, gb, seq // S5_T, fw)], axis=2)
        ops = s5_operators(s5_a_re[i], s5_a_im[i], s5_log_dt[i], s5_b_re[i], s5_b_im[i], s5_c_re[i], s5_c_im[i])
        y_fold = s5_chunked(u_fold, *ops, n_ctx_chunk=lc // S5_T)
        y_ctx = y_fold[:, :, :lc // S5_T].reshape(n_batch, gb, lc, S5_LANES).transpose(0, 2, 1, 3)
        y_lat = y_fold[:, :, lc // S5_T:].reshape(n_batch, gb, GRID_W, rows, S5_LANES).transpose(0, 3, 2, 1, 4)
        y_s5 = jnp.concatenate([y_lat.reshape(n_lat, s5_c), y_ctx.reshape(n_batch * lc, s5_c)], axis=0)
        act_c = s5_post(y_s5, z, s5_d[i][None, :], s5_w_glu, i, u_col=col_u // s5_c)
        merged = merge_branches(act_a, act_b, act_c, z, conv_proj, ret_proj, s5_proj, i,
                                s_col0=col_s, tm=tm_mm, tn=512)
        y_mix = matmul_stacked_w(merged, w_out, i, F32, tm_mm, 512)
        xl, h2, h2_f32, eidx, wk = resid_ln_mod(
            xl, [y_mix], mod, 2, ln1_g[i][None, :], ln1_b[i][None, :], mod, 3, 4,
            alpha=alpha, group_fn=group_fn, n_tiles=n_tiles,
            router=(router_w[i].T, router_bias[i][:, None]))
        counts = jnp.sum((eidx[None, :, :] == jnp.arange(n_exp, dtype=jnp.int32)[:, None, None]).astype(jnp.int32),
                         axis=(1, 2))
        padded = ((counts + MOE_ROW_TILE - 1) // MOE_ROW_TILE) * MOE_ROW_TILE
        ends = jnp.cumsum(padded)
        n_sorted_tiles = (ntok * TOP_K) // MOE_ROW_TILE + n_exp
        tile_start = jnp.arange(n_sorted_tiles, dtype=jnp.int32) * MOE_ROW_TILE
        tile_expert = jnp.minimum(jnp.sum((ends[None, :] <= tile_start[:, None]).astype(jnp.int32), axis=1),
                                  n_exp - 1)
        n_valid = (ends[-1:] // MOE_ROW_TILE).astype(jnp.int32)
        pos = moe_positions(eidx, (ends - padded).astype(F32)[:, None])
        src_tok = jnp.zeros((n_sorted_tiles * MOE_ROW_TILE,), jnp.int32).at[pos.reshape(-1)].set(
            jnp.tile(jnp.arange(ntok, dtype=jnp.int32), TOP_K))
        y_sorted = moe_ffn_sorted(h2_f32, src_tok.reshape(n_sorted_tiles, 1, MOE_ROW_TILE), tile_expert, n_valid,
                                  exp_w_gate, exp_w_up, exp_w_down, i, tm=MOE_ROW_TILE)
        pos_tiles = pos.reshape(TOP_K, ntok // MOE_COMBINE_TILE, MOE_COMBINE_TILE).transpose(1, 0, 2)
        y_routed = moe_combine(y_sorted, pos_tiles, wk.T, d, tc=MOE_COMBINE_TILE)
        y_shared = ffn_blocks(h2, sh_w_gate, sh_w_up, sh_w_down, i, exp_ff, tm=tm_ffn)
        if last:
            (xl,) = resid_ln_mod(xl, [y_routed, y_shared], mod, 5, ln2_g[i][None, :], ln2_b[i][None, :],
                                 None, 0, 0, alpha=alpha, group_fn=group_fn, n_tiles=n_lat_tiles)
        else:
            xl, h = resid_ln_mod(xl, [y_routed, y_shared], mod, 5, ln2_g[i][None, :], ln2_b[i][None, :],
                                 mods[i + 1], 0, 1, alpha=alpha, group_fn=group_fn, n_tiles=n_tiles)
    return xl.reshape(n_batch, seq, d)
```

```python
import functools
import math

import jax
import jax.numpy as jnp
from jax import lax
from jax.experimental import pallas as pl
from jax.experimental.pallas import tpu as pltpu

F32 = jnp.float32
BF16 = jnp.bfloat16

GRID_W = 64
RET_HEADS = 8
RET_DK = 256
S5_P = 16
S5_N = 64
ROPE_BASE = 10000.0
N_GROUPS = 8
TOPK_GROUPS = 4
TOP_K = 8
ROUTED_SCALE = 2.5
LN_EPS = 1e-5
HEAD_NORM_EPS = 1e-5
NEG_BIG = -1e30
N_BRANCH = 3

ROW_TILE = 256
S5_LANES = 128
S5_T = 8
MOD_ROWS = 8
MOE_ROW_TILE = 256
MOE_COMBINE_TILE = 64
DMA_ISSUE_UNROLL = 8
LANES = 128
ROW_PITCH = 24
V7X_VMEM_LIMIT = 56 * 1024 * 1024


def _largest_row_tile(ntok, max_tiles):
    n = ntok // ROW_TILE
    k = max(t for t in range(1, max_tiles + 1) if n % t == 0)
    return k * ROW_TILE


def _cparams(sem, vmem=None):
    return pltpu.CompilerParams(dimension_semantics=sem, vmem_limit_bytes=vmem)


def _split_bf16(v):
    hi = v.astype(BF16)
    lo = (v - hi.astype(F32)).astype(BF16)
    return hi, lo


def _dot(a, b):
    return jnp.dot(a, b, preferred_element_type=F32)


def _dot_nt(a, b):
    return lax.dot_general(a, b, (((1,), (1,)), ((), ())), preferred_element_type=F32)


def _dot_tn(a, b):
    return lax.dot_general(a, b, (((0,), (0,)), ((), ())), preferred_element_type=F32)


def _sigmoid(v):
    return 1.0 / (1.0 + jnp.exp(-v))


def _silu(v):
    return v * _sigmoid(v)


_HIGH_HALF = 0xFFFF0000


def _pack_bf16_pair(lo, hi):
    lo_bits = pltpu.bitcast(lo.astype(BF16).astype(F32), jnp.uint32) >> 16
    hi_bits = pltpu.bitcast(hi.astype(BF16).astype(F32), jnp.uint32) & jnp.uint32(_HIGH_HALF)
    return lo_bits | hi_bits


def _unpack_bf16_pair(words):
    lo = pltpu.bitcast(words << 16, F32)
    hi = pltpu.bitcast(words & jnp.uint32(_HIGH_HALF), F32)
    return lo, hi


def _token_data_rows(d):
    rows = d // (2 * LANES)
    assert rows * 2 * LANES == d and rows <= ROW_PITCH
    return rows


def _store_packed(ref, v):
    n, d = v.shape
    k = _token_data_rows(d)
    for c in range(k):
        lo = v[:, c * LANES:(c + 1) * LANES]
        hi = v[:, (k + c) * LANES:(k + c + 1) * LANES]
        ref[pl.ds(c, n, stride=ROW_PITCH), :] = _pack_bf16_pair(lo, hi)
    for c in range(k, ROW_PITCH):
        ref[pl.ds(c, n, stride=ROW_PITCH), :] = jnp.zeros((n, LANES), jnp.uint32)


def _load_packed(ref, n, d):
    k = _token_data_rows(d)
    parts = [_unpack_bf16_pair(ref[pl.ds(c, n, stride=ROW_PITCH), :]) for c in range(k)]
    return jnp.concatenate([p[0] for p in parts] + [p[1] for p in parts], axis=1)


def _layer_norm_rows(v, g, b, eps):
    mu = jnp.mean(v, axis=-1, keepdims=True)
    vc = v - mu
    var = jnp.mean(vc * vc, axis=-1, keepdims=True)
    return vc * lax.rsqrt(var + eps) * g + b


def _ada_kernel(c_ref, w_ref, b_ref, o_ref):
    c = _silu(c_ref[...])
    ch, cl = _split_bf16(c)
    wh, wl = _split_bf16(w_ref[...])
    o_ref[...] = _dot(ch, wh) + _dot(ch, wl) + _dot(cl, wh) + b_ref[...]


def ada_modulation(cvec, ada_w, ada_b_l, layer):
    _, d, n = ada_w.shape
    tn = 512
    return pl.pallas_call(
        _ada_kernel,
        grid=(n // tn,),
        in_specs=[
            pl.BlockSpec((MOD_ROWS, d), lambda j: (0, 0)),
            pl.BlockSpec((None, d, tn), lambda j: (layer, 0, j)),
            pl.BlockSpec((1, tn), lambda j: (0, j)),
        ],
        out_specs=pl.BlockSpec((MOD_ROWS, tn), lambda j: (0, j)),
        out_shape=jax.ShapeDtypeStruct((MOD_ROWS, n), F32),
        compiler_params=_cparams(("arbitrary",), V7X_VMEM_LIMIT),
        name="ada_modulation",
    )(cvec, ada_w, ada_b_l)


def _group_of_tile(i, n_latent_tiles, tiles_per_batch, n_batch):
    return jnp.where(i < n_latent_tiles, i // tiles_per_batch, n_batch)


def _route(h, rwt_ref, rb_ref):
    hh, hl = _split_bf16(h)
    wh, wl = _split_bf16(rwt_ref[...])
    logits = _dot_nt(wh, hh) + _dot_nt(wh, hl) + _dot_nt(wl, hh)
    scores = _sigmoid(logits)
    sel = scores + rb_ref[...]
    n_e, tm = sel.shape
    per = n_e // N_GROUPS
    shape3 = (N_GROUPS, per, tm)
    sel3 = sel.reshape(shape3)
    io_e = lax.broadcasted_iota(jnp.int32, shape3, 1)
    io_g = lax.broadcasted_iota(jnp.int32, shape3, 0)
    m1 = jnp.max(sel3, axis=1, keepdims=True)
    first = jnp.min(jnp.where(sel3 == m1, io_e, per), axis=1, keepdims=True)
    m2 = jnp.max(jnp.where(io_e == first, -jnp.inf, sel3), axis=1, keepdims=True)
    work = m1 + m2
    iog1 = lax.broadcasted_iota(jnp.int32, work.shape, 0)
    gsel = jnp.zeros(work.shape, F32)
    for _ in range(TOPK_GROUPS):
        m = jnp.max(work, axis=0, keepdims=True)
        fi = jnp.min(jnp.where(work == m, iog1, N_GROUPS), axis=0, keepdims=True)
        hit = iog1 == fi
        gsel = jnp.where(hit, 1.0, gsel)
        work = jnp.where(hit, -jnp.inf, work)
    work = jnp.where(jnp.broadcast_to(gsel, shape3) > 0.0, sel3, NEG_BIG)
    flat = io_g * per + io_e
    scores3 = scores.reshape(shape3)
    picked, picked_score = [], []
    for _ in range(TOP_K):
        m = jnp.max(jnp.max(work, axis=1, keepdims=True), axis=0, keepdims=True)
        cand = jnp.where(work == m, flat, n_e)
        fi = jnp.min(jnp.min(cand, axis=1, keepdims=True), axis=0, keepdims=True)
        hit = flat == fi
        sk = jnp.sum(jnp.sum(jnp.where(hit, scores3, 0.0), axis=1, keepdims=True), axis=0, keepdims=True)
        picked.append(fi.reshape(1, tm))
        picked_score.append(sk.reshape(1, tm))
        work = jnp.where(hit, -jnp.inf, work)
    eidx = jnp.concatenate(picked, axis=0)
    w = jnp.concatenate(picked_score, axis=0)
    return eidx, ROUTED_SCALE * w / jnp.sum(w, axis=0, keepdims=True)


def _resid_ln_mod_kernel(*refs, alpha, n_y, has_mod, has_router, group_fn):
    it = iter(refs)
    x_ref = next(it)
    y_refs = [next(it) for _ in range(n_y)]
    gate_ref = next(it) if n_y else None
    g_ref, b_ref = next(it), next(it)
    shift_ref = scale_ref = rwt_ref = rb_ref = None
    if has_mod:
        shift_ref, scale_ref = next(it), next(it)
    if has_router:
        rwt_ref, rb_ref = next(it), next(it)
    xl_ref = next(it)
    h_ref = next(it) if has_mod else None
    hf_ref, eidx_ref, wk_ref = (next(it), next(it), next(it)) if has_router else (None, None, None)

    grp = group_fn(pl.program_id(0))
    v = x_ref[...]
    if n_y:
        y = y_refs[0][...]
        for r in y_refs[1:]:
            y = y + r[...]
        v = alpha * v + gate_ref[pl.ds(grp, 1), :] * y
    xl = _layer_norm_rows(v, g_ref[...], b_ref[...], LN_EPS)
    xl_ref[...] = xl
    if has_mod:
        h = xl * (1.0 + scale_ref[pl.ds(grp, 1), :]) + shift_ref[pl.ds(grp, 1), :]
        h_ref[...] = h.astype(h_ref.dtype)
        if has_router:
            _store_packed(hf_ref, h)
            eidx_ref[...], wk_ref[...] = _route(h, rwt_ref, rb_ref)


def resid_ln_mod(x, ys, gate_mod, gate_col, ln_g, ln_b, mod, shift_col, scale_col, *,
                 alpha, group_fn, n_tiles, router=None):
    d = x.shape[1]
    n_y = len(ys)
    has_mod = mod is not None
    has_router = router is not None
    row = pl.BlockSpec((ROW_TILE, d), lambda i: (i, 0))
    vec = pl.BlockSpec((1, d), lambda i: (0, 0))
    args, specs = [x], [row]
    for y in ys:
        args.append(y)
        specs.append(row)
    if n_y:
        args.append(gate_mod)
        specs.append(pl.BlockSpec((MOD_ROWS, d), lambda i: (0, gate_col)))
    args += [ln_g, ln_b]
    specs += [vec, vec]
    if has_mod:
        args += [mod, mod]
        specs += [pl.BlockSpec((MOD_ROWS, d), lambda i: (0, shift_col)),
                  pl.BlockSpec((MOD_ROWS, d), lambda i: (0, scale_col))]
    out_shapes = [jax.ShapeDtypeStruct((n_tiles * ROW_TILE, d), F32)]
    out_specs = [row]
    if has_mod:
        out_shapes.append(jax.ShapeDtypeStruct((n_tiles * ROW_TILE, d), BF16))
        out_specs.append(row)
    if has_router:
        rwt, rb = router
        n_e = rwt.shape[0]
        args += [rwt, rb]
        specs += [pl.BlockSpec((n_e, d), lambda i: (0, 0)), pl.BlockSpec((n_e, 1), lambda i: (0, 0))]
        out_shapes += [jax.ShapeDtypeStruct((n_tiles * ROW_TILE * ROW_PITCH, LANES), jnp.uint32),
                       jax.ShapeDtypeStruct((TOP_K, n_tiles * ROW_TILE), jnp.int32),
                       jax.ShapeDtypeStruct((TOP_K, n_tiles * ROW_TILE), F32)]
        out_specs += [pl.BlockSpec((ROW_TILE * ROW_PITCH, LANES), lambda i: (i, 0)),
                      pl.BlockSpec((TOP_K, ROW_TILE), lambda i: (0, i)),
                      pl.BlockSpec((TOP_K, ROW_TILE), lambda i: (0, i))]
    kern = functools.partial(_resid_ln_mod_kernel, alpha=alpha, n_y=n_y, has_mod=has_mod,
                             has_router=has_router, group_fn=group_fn)
    return pl.pallas_call(
        kern, grid=(n_tiles,), in_specs=specs, out_specs=out_specs, out_shape=out_shapes,
        compiler_params=_cparams(("arbitrary",), V7X_VMEM_LIMIT),
        name="resid_ln_mod",
    )(*args)


def _mm_kernel(x_ref, w_ref, o_ref, wbf_ref):
    @pl.when(pl.program_id(1) == 0)
    def _():
        wbf_ref[...] = w_ref[...].astype(BF16)

    o_ref[...] = _dot(x_ref[...], wbf_ref[...]).astype(o_ref.dtype)


def matmul_stacked_w(x, w, layer, out_dtype, tm, tn):
    m, k = x.shape
    n = w.shape[2]
    return pl.pallas_call(
        _mm_kernel,
        grid=(n // tn, m // tm),
        in_specs=[
            pl.BlockSpec((tm, k), lambda j, i: (i, 0)),
            pl.BlockSpec((None, k, tn), lambda j, i: (layer, 0, j)),
        ],
        out_specs=pl.BlockSpec((tm, tn), lambda j, i: (i, j)),
        out_shape=jax.ShapeDtypeStruct((m, n), out_dtype),
        scratch_shapes=[pltpu.VMEM((k, tn), BF16)],
        compiler_params=_cparams(("arbitrary", "arbitrary"), V7X_VMEM_LIMIT),
        name="matmul",
    )(x, w)


def _conv_kernel(a1_ref, a2_ref, w_ref, cb_ref, g_ref, b_ref, o_ref, pad_ref, y_ref, *, seg, n_tap):
    half = n_tap // 2
    front = ((half + 7) // 8) * 8
    nseg = ROW_TILE // seg
    c = a1_ref.shape[1]
    u = a1_ref[...].astype(F32) * _sigmoid(a2_ref[...].astype(F32))
    pad_ref[...] = jnp.zeros(pad_ref.shape, F32)
    for s in range(nseg):
        pad_ref[s, front:front + seg, :] = u[s * seg:(s + 1) * seg, :]
    lanes = 128

    def chunk(ci, carry):
        c0 = pl.multiple_of(ci * lanes, lanes)
        acc = jnp.zeros((nseg, seg, lanes), F32)
        for k in range(n_tap):
            off = front - half + k
            acc = acc + w_ref[k:k + 1, pl.ds(c0, lanes)] * pad_ref[:, off:off + seg, pl.ds(c0, lanes)]
        y_ref[:, pl.ds(c0, lanes)] = acc.reshape(ROW_TILE, lanes)
        return carry

    lax.fori_loop(0, c // lanes, chunk, 0)
    y = y_ref[...] + cb_ref[...]
    o_ref[...] = _silu(_layer_norm_rows(y, g_ref[...], b_ref[...], LN_EPS)).astype(o_ref.dtype)


def conformer_conv_act(z, conv_w_l, conv_b_l, ln_g_l, ln_b_l, *, row_tile0, n_tiles, seg):
    n_tap, c = conv_w_l.shape
    half = n_tap // 2
    front = ((half + 7) // 8) * 8
    nseg = ROW_TILE // seg
    vec = pl.BlockSpec((1, c), lambda i: (0, 0))
    kern = functools.partial(_conv_kernel, seg=seg, n_tap=n_tap)
    return pl.pallas_call(
        kern,
        grid=(n_tiles,),
        in_specs=[
            pl.BlockSpec((ROW_TILE, c), lambda i: (row_tile0 + i, 0)),
            pl.BlockSpec((ROW_TILE, c), lambda i: (row_tile0 + i, 1)),
            pl.BlockSpec((n_tap, c), lambda i: (0, 0)),
            vec, vec, vec,
        ],
        out_specs=pl.BlockSpec((ROW_TILE, c), lambda i: (i, 0)),
        out_shape=jax.ShapeDtypeStruct((n_tiles * ROW_TILE, c), BF16),
        scratch_shapes=[pltpu.VMEM((nseg, seg + 2 * front, c), F32), pltpu.VMEM((ROW_TILE, c), F32)],
        compiler_params=_cparams(("arbitrary",), V7X_VMEM_LIMIT),
        name="conformer_conv",
    )(z, z, conv_w_l, conv_b_l, ln_g_l, ln_b_l)


def _rotary(t, cos, sin):
    half = t.shape[1] // 2
    t1, t2 = t[:, :half], t[:, half:]
    return jnp.concatenate([t1 * cos - t2 * sin, t1 * sin + t2 * cos], axis=1)


def _retention_kernel(logg_ref, qf_ref, kf_ref, vf_ref, cf_ref, sf_ref,
                      qb_ref, kb_ref, vb_ref, cb_ref, sb_ref,
                      of_ref, ob_ref, state_ref):
    h = pl.program_id(1)
    n = pl.program_id(2)
    c = ROW_TILE
    k_scale = RET_DK ** -0.5

    @pl.when(n == 0)
    def _():
        state_ref[...] = jnp.zeros(state_ref.shape, F32)

    row = lax.broadcasted_iota(jnp.int32, (c, c), 0)
    col = lax.broadcasted_iota(jnp.int32, (c, c), 1)
    ridx = lax.broadcasted_iota(jnp.int32, (c, 1), 0).astype(F32)

    def one_direction(d, q_ref, k_ref, v_ref, cos_ref, sin_ref, o_ref):
        lg = logg_ref[d, h]
        cos, sin = cos_ref[...], sin_ref[...]
        q = _rotary(q_ref[...].astype(F32), cos, sin)
        k = _rotary(k_ref[...].astype(F32), cos, sin) * k_scale
        v = v_ref[...].astype(BF16)
        dist = (row - col) if d == 0 else (col - row)
        decay = jnp.where(dist >= 0, jnp.exp(lg * jnp.maximum(dist, 0).astype(F32)), 0.0)
        qb = q.astype(BF16)
        scores = _dot_nt(qb, k.astype(BF16)) * decay
        inner = _dot(scores.astype(BF16), v)
        to_prev = (ridx + 1.0) if d == 0 else (c - ridx)
        to_end = (c - 1.0 - ridx) if d == 0 else ridx
        s_prev = state_ref[d]
        cross = _dot(qb, s_prev.astype(BF16)) * jnp.exp(lg * to_prev)
        o_ref[...] = inner + cross
        kw = (k * jnp.exp(lg * to_end)).astype(BF16)
        state_ref[d] = jnp.exp(lg * jnp.full((1, 1), float(c), F32)) * s_prev + _dot_tn(kw, v)

    one_direction(0, qf_ref, kf_ref, vf_ref, cf_ref, sf_ref, of_ref)
    one_direction(1, qb_ref, kb_ref, vb_ref, cb_ref, sb_ref, ob_ref)


def retention_scan(z, log_g, cos_tab, sin_tab, *, n_batch, tiles_per_batch, q_col0):
    ntok = z.shape[0]
    n_lat = n_batch * tiles_per_batch
    n_steps = tiles_per_batch + 1
    dk = RET_DK
    hh = RET_HEADS

    def row_f(b, n):
        return jnp.where(n == 0, n_lat + b, b * tiles_per_batch + n - 1)

    def row_b(b, n):
        return jnp.where(n == 0, n_lat + b, b * tiles_per_batch + tiles_per_batch - n)

    def pos_f(n):
        return n

    def pos_b(n):
        return jnp.where(n == 0, 0, tiles_per_batch + 1 - n)

    def zspec(rowfn, sec):
        return pl.BlockSpec((ROW_TILE, dk), lambda b, h, n: (rowfn(b, n), q_col0 + sec * hh + h))

    def tspec(posfn):
        return pl.BlockSpec((ROW_TILE, dk // 2), lambda b, h, n: (posfn(n), 0))

    def ospec(rowfn):
        return pl.BlockSpec((ROW_TILE, dk), lambda b, h, n: (rowfn(b, n), h))

    smem = pl.BlockSpec(memory_space=pltpu.SMEM)
    return pl.pallas_call(
        _retention_kernel,
        grid=(n_batch, hh, n_steps),
        in_specs=[smem,
                  zspec(row_f, 0), zspec(row_f, 1), zspec(row_f, 2), tspec(pos_f), tspec(pos_f),
                  zspec(row_b, 0), zspec(row_b, 1), zspec(row_b, 2), tspec(pos_b), tspec(pos_b)],
        out_specs=[ospec(row_f), ospec(row_b)],
        out_shape=[jax.ShapeDtypeStruct((ntok, hh * dk), F32)] * 2,
        scratch_shapes=[pltpu.VMEM((2, dk, dk), F32)],
        compiler_params=_cparams(("arbitrary", "arbitrary", "arbitrary"), V7X_VMEM_LIMIT),
        name="retention_scan",
    )(log_g, z, z, z, cos_tab, sin_tab, z, z, z, cos_tab, sin_tab)


def _ret_post_kernel(of_ref, ob_ref, g_ref, o_ref):
    dk = RET_DK
    for h in range(RET_HEADS):
        sl = slice(h * dk, (h + 1) * dk)
        o = of_ref[:, sl] + ob_ref[:, sl]
        mu = jnp.mean(o, axis=-1, keepdims=True)
        oc = o - mu
        var = jnp.mean(oc * oc, axis=-1, keepdims=True)
        on = oc * lax.rsqrt(var + HEAD_NORM_EPS)
        o_ref[:, sl] = (on * _silu(g_ref[:, sl].astype(F32))).astype(o_ref.dtype)


def retention_post(o_f, o_b, z, *, g_col):
    ntok, w = o_f.shape
    row = pl.BlockSpec((ROW_TILE, w), lambda i: (i, 0))
    return pl.pallas_call(
        _ret_post_kernel,
        grid=(ntok // ROW_TILE,),
        in_specs=[row, row, pl.BlockSpec((ROW_TILE, w), lambda i: (i, g_col))],
        out_specs=row,
        out_shape=jax.ShapeDtypeStruct((ntok, w), BF16),
        compiler_params=_cparams(("arbitrary",), V7X_VMEM_LIMIT),
        name="retention_post",
    )(o_f, o_b, z)


def _expand_block_diag(compact, w, rows_per_group):
    gpb = S5_LANES // S5_P
    n_rows, k = compact.shape
    n_cols = k * gpb
    lw, lg, lr = w.bit_length() - 1, gpb.bit_length() - 1, rows_per_group.bit_length() - 1
    i = lax.broadcasted_iota(jnp.int32, (k, n_cols), 0)
    c = lax.broadcasted_iota(jnp.int32, (k, n_cols), 1)
    src = ((c >> (lw + lg)) << lw) + (c & (w - 1))
    rep = jnp.where(i == src, 1.0, 0.0).astype(BF16)
    full = _dot(compact, rep)
    r = lax.broadcasted_iota(jnp.int32, (n_rows, n_cols), 0)
    c2 = lax.broadcasted_iota(jnp.int32, (n_rows, n_cols), 1)
    keep = ((r >> lr) & (gpb - 1)) == ((c2 >> lw) & (gpb - 1))
    return jnp.where(keep, full, 0.0).astype(BF16)


def _s5_local_kernel(u_ref, kc_ref, ic_ref, yl_ref, xre_ref, xim_ref, toep_ref, minc_ref):
    @pl.when(pl.program_id(2) == 0)
    def _():
        toep_ref[...] = _expand_block_diag(kc_ref[...], S5_P, S5_P)
        minc_ref[...] = _expand_block_diag(ic_ref[...], S5_N, S5_P)

    u = u_ref[...]
    yl_ref[...] = _dot(u, toep_ref[...])
    xi = _dot(u, minc_ref[...])
    half = xi.shape[1] // 2
    xre_ref[...] = xi[:, :half]
    xim_ref[...] = xi[:, half:]


def _s5_scan_kernel(xre_ref, xim_ref, are_ref, aim_ref, ore_ref, oim_ref, *, n_ctx_chunk):
    n_chunk = xre_ref.shape[0]
    backward = pl.program_id(1) == 1
    ar, ai = are_ref[...], aim_ref[...]

    def step(i, carry):
        sr, si = carry
        rev = jnp.where(i < n_ctx_chunk, n_ctx_chunk - 1 - i, n_chunk + n_ctx_chunk - 1 - i)
        c = jnp.where(backward, rev, i)
        ore_ref[pl.ds(c, 1), :] = sr
        oim_ref[pl.ds(c, 1), :] = si
        nr = ar * sr - ai * si + xre_ref[pl.ds(c, 1), :]
        ni = ar * si + ai * sr + xim_ref[pl.ds(c, 1), :]
        return nr, ni

    zero = jnp.zeros(ar.shape, F32)
    lax.fori_loop(0, n_chunk, step, (zero, zero))


def _s5_state_kernel(xre_ref, xim_ref, mc_ref, yl_ref, y_ref, mst_ref):
    @pl.when(pl.program_id(1) == 0)
    def _():
        for d in range(2):
            mst_ref[d] = _expand_block_diag(mc_ref[d], S5_P, S5_N)

    y = yl_ref[0] + yl_ref[1]
    for d in range(2):
        x0 = jnp.concatenate([xre_ref[d], xim_ref[d]], axis=1).astype(BF16)
        y = y + _dot(x0, mst_ref[d])
    y_ref[...] = y


def s5_chunked(u_fold, toep, minc, mstate, a_re, a_im, *, n_ctx_chunk):
    nb, gb, nch, fw = u_fold.shape
    sw = fw // 2
    op = pl.BlockSpec((None, None, fw, S5_LANES), lambda g, d, b: (d, g, 0, 0))
    yl, xre, xim = pl.pallas_call(
        _s5_local_kernel,
        grid=(gb, 2, nb),
        in_specs=[pl.BlockSpec((None, None, nch, fw), lambda g, d, b: (b, g, 0, 0)), op, op],
        out_specs=[pl.BlockSpec((None, None, None, nch, fw), lambda g, d, b: (b, d, g, 0, 0)),
                   pl.BlockSpec((None, None, nch, sw), lambda g, d, b: (b, d, 0, g)),
                   pl.BlockSpec((None, None, nch, sw), lambda g, d, b: (b, d, 0, g))],
        out_shape=[jax.ShapeDtypeStruct((nb, 2, gb, nch, fw), F32),
                   jax.ShapeDtypeStruct((nb, 2, nch, gb * sw), F32),
                   jax.ShapeDtypeStruct((nb, 2, nch, gb * sw), F32)],
        scratch_shapes=[pltpu.VMEM((fw, fw), BF16), pltpu.VMEM((fw, fw), BF16)],
        compiler_params=_cparams(("arbitrary", "arbitrary", "arbitrary"), V7X_VMEM_LIMIT),
        name="s5_local",
    )(u_fold, toep, minc)
    scan_w = 2 * sw
    full = pl.BlockSpec((None, None, nch, scan_w), lambda b, d, j: (b, d, 0, j))
    avec = pl.BlockSpec((None, 1, scan_w), lambda b, d, j: (d, 0, j))
    x0re, x0im = pl.pallas_call(
        functools.partial(_s5_scan_kernel, n_ctx_chunk=n_ctx_chunk),
        grid=(nb, 2, gb * sw // scan_w),
        in_specs=[full, full, avec, avec],
        out_specs=[full, full],
        out_shape=[jax.ShapeDtypeStruct((nb, 2, nch, gb * sw), F32)] * 2,
        compiler_params=_cparams(("arbitrary", "arbitrary", "arbitrary"), V7X_VMEM_LIMIT),
        name="s5_scan",
    )(xre, xim, a_re, a_im)
    xcol = pl.BlockSpec((None, 2, nch, sw), lambda g, b: (b, 0, 0, g))
    return pl.pallas_call(
        _s5_state_kernel,
        grid=(gb, nb),
        in_specs=[xcol, xcol,
                  pl.BlockSpec((2, None, fw, S5_LANES), lambda g, b: (0, g, 0, 0)),
                  pl.BlockSpec((None, 2, None, nch, fw), lambda g, b: (b, 0, g, 0, 0))],
        out_specs=pl.BlockSpec((None, None, nch, fw), lambda g, b: (b, g, 0, 0)),
        out_shape=jax.ShapeDtypeStruct((nb, gb, nch, fw), F32),
        scratch_shapes=[pltpu.VMEM((2, fw, fw), BF16)],
        compiler_params=_cparams(("arbitrary", "arbitrary"), V7X_VMEM_LIMIT),
        name="s5_state",
    )(x0re, x0im, mstate, yl)


def _gelu_tanh(v):
    return 0.5 * v * (1.0 + jnp.tanh(math.sqrt(2.0 / math.pi) * (v + 0.044715 * v * v * v)))


def _s5_post_kernel(y_ref, u_ref, d_ref, w_ref, o_ref, wbf_ref):
    @pl.when(pl.program_id(0) == 0)
    def _():
        wbf_ref[...] = w_ref[...].astype(BF16)

    t = _gelu_tanh(y_ref[...] + d_ref[...] * u_ref[...].astype(F32))
    o_ref[...] = (t * _sigmoid(_dot(t.astype(BF16), wbf_ref[...]))).astype(o_ref.dtype)


def s5_post(y, z, s5_d_l, w_glu, layer, *, u_col):
    ntok, c = y.shape
    row = pl.BlockSpec((ROW_TILE, c), lambda i: (i, 0))
    return pl.pallas_call(
        _s5_post_kernel,
        grid=(ntok // ROW_TILE,),
        in_specs=[row, pl.BlockSpec((ROW_TILE, c), lambda i: (i, u_col)),
                  pl.BlockSpec((1, c), lambda i: (0, 0)),
                  pl.BlockSpec((None, c, c), lambda i: (layer, 0, 0))],
        out_specs=row,
        out_shape=jax.ShapeDtypeStruct((ntok, c), BF16),
        scratch_shapes=[pltpu.VMEM((c, c), BF16)],
        compiler_params=_cparams(("arbitrary",), V7X_VMEM_LIMIT),
        name="s5_post",
    )(y, z, s5_d_l, w_glu)


def _merge_kernel(a_ref, b_ref, c_ref, s0_ref, s1_ref, s2_ref, wa_ref, wb_ref, wc_ref, o_ref,
                  wa_bf, wb_bf, wc_bf):
    @pl.when(pl.program_id(1) == 0)
    def _():
        wa_bf[...] = wa_ref[...].astype(BF16)
        wb_bf[...] = wb_ref[...].astype(BF16)
        wc_bf[...] = wc_ref[...].astype(BF16)

    m = _sigmoid(s0_ref[...].astype(F32)) * _dot(a_ref[...], wa_bf[...])
    m = m + _sigmoid(s1_ref[...].astype(F32)) * _dot(b_ref[...], wb_bf[...])
    m = m + _sigmoid(s2_ref[...].astype(F32)) * _dot(c_ref[...], wc_bf[...])
    o_ref[...] = m.astype(o_ref.dtype)


def merge_branches(act_a, act_b, act_c, z, conv_proj, ret_proj, s5_proj, layer, *, s_col0, tm, tn):
    m = act_a.shape[0]
    d = conv_proj.shape[2]
    ka, kb, kc = act_a.shape[1], act_b.shape[1], act_c.shape[1]
    nblk = d // tn

    def aspec(k):
        return pl.BlockSpec((tm, k), lambda j, i: (i, 0))

    def sspec(br):
        return pl.BlockSpec((tm, tn), lambda j, i: (i, s_col0 // tn + br * nblk + j))

    def wspec(k):
        return pl.BlockSpec((None, k, tn), lambda j, i: (layer, 0, j))

    return pl.pallas_call(
        _merge_kernel,
        grid=(nblk, m // tm),
        in_specs=[aspec(ka), aspec(kb), aspec(kc), sspec(0), sspec(1), sspec(2),
                  wspec(ka), wspec(kb), wspec(kc)],
        out_specs=pl.BlockSpec((tm, tn), lambda j, i: (i, j)),
        out_shape=jax.ShapeDtypeStruct((m, d), BF16),
        scratch_shapes=[pltpu.VMEM((ka, tn), BF16), pltpu.VMEM((kb, tn), BF16), pltpu.VMEM((kc, tn), BF16)],
        compiler_params=_cparams(("arbitrary", "arbitrary"), V7X_VMEM_LIMIT),
        name="merge_branches",
    )(act_a, act_b, act_c, z, z, z, conv_proj, ret_proj, s5_proj)


def _ffn_kernel(x_ref, wg_ref, wu_ref, wd_ref, o_ref):
    @pl.when(pl.program_id(1) == 0)
    def _():
        o_ref[...] = jnp.zeros(o_ref.shape, F32)

    x = x_ref[...]
    hg = _dot(x, wg_ref[...].astype(BF16))
    hu = _dot(x, wu_ref[...].astype(BF16))
    o_ref[...] += _dot((_silu(hg) * hu).astype(BF16), wd_ref[...].astype(BF16))


def ffn_blocks(x, wg, wu, wd, layer, ff_block, *, tm):
    m, d = x.shape
    ff = wg.shape[2]
    up = pl.BlockSpec((None, d, ff_block), lambda i, e: (layer, 0, e))
    return pl.pallas_call(
        _ffn_kernel,
        grid=(m // tm, ff // ff_block),
        in_specs=[pl.BlockSpec((tm, d), lambda i, e: (i, 0)), up, up,
                  pl.BlockSpec((None, ff_block, d), lambda i, e: (layer, e, 0))],
        out_specs=pl.BlockSpec((tm, d), lambda i, e: (i, 0)),
        out_shape=jax.ShapeDtypeStruct((m, d), F32),
        compiler_params=_cparams(("arbitrary", "arbitrary"), V7X_VMEM_LIMIT),
        name="ffn_blocks",
    )(x, wg, wu, wd)


def _moe_positions_kernel(eidx_ref, off_ref, pos_ref, run_ref):
    @pl.when(pl.program_id(0) == 0)
    def _():
        run_ref[...] = jnp.zeros(run_ref.shape, F32)

    n_e = off_ref.shape[0]
    top_k, tm = eidx_ref.shape
    expert = lax.broadcasted_iota(jnp.int32, (n_e, tm), 0)
    eidx = eidx_ref[...]
    member = jnp.zeros((n_e, tm), F32)
    for k in range(top_k):
        member = member + jnp.where(eidx[k:k + 1, :] == expert, 1.0, 0.0)
    r = lax.broadcasted_iota(jnp.int32, (tm, tm), 0)
    c = lax.broadcasted_iota(jnp.int32, (tm, tm), 1)
    upper = jnp.where(r <= c, 1.0, 0.0).astype(BF16)
    incl = _dot(member.astype(BF16), upper)
    row_of = off_ref[...] + run_ref[...] + incl - member
    rows = [jnp.sum(jnp.where(eidx[k:k + 1, :] == expert, row_of, 0.0), axis=0, keepdims=True)
            for k in range(top_k)]
    pos_ref[...] = jnp.concatenate(rows, axis=0).astype(jnp.int32)
    run_ref[...] = run_ref[...] + jnp.sum(member, axis=1, keepdims=True)


def moe_positions(eidx, offsets):
    top_k, ntok = eidx.shape
    n_e = offsets.shape[0]
    blk = pl.BlockSpec((top_k, ROW_TILE), lambda i: (0, i))
    return pl.pallas_call(
        _moe_positions_kernel,
        grid=(ntok // ROW_TILE,),
        in_specs=[blk, pl.BlockSpec((n_e, 1), lambda i: (0, 0))],
        out_specs=blk,
        out_shape=jax.ShapeDtypeStruct((top_k, ntok), jnp.int32),
        scratch_shapes=[pltpu.VMEM((n_e, 1), F32)],
        compiler_params=_cparams(("arbitrary",), V7X_VMEM_LIMIT),
        name="moe_positions",
    )(eidx, offsets)


def _token_copy(src_hbm, src_tok, buf, slot, dst_tok, sem, n_data_rows):
    src0 = pl.multiple_of(src_tok * ROW_PITCH, 8)
    dst0 = pl.multiple_of(dst_tok * ROW_PITCH, 8)
    return pltpu.make_async_copy(src_hbm.at[pl.ds(src0, n_data_rows), :],
                                 buf.at[slot, pl.ds(dst0, n_data_rows), :], sem.at[slot])


def _gather_rows_start(idx_ref, src_hbm, buf, slot, sem, n_tok, n_data_rows):
    def body(r, carry):
        _token_copy(src_hbm, idx_ref[0, r], buf, slot, r, sem, n_data_rows).start()
        return carry

    lax.fori_loop(0, n_tok, body, 0, unroll=DMA_ISSUE_UNROLL)


def _gather_rows_wait(src_hbm, buf, slot, sem, n_tok, n_data_rows):
    for _ in range(n_tok):
        _token_copy(src_hbm, 0, buf, slot, 0, sem, n_data_rows).wait()


def _moe_ffn_kernel(te_ref, nv_ref, idx_ref, idx_next_ref, h_hbm, wg_ref, wu_ref, wd_ref, o_ref,
                    xbuf, sem, wg_bf, wu_bf, wd_bf):
    j = pl.program_id(0)
    n_valid = nv_ref[0]
    slot = j % 2
    tm = xbuf.shape[1] // ROW_PITCH
    d = wg_ref.shape[0]
    k = _token_data_rows(d)

    @pl.when(j == 0)
    def _():
        _gather_rows_start(idx_ref, h_hbm, xbuf, 0, sem, tm, k)

    @pl.when(j + 1 < n_valid)
    def _():
        _gather_rows_start(idx_next_ref, h_hbm, xbuf, 1 - slot, sem, tm, k)

    @pl.when(jnp.logical_or(j == 0, te_ref[j] != te_ref[jnp.maximum(j - 1, 0)]))
    def _():
        wg_bf[...] = wg_ref[...].astype(BF16)
        wu_bf[...] = wu_ref[...].astype(BF16)
        wd_bf[...] = wd_ref[...].astype(BF16)

    @pl.when(j < n_valid)
    def _():
        _gather_rows_wait(h_hbm, xbuf, slot, sem, tm, k)
        x = _load_packed(xbuf.at[slot], tm, d).astype(BF16)
        hg = _dot(x, wg_bf[...])
        hu = _dot(x, wu_bf[...])
        _store_packed(o_ref, _dot((_silu(hg) * hu).astype(BF16), wd_bf[...]))

    @pl.when(j >= n_valid)
    def _():
        o_ref[...] = jnp.zeros(o_ref.shape, jnp.uint32)


def moe_ffn_sorted(h, src_tok, tile_expert, n_valid, wg, wu, wd, layer, *, tm):
    n_tiles = src_tok.shape[0]
    d, ff = wg.shape[2], wg.shape[3]
    smem_rows = functools.partial(pl.BlockSpec, (None, 1, tm), memory_space=pltpu.SMEM)
    grid_spec = pltpu.PrefetchScalarGridSpec(
        num_scalar_prefetch=2,
        grid=(n_tiles,),
        in_specs=[
            smem_rows(lambda j, te, nv: (j, 0, 0)),
            smem_rows(lambda j, te, nv: (jnp.minimum(j + 1, n_tiles - 1), 0, 0)),
            pl.BlockSpec(memory_space=pl.ANY),
            pl.BlockSpec((None, None, d, ff), lambda j, te, nv: (layer, te[j], 0, 0)),
            pl.BlockSpec((None, None, d, ff), lambda j, te, nv: (layer, te[j], 0, 0)),
            pl.BlockSpec((None, None, ff, d), lambda j, te, nv: (layer, te[j], 0, 0)),
        ],
        out_specs=pl.BlockSpec((tm * ROW_PITCH, LANES), lambda j, te, nv: (j, 0)),
        scratch_shapes=[pltpu.VMEM((2, tm * ROW_PITCH, LANES), jnp.uint32), pltpu.SemaphoreType.DMA((2,)),
                        pltpu.VMEM((d, ff), BF16), pltpu.VMEM((d, ff), BF16), pltpu.VMEM((ff, d), BF16)],
    )
    return pl.pallas_call(
        _moe_ffn_kernel,
        grid_spec=grid_spec,
        out_shape=jax.ShapeDtypeStruct((n_tiles * tm * ROW_PITCH, LANES), jnp.uint32),
        compiler_params=_cparams(("arbitrary",), V7X_VMEM_LIMIT),
        name="moe_ffn_sorted",
    )(tile_expert, n_valid, src_tok, src_tok, h, wg, wu, wd)


def _moe_combine_kernel(pos_ref, pos_next_ref, w_ref, ys_hbm, o_ref, gbuf, sem):
    i = pl.program_id(0)
    n = pl.num_programs(0)
    slot = i % 2
    top_k = gbuf.shape[0] // 2
    tc = gbuf.shape[1] // ROW_PITCH
    n_data_rows = _token_data_rows(o_ref.shape[1])

    def copy(p_ref, s, k, r):
        src0 = pl.multiple_of(p_ref[k, r] * ROW_PITCH, 8)
        dst0 = pl.multiple_of(r * ROW_PITCH, 8)
        return pltpu.make_async_copy(ys_hbm.at[pl.ds(src0, n_data_rows), :],
                                     gbuf.at[s * top_k + k, pl.ds(dst0, n_data_rows), :], sem.at[s])

    def start(p_ref, s):
        for k in range(top_k):
            def body(r2, carry, k=k):
                for prio in range(2):
                    copy(p_ref, s, k, 2 * r2 + prio).start(priority=prio)
                return carry

            lax.fori_loop(0, tc // 2, body, 0, unroll=DMA_ISSUE_UNROLL // 2)

    @pl.when(i == 0)
    def _():
        start(pos_ref, 0)

    @pl.when(i + 1 < n)
    def _():
        start(pos_next_ref, 1 - slot)

    for _ in range(top_k * tc):
        pltpu.make_async_copy(ys_hbm.at[pl.ds(0, n_data_rows), :],
                              gbuf.at[slot * top_k, pl.ds(0, n_data_rows), :], sem.at[slot]).wait()
    w = w_ref[...]
    for c in range(n_data_rows):
        acc_lo = acc_hi = None
        for k in range(top_k):
            lo, hi = _unpack_bf16_pair(gbuf[slot * top_k + k, pl.ds(c, tc, stride=ROW_PITCH), :])
            wk = w[:, k:k + 1]
            acc_lo = wk * lo if acc_lo is None else acc_lo + wk * lo
            acc_hi = wk * hi if acc_hi is None else acc_hi + wk * hi
        o_ref[:, c * LANES:(c + 1) * LANES] = acc_lo
        o_ref[:, (n_data_rows + c) * LANES:(n_data_rows + c + 1) * LANES] = acc_hi


def moe_combine(ys, pos_tiles, w_tok, d, *, tc):
    n_tiles, top_k, _ = pos_tiles.shape
    smem_pos = functools.partial(pl.BlockSpec, (None, top_k, tc), memory_space=pltpu.SMEM)
    return pl.pallas_call(
        _moe_combine_kernel,
        grid=(n_tiles,),
        in_specs=[smem_pos(lambda i: (i, 0, 0)),
                  smem_pos(lambda i: (jnp.minimum(i + 1, n_tiles - 1), 0, 0)),
                  pl.BlockSpec((tc, top_k), lambda i: (i, 0)),
                  pl.BlockSpec(memory_space=pl.ANY)],
        out_specs=pl.BlockSpec((tc, d), lambda i: (i, 0)),
        out_shape=jax.ShapeDtypeStruct((n_tiles * tc, d), F32),
        scratch_shapes=[pltpu.VMEM((2 * top_k, tc * ROW_PITCH, LANES), jnp.uint32),
                        pltpu.SemaphoreType.DMA((2,))],
        compiler_params=_cparams(("arbitrary",), V7X_VMEM_LIMIT),
        name="moe_combine",
    )(pos_tiles, pos_tiles, w_tok, ys)


_HI = lax.Precision.HIGHEST


def _cmul(ar, ai, br, bi):
    return ar * br - ai * bi, ar * bi + ai * br


def s5_operators(a_re, a_im, log_dt, b_re, b_im, c_re, c_im):
    t = S5_T
    dt = jnp.exp(log_dt)[..., None]
    adt_re, adt_im = a_re * dt, a_im * dt
    tau = jnp.arange(t + 1, dtype=F32)[None, None, :, None]
    mag = jnp.exp(adt_re[:, :, None, :] * tau)
    ang = adt_im[:, :, None, :] * tau
    pw_re, pw_im = mag * jnp.cos(ang), mag * jnp.sin(ang)
    ab_re, ab_im = pw_re[:, :, 1], pw_im[:, :, 1]
    den = a_re * a_re + a_im * a_im
    nr, ni = ab_re - 1.0, ab_im
    f_re = (nr * a_re + ni * a_im) / den
    f_im = (ni * a_re - nr * a_im) / den
    bb_re, bb_im = _cmul(f_re[..., None], f_im[..., None], b_re, b_im)
    m1_re, m1_im = _cmul(pw_re[..., None], pw_im[..., None], bb_re[:, :, None], bb_im[:, :, None])
    kk = (jnp.einsum('dgpn,dgtnq->dgtpq', c_re, m1_re[:, :, :t], precision=_HI)
          - jnp.einsum('dgpn,dgtnq->dgtpq', c_im, m1_im[:, :, :t], precision=_HI))
    ti = jnp.arange(t)
    lag = ti[:, None] - ti[None, :]
    kg = kk[:, :, jnp.clip(lag, 0, t - 1)]
    kg = jnp.where((lag >= 0)[None, None, :, :, None, None], kg, 0.0)
    nd, g = a_re.shape[0], a_re.shape[1]
    p = b_re.shape[-1]
    n = a_re.shape[-1]
    inc = jnp.stack([m1_re[:, :, t - 1 - ti], m1_im[:, :, t - 1 - ti]], axis=3)
    w_re, w_im = _cmul(c_re[:, :, None], c_im[:, :, None],
                       pw_re[:, :, 1:, None, :], pw_im[:, :, 1:, None, :])
    mst = jnp.stack([w_re, -w_im], axis=2)

    def reverse_backward(v, axes):
        return jnp.stack([v[0], jnp.flip(v[1], axes)], axis=0)

    kg = reverse_backward(kg, (1, 2))
    inc = reverse_backward(inc, (1,))
    mst = reverse_backward(mst, (2,))
    gpb = S5_LANES // p
    gb = g // gpb
    fw = t * S5_LANES
    assert t * p == S5_LANES and 2 * n == S5_LANES
    toep_c = kg.reshape(nd, gb, gpb, t, t, p, p).transpose(0, 1, 4, 2, 6, 3, 5)
    minc_c = inc.reshape(nd, gb, gpb, t, 2, n, p).transpose(0, 1, 3, 2, 6, 4, 5)
    mstate_c = mst.reshape(nd, gb, gpb, 2, t, p, n).transpose(0, 1, 3, 2, 6, 4, 5)
    a_t_re = pw_re[:, :, t].reshape(nd, 1, g * n)
    a_t_im = pw_im[:, :, t].reshape(nd, 1, g * n)
    return (toep_c.reshape(nd, gb, fw, S5_LANES).astype(BF16), minc_c.reshape(nd, gb, fw, S5_LANES).astype(BF16),
            mstate_c.reshape(nd, gb, fw, S5_LANES).astype(BF16), a_t_re, a_t_im)


def rotary_tables(n_pos, half):
    freq = ROPE_BASE ** (-jnp.arange(half, dtype=F32) / half)
    ang = jnp.arange(n_pos, dtype=F32)[:, None] * freq[None, :]
    return jnp.cos(ang), jnp.sin(ang)


def kernel(x, c, ctx, c_ctx, emb_ln_g, emb_ln_b, ada_w, ada_b, w_in, conv_w, conv_b, conv_ln_g, conv_ln_b, conv_proj, ret_decay_logit, ret_proj, s5_a_re, s5_a_im, s5_log_dt, s5_b_re, s5_b_im, s5_c_re, s5_c_im, s5_d, s5_w_glu, s5_proj, w_out, ln1_g, ln1_b, ln2_g, ln2_b, router_w, router_bias, exp_w_gate, exp_w_up, exp_w_down, sh_w_gate, sh_w_up, sh_w_down):
    n_batch, seq, d = x.shape
    lc = ctx.shape[1]
    depth = w_in.shape[0]
    conv_c = conv_w.shape[2]
    ret_w = ret_proj.shape[1]
    s5_c = s5_d.shape[1]
    s5_g = s5_c // S5_P
    n_exp, _, exp_ff = exp_w_gate.shape[1:]
    sh_ff = sh_w_gate.shape[2]
    rows = seq // GRID_W
    assert lc == ROW_TILE and seq % ROW_TILE == 0 and n_batch + 1 <= MOD_ROWS
    assert RET_HEADS * RET_DK == ret_w and S5_T * S5_LANES == 2 * (S5_LANES // S5_P) * S5_N
    assert s5_c % S5_LANES == 0 and lc % S5_T == 0 and rows % S5_T == 0
    tiles_per_batch = seq // ROW_TILE
    n_lat_tiles = n_batch * tiles_per_batch
    n_tiles = n_lat_tiles + n_batch
    n_lat = n_batch * seq
    alpha = (2.0 * depth) ** 0.25
    ntok = n_tiles * ROW_TILE
    tm_mm = _largest_row_tile(ntok, 2)
    tm_ffn = _largest_row_tile(ntok, 2) if ntok % 544 else 544
    col_a, col_q = 0, 2 * conv_c
    col_g = col_q + 3 * ret_w
    col_u = col_g + ret_w
    col_s = col_u + s5_c

    group_fn = functools.partial(_group_of_tile, n_latent_tiles=n_lat_tiles,
                                 tiles_per_batch=tiles_per_batch, n_batch=n_batch)

    tokens = jnp.concatenate([x.reshape(n_lat, d), ctx.reshape(n_batch * lc, d)], axis=0)
    cvec = jnp.concatenate([c, c_ctx[None, :], jnp.zeros((MOD_ROWS - n_batch - 1, d), F32)], axis=0)
    mods = [ada_modulation(cvec, ada_w, ada_b[i][None, :], i) for i in range(depth)]
    cos_tab, sin_tab = rotary_tables(lc + seq, RET_DK // 2)
    log_g = jax.nn.log_sigmoid(ret_decay_logit.astype(F32))

    xl, h = resid_ln_mod(tokens, [], None, 0, emb_ln_g[None, :], emb_ln_b[None, :], mods[0], 0, 1,
                         alpha=1.0, group_fn=group_fn, n_tiles=n_tiles)

    for i in range(depth):
        last = i == depth - 1
        mod = mods[i]
        z = matmul_stacked_w(h, w_in, i, BF16, tm_mm, 512)
        act_a = jnp.concatenate([
            conformer_conv_act(z, conv_w[i], conv_b[i][None, :], conv_ln_g[i][None, :], conv_ln_b[i][None, :],
                               row_tile0=0, n_tiles=n_lat_tiles, seg=GRID_W),
            conformer_conv_act(z, conv_w[i], conv_b[i][None, :], conv_ln_g[i][None, :], conv_ln_b[i][None, :],
                               row_tile0=n_lat_tiles, n_tiles=n_batch, seg=lc)], axis=0)
        o_f, o_b = retention_scan(z, log_g[i], cos_tab, sin_tab, n_batch=n_batch,
                                  tiles_per_batch=tiles_per_batch, q_col0=col_q // RET_DK)
        act_b = retention_post(o_f, o_b, z, g_col=col_g // ret_w)
        u = z[:, col_u:col_u + s5_c].astype(BF16)
        gb = s5_c // S5_LANES
        fw = S5_T * S5_LANES
        u_lat = u[:n_lat].reshape(n_batch, rows, GRID_W, gb, S5_LANES).transpose(0, 3, 2, 1, 4)
        u_ctx = u[n_lat:].reshape(n_batch, lc, gb, S5_LANES).transpose(0, 2, 1, 3)
        u_fold = jnp.concatenate([u_ctx.reshape(n_batch, gb, lc // S5_T, fw),
                                  u_lat.reshape(n_batch, gb, seq // S5_T, fw)], axis=2)
        ops = s5_operators(s5_a_re[i], s5_a_im[i], s5_log_dt[i], s5_b_re[i], s5_b_im[i], s5_c_re[i], s5_c_im[i])
        y_fold = s5_chunked(u_fold, *ops, n_ctx_chunk=lc // S5_T)
        y_ctx = y_fold[:, :, :lc // S5_T].reshape(n_batch, gb, lc, S5_LANES).transpose(0, 2, 1, 3)
        y_lat = y_fold[:, :, lc // S5_T:].reshape(n_batch, gb, GRID_W, rows, S5_LANES).transpose(0, 3, 2, 1, 4)
        y_s5 = jnp.concatenate([y_lat.reshape(n_lat, s5_c), y_ctx.reshape(n_batch * lc, s5_c)], axis=0)
        act_c = s5_post(y_s5, z, s5_d[i][None, :], s5_w_glu, i, u_col=col_u // s5_c)
        merged = merge_branches(act_a, act_b, act_c, z, conv_proj, ret_proj, s5_proj, i,
                                s_col0=col_s, tm=tm_mm, tn=512)
        y_mix = matmul_stacked_w(merged, w_out, i, F32, tm_mm, 512)
        xl, h2, h2_f32, eidx, wk = resid_ln_mod(
            xl, [y_mix], mod, 2, ln1_g[i][None, :], ln1_b[i][None, :], mod, 3, 4,
            alpha=alpha, group_fn=group_fn, n_tiles=n_tiles,
            router=(router_w[i].T, router_bias[i][:, None]))
        counts = jnp.sum((eidx[None, :, :] == jnp.arange(n_exp, dtype=jnp.int32)[:, None, None]).astype(jnp.int32),
                         axis=(1, 2))
        padded = ((counts + MOE_ROW_TILE - 1) // MOE_ROW_TILE) * MOE_ROW_TILE
        ends = jnp.cumsum(padded)
        n_sorted_tiles = (ntok * TOP_K) // MOE_ROW_TILE + n_exp
        tile_start = jnp.arange(n_sorted_tiles, dtype=jnp.int32) * MOE_ROW_TILE
        tile_expert = jnp.minimum(jnp.sum((ends[None, :] <= tile_start[:, None]).astype(jnp.int32), axis=1),
                                  n_exp - 1)
        n_valid = (ends[-1:] // MOE_ROW_TILE).astype(jnp.int32)
        pos = moe_positions(eidx, (ends - padded).astype(F32)[:, None])
        src_tok = jnp.zeros((n_sorted_tiles * MOE_ROW_TILE,), jnp.int32).at[pos.reshape(-1)].set(
            jnp.tile(jnp.arange(ntok, dtype=jnp.int32), TOP_K))
        y_sorted = moe_ffn_sorted(h2_f32, src_tok.reshape(n_sorted_tiles, 1, MOE_ROW_TILE), tile_expert, n_valid,
                                  exp_w_gate, exp_w_up, exp_w_down, i, tm=MOE_ROW_TILE)
        pos_tiles = pos.reshape(TOP_K, ntok // MOE_COMBINE_TILE, MOE_COMBINE_TILE).transpose(1, 0, 2)
        y_routed = moe_combine(y_sorted, pos_tiles, wk.T, d, tc=MOE_COMBINE_TILE)
        y_shared = ffn_blocks(h2, sh_w_gate, sh_w_up, sh_w_down, i, exp_ff, tm=tm_ffn)
        if last:
            (xl,) = resid_ln_mod(xl, [y_routed, y_shared], mod, 5, ln2_g[i][None, :], ln2_b[i][None, :],
                                 None, 0, 0, alpha=alpha, group_fn=group_fn, n_tiles=n_lat_tiles)
        else:
            xl, h = resid_ln_mod(xl, [y_routed, y_shared], mod, 5, ln2_g[i][None, :], ln2_b[i][None, :],
                                 mods[i + 1], 0, 1, alpha=alpha, group_fn=group_fn, n_tiles=n_tiles)
    return xl.reshape(n_batch, seq, d)
```

```python
import functools
import math

import jax
import jax.numpy as jnp
from jax import lax
from jax.experimental import pallas as pl
from jax.experimental.pallas import tpu as pltpu

F32 = jnp.float32
BF16 = jnp.bfloat16

GRID_W = 64
RET_HEADS = 8
RET_DK = 256
S5_P = 16
S5_N = 64
ROPE_BASE = 10000.0
N_GROUPS = 8
TOPK_GROUPS = 4
TOP_K = 8
ROUTED_SCALE = 2.5
LN_EPS = 1e-5
HEAD_NORM_EPS = 1e-5
NEG_BIG = -1e30
N_BRANCH = 3

ROW_TILE = 256
S5_LANES = 128
S5_T = 8
MOD_ROWS = 8
MOE_ROW_TILE = 256
MOE_COMBINE_TILE = 64
DMA_ISSUE_UNROLL = 8
LANES = 128
ROW_PITCH = 24
V7X_VMEM_LIMIT = 56 * 1024 * 1024


def _largest_row_tile(ntok, max_tiles):
    n = ntok // ROW_TILE
    k = max(t for t in range(1, max_tiles + 1) if n % t == 0)
    return k * ROW_TILE


def _cparams(sem, vmem=None):
    return pltpu.CompilerParams(dimension_semantics=sem, vmem_limit_bytes=vmem)


def _split_bf16(v):
    hi = v.astype(BF16)
    lo = (v - hi.astype(F32)).astype(BF16)
    return hi, lo


def _dot(a, b):
    return jnp.dot(a, b, preferred_element_type=F32)


def _dot_nt(a, b):
    return lax.dot_general(a, b, (((1,), (1,)), ((), ())), preferred_element_type=F32)


def _dot_tn(a, b):
    return lax.dot_general(a, b, (((0,), (0,)), ((), ())), preferred_element_type=F32)


def _sigmoid(v):
    return 1.0 / (1.0 + jnp.exp(-v))


def _silu(v):
    return v * _sigmoid(v)


_HIGH_HALF = 0xFFFF0000


def _pack_bf16_pair(lo, hi):
    lo_bits = pltpu.bitcast(lo.astype(BF16).astype(F32), jnp.uint32) >> 16
    hi_bits = pltpu.bitcast(hi.astype(BF16).astype(F32), jnp.uint32) & jnp.uint32(_HIGH_HALF)
    return lo_bits | hi_bits


def _unpack_bf16_pair(words):
    lo = pltpu.bitcast(words << 16, F32)
    hi = pltpu.bitcast(words & jnp.uint32(_HIGH_HALF), F32)
    return lo, hi


def _token_data_rows(d):
    rows = d // (2 * LANES)
    assert rows * 2 * LANES == d and rows <= ROW_PITCH
    return rows


def _store_packed(ref, v):
    n, d = v.shape
    k = _token_data_rows(d)
    for c in range(k):
        lo = v[:, c * LANES:(c + 1) * LANES]
        hi = v[:, (k + c) * LANES:(k + c + 1) * LANES]
        ref[pl.ds(c, n, stride=ROW_PITCH), :] = _pack_bf16_pair(lo, hi)
    for c in range(k, ROW_PITCH):
        ref[pl.ds(c, n, stride=ROW_PITCH), :] = jnp.zeros((n, LANES), jnp.uint32)


def _load_packed(ref, n, d):
    k = _token_data_rows(d)
    parts = [_unpack_bf16_pair(ref[pl.ds(c, n, stride=ROW_PITCH), :]) for c in range(k)]
    return jnp.concatenate([p[0] for p in parts] + [p[1] for p in parts], axis=1)


def _layer_norm_rows(v, g, b, eps):
    mu = jnp.mean(v, axis=-1, keepdims=True)
    vc = v - mu
    var = jnp.mean(vc * vc, axis=-1, keepdims=True)
    return vc * lax.rsqrt(var + eps) * g + b


def _ada_kernel(c_ref, w_ref, b_ref, o_ref):
    c = _silu(c_ref[...])
    ch, cl = _split_bf16(c)
    wh, wl = _split_bf16(w_ref[...])
    o_ref[...] = _dot(ch, wh) + _dot(ch, wl) + _dot(cl, wh) + b_ref[...]


def ada_modulation(cvec, ada_w, ada_b_l, layer):
    _, d, n = ada_w.shape
    tn = 512
    return pl.pallas_call(
        _ada_kernel,
        grid=(n // tn,),
        in_specs=[
            pl.BlockSpec((MOD_ROWS, d), lambda j: (0, 0)),
            pl.BlockSpec((None, d, tn), lambda j: (layer, 0, j)),
            pl.BlockSpec((1, tn), lambda j: (0, j)),
        ],
        out_specs=pl.BlockSpec((MOD_ROWS, tn), lambda j: (0, j)),
        out_shape=jax.ShapeDtypeStruct((MOD_ROWS, n), F32),
        compiler_params=_cparams(("arbitrary",), V7X_VMEM_LIMIT),
        name="ada_modulation",
    )(cvec, ada_w, ada_b_l)


def _group_of_tile(i, n_latent_tiles, tiles_per_batch, n_batch):
    return jnp.where(i < n_latent_tiles, i // tiles_per_batch, n_batch)


def _route(h, rwt_ref, rb_ref):
    hh, hl = _split_bf16(h)
    wh, wl = _split_bf16(rwt_ref[...])
    logits = _dot_nt(wh, hh) + _dot_nt(wh, hl) + _dot_nt(wl, hh)
    scores = _sigmoid(logits)
    sel = scores + rb_ref[...]
    n_e, tm = sel.shape
    per = n_e // N_GROUPS
    shape3 = (N_GROUPS, per, tm)
    sel3 = sel.reshape(shape3)
    io_e = lax.broadcasted_iota(jnp.int32, shape3, 1)
    io_g = lax.broadcasted_iota(jnp.int32, shape3, 0)
    m1 = jnp.max(sel3, axis=1, keepdims=True)
    first = jnp.min(jnp.where(sel3 == m1, io_e, per), axis=1, keepdims=True)
    m2 = jnp.max(jnp.where(io_e == first, -jnp.inf, sel3), axis=1, keepdims=True)
    work = m1 + m2
    iog1 = lax.broadcasted_iota(jnp.int32, work.shape, 0)
    gsel = jnp.zeros(work.shape, F32)
    for _ in range(TOPK_GROUPS):
        m = jnp.max(work, axis=0, keepdims=True)
        fi = jnp.min(jnp.where(work == m, iog1, N_GROUPS), axis=0, keepdims=True)
        hit = iog1 == fi
        gsel = jnp.where(hit, 1.0, gsel)
        work = jnp.where(hit, -jnp.inf, work)
    work = jnp.where(jnp.broadcast_to(gsel, shape3) > 0.0, sel3, NEG_BIG)
    flat = io_g * per + io_e
    scores3 = scores.reshape(shape3)
    picked, picked_score = [], []
    for _ in range(TOP_K):
        m = jnp.max(jnp.max(work, axis=1, keepdims=True), axis=0, keepdims=True)
        cand = jnp.where(work == m, flat, n_e)
        fi = jnp.min(jnp.min(cand, axis=1, keepdims=True), axis=0, keepdims=True)
        hit = flat == fi
        sk = jnp.sum(jnp.sum(jnp.where(hit, scores3, 0.0), axis=1, keepdims=True), axis=0, keepdims=True)
        picked.append(fi.reshape(1, tm))
        picked_score.append(sk.reshape(1, tm))
        work = jnp.where(hit, -jnp.inf, work)
    eidx = jnp.concatenate(picked, axis=0)
    w = jnp.concatenate(picked_score, axis=0)
    return eidx, ROUTED_SCALE * w / jnp.sum(w, axis=0, keepdims=True)


def _resid_ln_mod_kernel(*refs, alpha, n_y, has_mod, has_router, group_fn):
    it = iter(refs)
    x_ref = next(it)
    y_refs = [next(it) for _ in range(n_y)]
    gate_ref = next(it) if n_y else None
    g_ref, b_ref = next(it), next(it)
    shift_ref = scale_ref = rwt_ref = rb_ref = None
    if has_mod:
        shift_ref, scale_ref = next(it), next(it)
    if has_router:
        rwt_ref, rb_ref = next(it), next(it)
    xl_ref = next(it)
    h_ref = next(it) if has_mod else None
    hf_ref, eidx_ref, wk_ref = (next(it), next(it), next(it)) if has_router else (None, None, None)

    grp = group_fn(pl.program_id(0))
    v = x_ref[...]
    if n_y:
        y = y_refs[0][...]
        for r in y_refs[1:]:
            y = y + r[...]
        v = alpha * v + gate_ref[pl.ds(grp, 1), :] * y
    xl = _layer_norm_rows(v, g_ref[...], b_ref[...], LN_EPS)
    xl_ref[...] = xl
    if has_mod:
        h = xl * (1.0 + scale_ref[pl.ds(grp, 1), :]) + shift_ref[pl.ds(grp, 1), :]
        h_ref[...] = h.astype(h_ref.dtype)
        if has_router:
            _store_packed(hf_ref, h)
            eidx_ref[...], wk_ref[...] = _route(h, rwt_ref, rb_ref)


def resid_ln_mod(x, ys, gate_mod, gate_col, ln_g, ln_b, mod, shift_col, scale_col, *,
                 alpha, group_fn, n_tiles, router=None):
    d = x.shape[1]
    n_y = len(ys)
    has_mod = mod is not None
    has_router = router is not None
    row = pl.BlockSpec((ROW_TILE, d), lambda i: (i, 0))
    vec = pl.BlockSpec((1, d), lambda i: (0, 0))
    args, specs = [x], [row]
    for y in ys:
        args.append(y)
        specs.append(row)
    if n_y:
        args.append(gate_mod)
        specs.append(pl.BlockSpec((MOD_ROWS, d), lambda i: (0, gate_col)))
    args += [ln_g, ln_b]
    specs += [vec, vec]
    if has_mod:
        args += [mod, mod]
        specs += [pl.BlockSpec((MOD_ROWS, d), lambda i: (0, shift_col)),
                  pl.BlockSpec((MOD_ROWS, d), lambda i: (0, scale_col))]
    out_shapes = [jax.ShapeDtypeStruct((n_tiles * ROW_TILE, d), F32)]
    out_specs = [row]
    if has_mod:
        out_shapes.append(jax.ShapeDtypeStruct((n_tiles * ROW_TILE, d), BF16))
        out_specs.append(row)
    if has_router:
        rwt, rb = router
        n_e = rwt.shape[0]
        args += [rwt, rb]
        specs += [pl.BlockSpec((n_e, d), lambda i: (0, 0)), pl.BlockSpec((n_e, 1), lambda i: (0, 0))]
        out_shapes += [jax.ShapeDtypeStruct((n_tiles * ROW_TILE * ROW_PITCH, LANES), jnp.uint32),
                       jax.ShapeDtypeStruct((TOP_K, n_tiles * ROW_TILE), jnp.int32),
                       jax.ShapeDtypeStruct((TOP_K, n_tiles * ROW_TILE), F32)]
        out_specs += [pl.BlockSpec((ROW_TILE * ROW_PITCH, LANES), lambda i: (i, 0)),
                      pl.BlockSpec((TOP_K, ROW_TILE), lambda i: (0, i)),
                      pl.BlockSpec((TOP_K, ROW_TILE), lambda i: (0, i))]
    kern = functools.partial(_resid_ln_mod_kernel, alpha=alpha, n_y=n_y, has_mod=has_mod,
                             has_router=has_router, group_fn=group_fn)
    return pl.pallas_call(
        kern, grid=(n_tiles,), in_specs=specs, out_specs=out_specs, out_shape=out_shapes,
        compiler_params=_cparams(("arbitrary",), V7X_VMEM_LIMIT),
        name="resid_ln_mod",
    )(*args)


def _mm_kernel(x_ref, w_ref, o_ref, wbf_ref):
    @pl.when(pl.program_id(1) == 0)
    def _():
        wbf_ref[...] = w_ref[...].astype(BF16)

    o_ref[...] = _dot(x_ref[...], wbf_ref[...]).astype(o_ref.dtype)


def matmul_stacked_w(x, w, layer, out_dtype, tm, tn):
    m, k = x.shape
    n = w.shape[2]
    return pl.pallas_call(
        _mm_kernel,
        grid=(n // tn, m // tm),
        in_specs=[
            pl.BlockSpec((tm, k), lambda j, i: (i, 0)),
            pl.BlockSpec((None, k, tn), lambda j, i: (layer, 0, j)),
        ],
        out_specs=pl.BlockSpec((tm, tn), lambda j, i: (i, j)),
        out_shape=jax.ShapeDtypeStruct((m, n), out_dtype),
        scratch_shapes=[pltpu.VMEM((k, tn), BF16)],
        compiler_params=_cparams(("arbitrary", "arbitrary"), V7X_VMEM_LIMIT),
        name="matmul",
    )(x, w)


def _conv_kernel(a1_ref, a2_ref, w_ref, cb_ref, g_ref, b_ref, o_ref, pad_ref, y_ref, *, seg, n_tap):
    half = n_tap // 2
    front = ((half + 7) // 8) * 8
    nseg = ROW_TILE // seg
    c = a1_ref.shape[1]
    u = a1_ref[...].astype(F32) * _sigmoid(a2_ref[...].astype(F32))
    pad_ref[...] = jnp.zeros(pad_ref.shape, F32)
    for s in range(nseg):
        pad_ref[s, front:front + seg, :] = u[s * seg:(s + 1) * seg, :]
    lanes = 128

    def chunk(ci, carry):
        c0 = pl.multiple_of(ci * lanes, lanes)
        acc = jnp.zeros((nseg, seg, lanes), F32)
        for k in range(n_tap):
            off = front - half + k
            acc = acc + w_ref[k:k + 1, pl.ds(c0, lanes)] * pad_ref[:, off:off + seg, pl.ds(c0, lanes)]
        y_ref[:, pl.ds(c0, lanes)] = acc.reshape(ROW_TILE, lanes)
        return carry

    lax.fori_loop(0, c // lanes, chunk, 0)
    y = y_ref[...] + cb_ref[...]
    o_ref[...] = _silu(_layer_norm_rows(y, g_ref[...], b_ref[...], LN_EPS)).astype(o_ref.dtype)


def conformer_conv_act(z, conv_w_l, conv_b_l, ln_g_l, ln_b_l, *, row_tile0, n_tiles, seg):
    n_tap, c = conv_w_l.shape
    half = n_tap // 2
    front = ((half + 7) // 8) * 8
    nseg = ROW_TILE // seg
    vec = pl.BlockSpec((1, c), lambda i: (0, 0))
    kern = functools.partial(_conv_kernel, seg=seg, n_tap=n_tap)
    return pl.pallas_call(
        kern,
        grid=(n_tiles,),
        in_specs=[
            pl.BlockSpec((ROW_TILE, c), lambda i: (row_tile0 + i, 0)),
            pl.BlockSpec((ROW_TILE, c), lambda i: (row_tile0 + i, 1)),
            pl.BlockSpec((n_tap, c), lambda i: (0, 0)),
            vec, vec, vec,
        ],
        out_specs=pl.BlockSpec((ROW_TILE, c), lambda i: (i, 0)),
        out_shape=jax.ShapeDtypeStruct((n_tiles * ROW_TILE, c), BF16),
        scratch_shapes=[pltpu.VMEM((nseg, seg + 2 * front, c), F32), pltpu.VMEM((ROW_TILE, c), F32)],
        compiler_params=_cparams(("arbitrary",), V7X_VMEM_LIMIT),
        name="conformer_conv",
    )(z, z, conv_w_l, conv_b_l, ln_g_l, ln_b_l)


def _rotary(t, cos, sin):
    half = t.shape[1] // 2
    t1, t2 = t[:, :half], t[:, half:]
    return jnp.concatenate([t1 * cos - t2 * sin, t1 * sin + t2 * cos], axis=1)


def _retention_kernel(logg_ref, qf_ref, kf_ref, vf_ref, cf_ref, sf_ref,
                      qb_ref, kb_ref, vb_ref, cb_ref, sb_ref,
                      of_ref, ob_ref, state_ref):
    h = pl.program_id(1)
    n = pl.program_id(2)
    c = ROW_TILE
    k_scale = RET_DK ** -0.5

    @pl.when(n == 0)
    def _():
        state_ref[...] = jnp.zeros(state_ref.shape, F32)

    row = lax.broadcasted_iota(jnp.int32, (c, c), 0)
    col = lax.broadcasted_iota(jnp.int32, (c, c), 1)
    ridx = lax.broadcasted_iota(jnp.int32, (c, 1), 0).astype(F32)

    def one_direction(d, q_ref, k_ref, v_ref, cos_ref, sin_ref, o_ref):
        lg = logg_ref[d, h]
        cos, sin = cos_ref[...], sin_ref[...]
        q = _rotary(q_ref[...].astype(F32), cos, sin)
        k = _rotary(k_ref[...].astype(F32), cos, sin) * k_scale
        v = v_ref[...].astype(BF16)
        dist = (row - col) if d == 0 else (col - row)
        decay = jnp.where(dist >= 0, jnp.exp(lg * jnp.maximum(dist, 0).astype(F32)), 0.0)
        qb = q.astype(BF16)
        scores = _dot_nt(qb, k.astype(BF16)) * decay
        inner = _dot(scores.astype(BF16), v)
        to_prev = (ridx + 1.0) if d == 0 else (c - ridx)
        to_end = (c - 1.0 - ridx) if d == 0 else ridx
        s_prev = state_ref[d]
        cross = _dot(qb, s_prev.astype(BF16)) * jnp.exp(lg * to_prev)
        o_ref[...] = inner + cross
        kw = (k * jnp.exp(lg * to_end)).astype(BF16)
        state_ref[d] = jnp.exp(lg * jnp.full((1, 1), float(c), F32)) * s_prev + _dot_tn(kw, v)

    one_direction(0, qf_ref, kf_ref, vf_ref, cf_ref, sf_ref, of_ref)
    one_direction(1, qb_ref, kb_ref, vb_ref, cb_ref, sb_ref, ob_ref)


def retention_scan(z, log_g, cos_tab, sin_tab, *, n_batch, tiles_per_batch, q_col0):
    ntok = z.shape[0]
    n_lat = n_batch * tiles_per_batch
    n_steps = tiles_per_batch + 1
    dk = RET_DK
    hh = RET_HEADS

    def row_f(b, n):
        return jnp.where(n == 0, n_lat + b, b * tiles_per_batch + n - 1)

    def row_b(b, n):
        return jnp.where(n == 0, n_lat + b, b * tiles_per_batch + tiles_per_batch - n)

    def pos_f(n):
        return n

    def pos_b(n):
        return jnp.where(n == 0, 0, tiles_per_batch + 1 - n)

    def zspec(rowfn, sec):
        return pl.BlockSpec((ROW_TILE, dk), lambda b, h, n: (rowfn(b, n), q_col0 + sec * hh + h))

    def tspec(posfn):
        return pl.BlockSpec((ROW_TILE, dk // 2), lambda b, h, n: (posfn(n), 0))

    def ospec(rowfn):
        return pl.BlockSpec((ROW_TILE, dk), lambda b, h, n: (rowfn(b, n), h))

    smem = pl.BlockSpec(memory_space=pltpu.SMEM)
    return pl.pallas_call(
        _retention_kernel,
        grid=(n_batch, hh, n_steps),
        in_specs=[smem,
                  zspec(row_f, 0), zspec(row_f, 1), zspec(row_f, 2), tspec(pos_f), tspec(pos_f),
                  zspec(row_b, 0), zspec(row_b, 1), zspec(row_b, 2), tspec(pos_b), tspec(pos_b)],
        out_specs=[ospec(row_f), ospec(row_b)],
        out_shape=[jax.ShapeDtypeStruct((ntok, hh * dk), F32)] * 2,
        scratch_shapes=[pltpu.VMEM((2, dk, dk), F32)],
        compiler_params=_cparams(("arbitrary", "arbitrary", "arbitrary"), V7X_VMEM_LIMIT),
        name="retention_scan",
    )(log_g, z, z, z, cos_tab, sin_tab, z, z, z, cos_tab, sin_tab)


def _ret_post_kernel(of_ref, ob_ref, g_ref, o_ref):
    dk = RET_DK
    for h in range(RET_HEADS):
        sl = slice(h * dk, (h + 1) * dk)
        o = of_ref[:, sl] + ob_ref[:, sl]
        mu = jnp.mean(o, axis=-1, keepdims=True)
        oc = o - mu
        var = jnp.mean(oc * oc, axis=-1, keepdims=True)
        on = oc * lax.rsqrt(var + HEAD_NORM_EPS)
        o_ref[:, sl] = (on * _silu(g_ref[:, sl].astype(F32))).astype(o_ref.dtype)


def retention_post(o_f, o_b, z, *, g_col):
    ntok, w = o_f.shape
    row = pl.BlockSpec((ROW_TILE, w), lambda i: (i, 0))
    return pl.pallas_call(
        _ret_post_kernel,
        grid=(ntok // ROW_TILE,),
        in_specs=[row, row, pl.BlockSpec((ROW_TILE, w), lambda i: (i, g_col))],
        out_specs=row,
        out_shape=jax.ShapeDtypeStruct((ntok, w), BF16),
        compiler_params=_cparams(("arbitrary",), V7X_VMEM_LIMIT),
        name="retention_post",
    )(o_f, o_b, z)


def _expand_block_diag(compact, w, rows_per_group):
    gpb = S5_LANES // S5_P
    n_rows, k = compact.shape
    n_cols = k * gpb
    lw, lg, lr = w.bit_length() - 1, gpb.bit_length() - 1, rows_per_group.bit_length() - 1
    i = lax.broadcasted_iota(jnp.int32, (k, n_cols), 0)
    c = lax.broadcasted_iota(jnp.int32, (k, n_cols), 1)
    src = ((c >> (lw + lg)) << lw) + (c & (w - 1))
    rep = jnp.where(i == src, 1.0, 0.0).astype(BF16)
    full = _dot(compact, rep)
    r = lax.broadcasted_iota(jnp.int32, (n_rows, n_cols), 0)
    c2 = lax.broadcasted_iota(jnp.int32, (n_rows, n_cols), 1)
    keep = ((r >> lr) & (gpb - 1)) == ((c2 >> lw) & (gpb - 1))
    return jnp.where(keep, full, 0.0).astype(BF16)


def _s5_local_kernel(u_ref, kc_ref, ic_ref, yl_ref, xre_ref, xim_ref, toep_ref, minc_ref):
    @pl.when(pl.program_id(2) == 0)
    def _():
        toep_ref[...] = _expand_block_diag(kc_ref[...], S5_P, S5_P)
        minc_ref[...] = _expand_block_diag(ic_ref[...], S5_N, S5_P)

    u = u_ref[...]
    yl_ref[...] = _dot(u, toep_ref[...])
    xi = _dot(u, minc_ref[...])
    half = xi.shape[1] // 2
    xre_ref[...] = xi[:, :half]
    xim_ref[...] = xi[:, half:]


def _s5_scan_kernel(xre_ref, xim_ref, are_ref, aim_ref, ore_ref, oim_ref, *, n_ctx_chunk):
    n_chunk = xre_ref.shape[0]
    backward = pl.program_id(1) == 1
    ar, ai = are_ref[...], aim_ref[...]

    def step(i, carry):
        sr, si = carry
        rev = jnp.where(i < n_ctx_chunk, n_ctx_chunk - 1 - i, n_chunk + n_ctx_chunk - 1 - i)
        c = jnp.where(backward, rev, i)
        ore_ref[pl.ds(c, 1), :] = sr
        oim_ref[pl.ds(c, 1), :] = si
        nr = ar * sr - ai * si + xre_ref[pl.ds(c, 1), :]
        ni = ar * si + ai * sr + xim_ref[pl.ds(c, 1), :]
        return nr, ni

    zero = jnp.zeros(ar.shape, F32)
    lax.fori_loop(0, n_chunk, step, (zero, zero))


def _s5_state_kernel(xre_ref, xim_ref, mc_ref, yl_ref, y_ref, mst_ref):
    @pl.when(pl.program_id(1) == 0)
    def _():
        for d in range(2):
            mst_ref[d] = _expand_block_diag(mc_ref[d], S5_P, S5_N)

    y = yl_ref[0] + yl_ref[1]
    for d in range(2):
        x0 = jnp.concatenate([xre_ref[d], xim_ref[d]], axis=1).astype(BF16)
        y = y + _dot(x0, mst_ref[d])
    y_ref[...] = y


def s5_chunked(u_fold, toep, minc, mstate, a_re, a_im, *, n_ctx_chunk):
    nb, gb, nch, fw = u_fold.shape
    sw = fw // 2
    op = pl.BlockSpec((None, None, fw, S5_LANES), lambda g, d, b: (d, g, 0, 0))
    yl, xre, xim = pl.pallas_call(
        _s5_local_kernel,
        grid=(gb, 2, nb),
        in_specs=[pl.BlockSpec((None, None, nch, fw), lambda g, d, b: (b, g, 0, 0)), op, op],
        out_specs=[pl.BlockSpec((None, None, None, nch, fw), lambda g, d, b: (b, d, g, 0, 0)),
                   pl.BlockSpec((None, None, nch, sw), lambda g, d, b: (b, d, 0, g)),
                   pl.BlockSpec((None, None, nch, sw), lambda g, d, b: (b, d, 0, g))],
        out_shape=[jax.ShapeDtypeStruct((nb, 2, gb, nch, fw), F32),
                   jax.ShapeDtypeStruct((nb, 2, nch, gb * sw), F32),
                   jax.ShapeDtypeStruct((nb, 2, nch, gb * sw), F32)],
        scratch_shapes=[pltpu.VMEM((fw, fw), BF16), pltpu.VMEM((fw, fw), BF16)],
        compiler_params=_cparams(("arbitrary", "arbitrary", "arbitrary"), V7X_VMEM_LIMIT),
        name="s5_local",
    )(u_fold, toep, minc)
    scan_w = 2 * sw
    full = pl.BlockSpec((None, None, nch, scan_w), lambda b, d, j: (b, d, 0, j))
    avec = pl.BlockSpec((None, 1, scan_w), lambda b, d, j: (d, 0, j))
    x0re, x0im = pl.pallas_call(
        functools.partial(_s5_scan_kernel, n_ctx_chunk=n_ctx_chunk),
        grid=(nb, 2, gb * sw // scan_w),
        in_specs=[full, full, avec, avec],
        out_specs=[full, full],
        out_shape=[jax.ShapeDtypeStruct((nb, 2, nch, gb * sw), F32)] * 2,
        compiler_params=_cparams(("arbitrary", "arbitrary", "arbitrary"), V7X_VMEM_LIMIT),
        name="s5_scan",
    )(xre, xim, a_re, a_im)
    xcol = pl.BlockSpec((None, 2, nch, sw), lambda g, b: (b, 0, 0, g))
    return pl.pallas_call(
        _s5_state_kernel,
        grid=(gb, nb),
        in_specs=[xcol, xcol,
                  pl.BlockSpec((2, None, fw, S5_LANES), lambda g, b: (0, g, 0, 0)),
                  pl.BlockSpec((None, 2, None, nch, fw), lambda g, b: (b, 0, g, 0, 0))],
        out_specs=pl.BlockSpec((None, None, nch, fw), lambda g, b: (b, g, 0, 0)),
        out_shape=jax.ShapeDtypeStruct((nb, gb, nch, fw), F32),
        scratch_shapes=[pltpu.VMEM((2, fw, fw), BF16)],
        compiler_params=_cparams(("arbitrary", "arbitrary"), V7X_VMEM_LIMIT),
        name="s5_state",
    )(x0re, x0im, mstate, yl)


def _gelu_tanh(v):
    return 0.5 * v * (1.0 + jnp.tanh(math.sqrt(2.0 / math.pi) * (v + 0.044715 * v * v * v)))


def _s5_post_kernel(y_ref, u_ref, d_ref, w_ref, o_ref, wbf_ref):
    @pl.when(pl.program_id(0) == 0)
    def _():
        wbf_ref[...] = w_ref[...].astype(BF16)

    t = _gelu_tanh(y_ref[...] + d_ref[...] * u_ref[...].astype(F32))
    o_ref[...] = (t * _sigmoid(_dot(t.astype(BF16), wbf_ref[...]))).astype(o_ref.dtype)


def s5_post(y, z, s5_d_l, w_glu, layer, *, u_col):
    ntok, c = y.shape
    row = pl.BlockSpec((ROW_TILE, c), lambda i: (i, 0))
    return pl.pallas_call(
        _s5_post_kernel,
        grid=(ntok // ROW_TILE,),
        in_specs=[row, pl.BlockSpec((ROW_TILE, c), lambda i: (i, u_col)),
                  pl.BlockSpec((1, c), lambda i: (0, 0)),
                  pl.BlockSpec((None, c, c), lambda i: (layer, 0, 0))],
        out_specs=row,
        out_shape=jax.ShapeDtypeStruct((ntok, c), BF16),
        scratch_shapes=[pltpu.VMEM((c, c), BF16)],
        compiler_params=_cparams(("arbitrary",), V7X_VMEM_LIMIT),
        name="s5_post",
    )(y, z, s5_d_l, w_glu)


def _merge_kernel(a_ref, b_ref, c_ref, s0_ref, s1_ref, s2_ref, wa_ref, wb_ref, wc_ref, o_ref,
                  wa_bf, wb_bf, wc_bf):
    @pl.when(pl.program_id(1) == 0)
    def _():
        wa_bf[...] = wa_ref[...].astype(BF16)
        wb_bf[...] = wb_ref[...].astype(BF16)
        wc_bf[...] = wc_ref[...].astype(BF16)

    m = _sigmoid(s0_ref[...].astype(F32)) * _dot(a_ref[...], wa_bf[...])
    m = m + _sigmoid(s1_ref[...].astype(F32)) * _dot(b_ref[...], wb_bf[...])
    m = m + _sigmoid(s2_ref[...].astype(F32)) * _dot(c_ref[...], wc_bf[...])
    o_ref[...] = m.astype(o_ref.dtype)


def merge_branches(act_a, act_b, act_c, z, conv_proj, ret_proj, s5_proj, layer, *, s_col0, tm, tn):
    m = act_a.shape[0]
    d = conv_proj.shape[2]
    ka, kb, kc = act_a.shape[1], act_b.shape[1], act_c.shape[1]
    nblk = d // tn

    def aspec(k):
        return pl.BlockSpec((tm, k), lambda j, i: (i, 0))

    def sspec(br):
        return pl.BlockSpec((tm, tn), lambda j, i: (i, s_col0 // tn + br * nblk + j))

    def wspec(k):
        return pl.BlockSpec((None, k, tn), lambda j, i: (layer, 0, j))

    return pl.pallas_call(
        _merge_kernel,
        grid=(nblk, m // tm),
        in_specs=[aspec(ka), aspec(kb), aspec(kc), sspec(0), sspec(1), sspec(2),
                  wspec(ka), wspec(kb), wspec(kc)],
        out_specs=pl.BlockSpec((tm, tn), lambda j, i: (i, j)),
        out_shape=jax.ShapeDtypeStruct((m, d), BF16),
        scratch_shapes=[pltpu.VMEM((ka, tn), BF16), pltpu.VMEM((kb, tn), BF16), pltpu.VMEM((kc, tn), BF16)],
        compiler_params=_cparams(("arbitrary", "arbitrary"), V7X_VMEM_LIMIT),
        name="merge_branches",
    )(act_a, act_b, act_c, z, z, z, conv_proj, ret_proj, s5_proj)


def _ffn_kernel(x_ref, wg_ref, wu_ref, wd_ref, o_ref):
    @pl.when(pl.program_id(1) == 0)
    def _():
        o_ref[...] = jnp.zeros(o_ref.shape, F32)

    x = x_ref[...]
    hg = _dot(x, wg_ref[...].astype(BF16))
    hu = _dot(x, wu_ref[...].astype(BF16))
    o_ref[...] += _dot((_silu(hg) * hu).astype(BF16), wd_ref[...].astype(BF16))


def ffn_blocks(x, wg, wu, wd, layer, ff_block, *, tm):
    m, d = x.shape
    ff = wg.shape[2]
    up = pl.BlockSpec((None, d, ff_block), lambda i, e: (layer, 0, e))
    return pl.pallas_call(
        _ffn_kernel,
        grid=(m // tm, ff // ff_block),
        in_specs=[pl.BlockSpec((tm, d), lambda i, e: (i, 0)), up, up,
                  pl.BlockSpec((None, ff_block, d), lambda i, e: (layer, e, 0))],
        out_specs=pl.BlockSpec((tm, d), lambda i, e: (i, 0)),
        out_shape=jax.ShapeDtypeStruct((m, d), F32),
        compiler_params=_cparams(("arbitrary", "arbitrary"), V7X_VMEM_LIMIT),
        name="ffn_blocks",
    )(x, wg, wu, wd)


def _moe_positions_kernel(eidx_ref, off_ref, pos_ref, run_ref):
    @pl.when(pl.program_id(0) == 0)
    def _():
        run_ref[...] = jnp.zeros(run_ref.shape, F32)

    n_e = off_ref.shape[0]
    top_k, tm = eidx_ref.shape
    expert = lax.broadcasted_iota(jnp.int32, (n_e, tm), 0)
    eidx = eidx_ref[...]
    member = jnp.zeros((n_e, tm), F32)
    for k in range(top_k):
        member = member + jnp.where(eidx[k:k + 1, :] == expert, 1.0, 0.0)
    r = lax.broadcasted_iota(jnp.int32, (tm, tm), 0)
    c = lax.broadcasted_iota(jnp.int32, (tm, tm), 1)
    upper = jnp.where(r <= c, 1.0, 0.0).astype(BF16)
    incl = _dot(member.astype(BF16), upper)
    row_of = off_ref[...] + run_ref[...] + incl - member
    rows = [jnp.sum(jnp.where(eidx[k:k + 1, :] == expert, row_of, 0.0), axis=0, keepdims=True)
            for k in range(top_k)]
    pos_ref[...] = jnp.concatenate(rows, axis=0).astype(jnp.int32)
    run_ref[...] = run_ref[...] + jnp.sum(member, axis=1, keepdims=True)


def moe_positions(eidx, offsets):
    top_k, ntok = eidx.shape
    n_e = offsets.shape[0]
    blk = pl.BlockSpec((top_k, ROW_TILE), lambda i: (0, i))
    return pl.pallas_call(
        _moe_positions_kernel,
        grid=(ntok // ROW_TILE,),
        in_specs=[blk, pl.BlockSpec((n_e, 1), lambda i: (0, 0))],
        out_specs=blk,
        out_shape=jax.ShapeDtypeStruct((top_k, ntok), jnp.int32),
        scratch_shapes=[pltpu.VMEM((n_e, 1), F32)],
        compiler_params=_cparams(("arbitrary",), V7X_VMEM_LIMIT),
        name="moe_positions",
    )(eidx, offsets)


def _token_copy(src_hbm, src_tok, buf, slot, dst_tok, sem, n_data_rows):
    src0 = pl.multiple_of(src_tok * ROW_PITCH, 8)
    dst0 = pl.multiple_of(dst_tok * ROW_PITCH, 8)
    return pltpu.make_async_copy(src_hbm.at[pl.ds(src0, n_data_rows), :],
                                 buf.at[slot, pl.ds(dst0, n_data_rows), :], sem.at[slot])


def _gather_rows_start(idx_ref, src_hbm, buf, slot, sem, n_tok, n_data_rows):
    def body(r, carry):
        _token_copy(src_hbm, idx_ref[0, r], buf, slot, r, sem, n_data_rows).start()
        return carry

    lax.fori_loop(0, n_tok, body, 0, unroll=DMA_ISSUE_UNROLL)


def _gather_rows_wait(src_hbm, buf, slot, sem, n_tok, n_data_rows):
    for _ in range(n_tok):
        _token_copy(src_hbm, 0, buf, slot, 0, sem, n_data_rows).wait()


def _moe_ffn_kernel(te_ref, nv_ref, idx_ref, idx_next_ref, h_hbm, wg_ref, wu_ref, wd_ref, o_ref,
                    xbuf, sem, wg_bf, wu_bf, wd_bf):
    j = pl.program_id(0)
    n_valid = nv_ref[0]
    slot = j % 2
    tm = xbuf.shape[1] // ROW_PITCH
    d = wg_ref.shape[0]
    k = _token_data_rows(d)

    @pl.when(j == 0)
    def _():
        _gather_rows_start(idx_ref, h_hbm, xbuf, 0, sem, tm, k)

    @pl.when(j + 1 < n_valid)
    def _():
        _gather_rows_start(idx_next_ref, h_hbm, xbuf, 1 - slot, sem, tm, k)

    @pl.when(jnp.logical_or(j == 0, te_ref[j] != te_ref[jnp.maximum(j - 1, 0)]))
    def _():
        wg_bf[...] = wg_ref[...].astype(BF16)
        wu_bf[...] = wu_ref[...].astype(BF16)
        wd_bf[...] = wd_ref[...].astype(BF16)

    @pl.when(j < n_valid)
    def _():
        _gather_rows_wait(h_hbm, xbuf, slot, sem, tm, k)
        x = _load_packed(xbuf.at[slot], tm, d).astype(BF16)
        hg = _dot(x, wg_bf[...])
        hu = _dot(x, wu_bf[...])
        _store_packed(o_ref, _dot((_silu(hg) * hu).astype(BF16), wd_bf[...]))

    @pl.when(j >= n_valid)
    def _():
        o_ref[...] = jnp.zeros(o_ref.shape, jnp.uint32)


def moe_ffn_sorted(h, src_tok, tile_expert, n_valid, wg, wu, wd, layer, *, tm):
    n_tiles = src_tok.shape[0]
    d, ff = wg.shape[2], wg.shape[3]
    smem_rows = functools.partial(pl.BlockSpec, (None, 1, tm), memory_space=pltpu.SMEM)
    grid_spec = pltpu.PrefetchScalarGridSpec(
        num_scalar_prefetch=2,
        grid=(n_tiles,),
        in_specs=[
            smem_rows(lambda j, te, nv: (j, 0, 0)),
            smem_rows(lambda j, te, nv: (jnp.minimum(j + 1, n_tiles - 1), 0, 0)),
            pl.BlockSpec(memory_space=pl.ANY),
            pl.BlockSpec((None, None, d, ff), lambda j, te, nv: (layer, te[j], 0, 0)),
            pl.BlockSpec((None, None, d, ff), lambda j, te, nv: (layer, te[j], 0, 0)),
            pl.BlockSpec((None, None, ff, d), lambda j, te, nv: (layer, te[j], 0, 0)),
        ],
        out_specs=pl.BlockSpec((tm * ROW_PITCH, LANES), lambda j, te, nv: (j, 0)),
        scratch_shapes=[pltpu.VMEM((2, tm * ROW_PITCH, LANES), jnp.uint32), pltpu.SemaphoreType.DMA((2,)),
                        pltpu.VMEM((d, ff), BF16), pltpu.VMEM((d, ff), BF16), pltpu.VMEM((ff, d), BF16)],
    )
    return pl.pallas_call(
        _moe_ffn_kernel,
        grid_spec=grid_spec,
        out_shape=jax.ShapeDtypeStruct((n_tiles * tm * ROW_PITCH, LANES), jnp.uint32),
        compiler_params=_cparams(("arbitrary",), V7X_VMEM_LIMIT),
        name="moe_ffn_sorted",
    )(tile_expert, n_valid, src_tok, src_tok, h, wg, wu, wd)


def _moe_combine_kernel(pos_ref, pos_next_ref, w_ref, ys_hbm, o_ref, gbuf, sem):
    i = pl.program_id(0)
    n = pl.num_programs(0)
    slot = i % 2
    top_k = gbuf.shape[0] // 2
    tc = gbuf.shape[1] // ROW_PITCH
    n_data_rows = _token_data_rows(o_ref.shape[1])

    def copy(p_ref, s, k, r):
        src0 = pl.multiple_of(p_ref[k, r] * ROW_PITCH, 8)
        dst0 = pl.multiple_of(r * ROW_PITCH, 8)
        return pltpu.make_async_copy(ys_hbm.at[pl.ds(src0, n_data_rows), :],
                                     gbuf.at[s * top_k + k, pl.ds(dst0, n_data_rows), :], sem.at[s])

    def start(p_ref, s):
        for k in range(top_k):
            def body(r2, carry, k=k):
                for prio in range(2):
                    copy(p_ref, s, k, 2 * r2 + prio).start(priority=prio)
                return carry

            lax.fori_loop(0, tc // 2, body, 0, unroll=DMA_ISSUE_UNROLL // 2)

    @pl.when(i == 0)
    def _():
        start(pos_ref, 0)

    @pl.when(i + 1 < n)
    def _():
        start(pos_next_ref, 1 - slot)

    for _ in range(top_k * tc):
        pltpu.make_async_copy(ys_hbm.at[pl.ds(0, n_data_rows), :],
                              gbuf.at[slot * top_k, pl.ds(0, n_data_rows), :], sem.at[slot]).wait()
    w = w_ref[...]
    for c in range(n_data_rows):
        acc_lo = acc_hi = None
        for k in range(top_k):
            lo, hi = _unpack_bf16_pair(gbuf[slot * top_k + k, pl.ds(c, tc, stride=ROW_PITCH), :])
            wk = w[:, k:k + 1]
            acc_lo = wk * lo if acc_lo is None else acc_lo + wk * lo
            acc_hi = wk * hi if acc_hi is None else acc_hi + wk * hi
        o_ref[:, c * LANES:(c + 1) * LANES] = acc_lo
        o_ref[:, (n_data_rows + c) * LANES:(n_data_rows + c + 1) * LANES] = acc_hi


def moe_combine(ys, pos_tiles, w_tok, d, *, tc):
    n_tiles, top_k, _ = pos_tiles.shape
    smem_pos = functools.partial(pl.BlockSpec, (None, top_k, tc), memory_space=pltpu.SMEM)
    return pl.pallas_call(
        _moe_combine_kernel,
        grid=(n_tiles,),
        in_specs=[smem_pos(lambda i: (i, 0, 0)),
                  smem_pos(lambda i: (jnp.minimum(i + 1, n_tiles - 1), 0, 0)),
                  pl.BlockSpec((tc, top_k), lambda i: (i, 0)),
                  pl.BlockSpec(memory_space=pl.ANY)],
        out_specs=pl.BlockSpec((tc, d), lambda i: (i, 0)),
        out_shape=jax.ShapeDtypeStruct((n_tiles * tc, d), F32),
        scratch_shapes=[pltpu.VMEM((2 * top_k, tc * ROW_PITCH, LANES), jnp.uint32),
                        pltpu.SemaphoreType.DMA((2,))],
        compiler_params=_cparams(("arbitrary",), V7X_VMEM_LIMIT),
        name="moe_combine",
    )(pos_tiles, pos_tiles, w_tok, ys)


_HI = lax.Precision.HIGHEST


def _cmul(ar, ai, br, bi):
    return ar * br - ai * bi, ar * bi + ai * br


def s5_operators(a_re, a_im, log_dt, b_re, b_im, c_re, c_im):
    t = S5_T
    dt = jnp.exp(log_dt)[..., None]
    adt_re, adt_im = a_re * dt, a_im * dt
    tau = jnp.arange(t + 1, dtype=F32)[None, None, :, None]
    mag = jnp.exp(adt_re[:, :, None, :] * tau)
    ang = adt_im[:, :, None, :] * tau
    pw_re, pw_im = mag * jnp.cos(ang), mag * jnp.sin(ang)
    ab_re, ab_im = pw_re[:, :, 1], pw_im[:, :, 1]
    den = a_re * a_re + a_im * a_im
    nr, ni = ab_re - 1.0, ab_im
    f_re = (nr * a_re + ni * a_im) / den
    f_im = (ni * a_re - nr * a_im) / den
    bb_re, bb_im = _cmul(f_re[..., None], f_im[..., None], b_re, b_im)
    m1_re, m1_im = _cmul(pw_re[..., None], pw_im[..., None], bb_re[:, :, None], bb_im[:, :, None])
    kk = (jnp.einsum('dgpn,dgtnq->dgtpq', c_re, m1_re[:, :, :t], precision=_HI)
          - jnp.einsum('dgpn,dgtnq->dgtpq', c_im, m1_im[:, :, :t], precision=_HI))
    ti = jnp.arange(t)
    lag = ti[:, None] - ti[None, :]
    kg = kk[:, :, jnp.clip(lag, 0, t - 1)]
    kg = jnp.where((lag >= 0)[None, None, :, :, None, None], kg, 0.0)
    nd, g = a_re.shape[0], a_re.shape[1]
    p = b_re.shape[-1]
    n = a_re.shape[-1]
    inc = jnp.stack([m1_re[:, :, t - 1 - ti], m1_im[:, :, t - 1 - ti]], axis=3)
    w_re, w_im = _cmul(c_re[:, :, None], c_im[:, :, None],
                       pw_re[:, :, 1:, None, :], pw_im[:, :, 1:, None, :])
    mst = jnp.stack([w_re, -w_im], axis=2)

    def reverse_backward(v, axes):
        return jnp.stack([v[0], jnp.flip(v[1], axes)], axis=0)

    kg = reverse_backward(kg, (1, 2))
    inc = reverse_backward(inc, (1,))
    mst = reverse_backward(mst, (2,))
    gpb = S5_LANES // p
    gb = g // gpb
    fw = t * S5_LANES
    assert t * p == S5_LANES and 2 * n == S5_LANES
    toep_c = kg.reshape(nd, gb, gpb, t, t, p, p).transpose(0, 1, 4, 2, 6, 3, 5)
    minc_c = inc.reshape(nd, gb, gpb, t, 2, n, p).transpose(0, 1, 3, 2, 6, 4, 5)
    mstate_c = mst.reshape(nd, gb, gpb, 2, t, p, n).transpose(0, 1, 3, 2, 6, 4, 5)
    a_t_re = pw_re[:, :, t].reshape(nd, 1, g * n)
    a_t_im = pw_im[:, :, t].reshape(nd, 1, g * n)
    return (toep_c.reshape(nd, gb, fw, S5_LANES).astype(BF16), minc_c.reshape(nd, gb, fw, S5_LANES).astype(BF16),
            mstate_c.reshape(nd, gb, fw, S5_LANES).astype(BF16), a_t_re, a_t_im)


def rotary_tables(n_pos, half):
    freq = ROPE_BASE ** (-jnp.arange(half, dtype=F32) / half)
    ang = jnp.arange(n_pos, dtype=F32)[:, None] * freq[None, :]
    return jnp.cos(ang), jnp.sin(ang)


def kernel(x, c, ctx, c_ctx, emb_ln_g, emb_ln_b, ada_w, ada_b, w_in, conv_w, conv_b, conv_ln_g, conv_ln_b, conv_proj, ret_decay_logit, ret_proj, s5_a_re, s5_a_im, s5_log_dt, s5_b_re, s5_b_im, s5_c_re, s5_c_im, s5_d, s5_w_glu, s5_proj, w_out, ln1_g, ln1_b, ln2_g, ln2_b, router_w, router_bias, exp_w_gate, exp_w_up, exp_w_down, sh_w_gate, sh_w_up, sh_w_down):
    n_batch, seq, d = x.shape
    lc = ctx.shape[1]
    depth = w_in.shape[0]
    conv_c = conv_w.shape[2]
    ret_w = ret_proj.shape[1]
    s5_c = s5_d.shape[1]
    s5_g = s5_c // S5_P
    n_exp, _, exp_ff = exp_w_gate.shape[1:]
    sh_ff = sh_w_gate.shape[2]
    rows = seq // GRID_W
    assert lc == ROW_TILE and seq % ROW_TILE == 0 and n_batch + 1 <= MOD_ROWS
    assert RET_HEADS * RET_DK == ret_w and S5_T * S5_LANES == 2 * (S5_LANES // S5_P) * S5_N
    assert s5_c % S5_LANES == 0 and lc % S5_T == 0 and rows % S5_T == 0
    tiles_per_batch = seq // ROW_TILE
    n_lat_tiles = n_batch * tiles_per_batch
    n_tiles = n_lat_tiles + n_batch
    n_lat = n_batch * seq
    alpha = (2.0 * depth) ** 0.25
    ntok = n_tiles * ROW_TILE
    tm_mm = _largest_row_tile(ntok, 2)
    tm_ffn = _largest_row_tile(ntok, 2) if ntok % 544 else 544
    col_a, col_q = 0, 2 * conv_c
    col_g = col_q + 3 * ret_w
    col_u = col_g + ret_w
    col_s = col_u + s5_c

    group_fn = functools.partial(_group_of_tile, n_latent_tiles=n_lat_tiles,
                                 tiles_per_batch=tiles_per_batch, n_batch=n_batch)

    tokens = jnp.concatenate([x.reshape(n_lat, d), ctx.reshape(n_batch * lc, d)], axis=0)
    cvec = jnp.concatenate([c, c_ctx[None, :], jnp.zeros((MOD_ROWS - n_batch - 1, d), F32)], axis=0)
    mods = [ada_modulation(cvec, ada_w, ada_b[i][None, :], i) for i in range(depth)]
    cos_tab, sin_tab = rotary_tables(lc + seq, RET_DK // 2)
    log_g = jax.nn.log_sigmoid(ret_decay_logit.astype(F32))

    xl, h = resid_ln_mod(tokens, [], None, 0, emb_ln_g[None, :], emb_ln_b[None, :], mods[0], 0, 1,
                         alpha=1.0, group_fn=group_fn, n_tiles=n_tiles)

    for i in range(depth):
        last = i == depth - 1
        mod = mods[i]
        z = matmul_stacked_w(h, w_in, i, BF16, tm_mm, 1024)
        act_a = jnp.concatenate([
            conformer_conv_act(z, conv_w[i], conv_b[i][None, :], conv_ln_g[i][None, :], conv_ln_b[i][None, :],
                               row_tile0=0, n_tiles=n_lat_tiles, seg=GRID_W),
            conformer_conv_act(z, conv_w[i], conv_b[i][None, :], conv_ln_g[i][None, :], conv_ln_b[i][None, :],
                               row_tile0=n_lat_tiles, n_tiles=n_batch, seg=lc)], axis=0)
        o_f, o_b = retention_scan(z, log_g[i], cos_tab, sin_tab, n_batch=n_batch,
                                  tiles_per_batch=tiles_per_batch, q_col0=col_q // RET_DK)
        act_b = retention_post(o_f, o_b, z, g_col=col_g // ret_w)
        u = z[:, col_u:col_u + s5_c].astype(BF16)
        gb = s5_c // S5_LANES
        fw = S5_T * S5_LANES
        u_lat = u[:n_lat].reshape(n_batch, rows, GRID_W, gb, S5_LANES).transpose(0, 3, 2, 1, 4)
        u_ctx = u[n_lat:].reshape(n_batch, lc, gb, S5_LANES).transpose(0, 2, 1, 3)
        u_fold = jnp.concatenate([u_ctx.reshape(n_batch, gb, lc // S5_T, fw),
                                  u_lat.reshape(n_batch, gb, seq // S5_T, fw)], axis=2)
        ops = s5_operators(s5_a_re[i], s5_a_im[i], s5_log_dt[i], s5_b_re[i], s5_b_im[i], s5_c_re[i], s5_c_im[i])
        y_fold = s5_chunked(u_fold, *ops, n_ctx_chunk=lc // S5_T)
        y_ctx = y_fold[:, :, :lc // S5_T].reshape(n_batch, gb, lc, S5_LANES).transpose(0, 2, 1, 3)
        y_lat = y_fold[:, :, lc // S5_T:].reshape(n_batch, gb, GRID_W, rows, S5_LANES).transpose(0, 3, 2, 1, 4)
        y_s5 = jnp.concatenate([y_lat.reshape(n_lat, s5_c), y_ctx.reshape(n_batch * lc, s5_c)], axis=0)
        act_c = s5_post(y_s5, z, s5_d[i][None, :], s5_w_glu, i, u_col=col_u // s5_c)
        merged = merge_branches(act_a, act_b, act_c, z, conv_proj, ret_proj, s5_proj, i,
                                s_col0=col_s, tm=tm_mm, tn=512)
        y_mix = matmul_stacked_w(merged, w_out, i, F32, tm_mm, 512)
        xl, h2, h2_f32, eidx, wk = resid_ln_mod(
            xl, [y_mix], mod, 2, ln1_g[i][None, :], ln1_b[i][None, :], mod, 3, 4,
            alpha=alpha, group_fn=group_fn, n_tiles=n_tiles,
            router=(router_w[i].T, router_bias[i][:, None]))
        counts = jnp.sum((eidx[None, :, :] == jnp.arange(n_exp, dtype=jnp.int32)[:, None, None]).astype(jnp.int32),
                         axis=(1, 2))
        padded = ((counts + MOE_ROW_TILE - 1) // MOE_ROW_TILE) * MOE_ROW_TILE
        ends = jnp.cumsum(padded)
        n_sorted_tiles = (ntok * TOP_K) // MOE_ROW_TILE + n_exp
        tile_start = jnp.arange(n_sorted_tiles, dtype=jnp.int32) * MOE_ROW_TILE
        tile_expert = jnp.minimum(jnp.sum((ends[None, :] <= tile_start[:, None]).astype(jnp.int32), axis=1),
                                  n_exp - 1)
        n_valid = (ends[-1:] // MOE_ROW_TILE).astype(jnp.int32)
        pos = moe_positions(eidx, (ends - padded).astype(F32)[:, None])
        src_tok = jnp.zeros((n_sorted_tiles * MOE_ROW_TILE,), jnp.int32).at[pos.reshape(-1)].set(
            jnp.tile(jnp.arange(ntok, dtype=jnp.int32), TOP_K))
        y_sorted = moe_ffn_sorted(h2_f32, src_tok.reshape(n_sorted_tiles, 1, MOE_ROW_TILE), tile_expert, n_valid,
                                  exp_w_gate, exp_w_up, exp_w_down, i, tm=MOE_ROW_TILE)
        pos_tiles = pos.reshape(TOP_K, ntok // MOE_COMBINE_TILE, MOE_COMBINE_TILE).transpose(1, 0, 2)
        y_routed = moe_combine(y_sorted, pos_tiles, wk.T, d, tc=MOE_COMBINE_TILE)
        y_shared = ffn_blocks(h2, sh_w_gate, sh_w_up, sh_w_down, i, exp_ff, tm=tm_ffn)
        if last:
            (xl,) = resid_ln_mod(xl, [y_routed, y_shared], mod, 5, ln2_g[i][None, :], ln2_b[i][None, :],
                                 None, 0, 0, alpha=alpha, group_fn=group_fn, n_tiles=n_lat_tiles)
        else:
            xl, h = resid_ln_mod(xl, [y_routed, y_shared], mod, 5, ln2_g[i][None, :], ln2_b[i][None, :],
                                 mods[i + 1], 0, 1, alpha=alpha, group_fn=group_fn, n_tiles=n_tiles)
    return xl.reshape(n_batch, seq, d)
```

```python
import functools
import math

import jax
import jax.numpy as jnp
from jax import lax
from jax.experimental import pallas as pl
from jax.experimental.pallas import tpu as pltpu

F32 = jnp.float32
BF16 = jnp.bfloat16

GRID_W = 64
RET_HEADS = 8
RET_DK = 256
S5_P = 16
S5_N = 64
ROPE_BASE = 10000.0
N_GROUPS = 8
TOPK_GROUPS = 4
TOP_K = 8
ROUTED_SCALE = 2.5
LN_EPS = 1e-5
HEAD_NORM_EPS = 1e-5
NEG_BIG = -1e30
N_BRANCH = 3

ROW_TILE = 256
S5_LANES = 128
S5_T = 8
MOD_ROWS = 8
MOE_ROW_TILE = 256
MOE_COMBINE_TILE = 64
DMA_ISSUE_UNROLL = 8
LANES = 128
ROW_PITCH = 24
V7X_VMEM_LIMIT = 56 * 1024 * 1024


def _largest_row_tile(ntok, max_tiles):
    n = ntok // ROW_TILE
    k = max(t for t in range(1, max_tiles + 1) if n % t == 0)
    return k * ROW_TILE


def _cparams(sem, vmem=None):
    return pltpu.CompilerParams(dimension_semantics=sem, vmem_limit_bytes=vmem)


def _split_bf16(v):
    hi = v.astype(BF16)
    lo = (v - hi.astype(F32)).astype(BF16)
    return hi, lo


def _dot(a, b):
    return jnp.dot(a, b, preferred_element_type=F32)


def _dot_nt(a, b):
    return lax.dot_general(a, b, (((1,), (1,)), ((), ())), preferred_element_type=F32)


def _dot_tn(a, b):
    return lax.dot_general(a, b, (((0,), (0,)), ((), ())), preferred_element_type=F32)


def _sigmoid(v):
    return 1.0 / (1.0 + jnp.exp(-v))


def _silu(v):
    return v * _sigmoid(v)


_HIGH_HALF = 0xFFFF0000


def _pack_bf16_pair(lo, hi):
    lo_bits = pltpu.bitcast(lo.astype(BF16).astype(F32), jnp.uint32) >> 16
    hi_bits = pltpu.bitcast(hi.astype(BF16).astype(F32), jnp.uint32) & jnp.uint32(_HIGH_HALF)
    return lo_bits | hi_bits


def _unpack_bf16_pair(words):
    lo = pltpu.bitcast(words << 16, F32)
    hi = pltpu.bitcast(words & jnp.uint32(_HIGH_HALF), F32)
    return lo, hi


def _token_data_rows(d):
    rows = d // (2 * LANES)
    assert rows * 2 * LANES == d and rows <= ROW_PITCH
    return rows


def _store_packed(ref, v):
    n, d = v.shape
    k = _token_data_rows(d)
    for c in range(k):
        lo = v[:, c * LANES:(c + 1) * LANES]
        hi = v[:, (k + c) * LANES:(k + c + 1) * LANES]
        ref[pl.ds(c, n, stride=ROW_PITCH), :] = _pack_bf16_pair(lo, hi)
    for c in range(k, ROW_PITCH):
        ref[pl.ds(c, n, stride=ROW_PITCH), :] = jnp.zeros((n, LANES), jnp.uint32)


def _load_packed(ref, n, d):
    k = _token_data_rows(d)
    parts = [_unpack_bf16_pair(ref[pl.ds(c, n, stride=ROW_PITCH), :]) for c in range(k)]
    return jnp.concatenate([p[0] for p in parts] + [p[1] for p in parts], axis=1)


def _layer_norm_rows(v, g, b, eps):
    mu = jnp.mean(v, axis=-1, keepdims=True)
    vc = v - mu
    var = jnp.mean(vc * vc, axis=-1, keepdims=True)
    return vc * lax.rsqrt(var + eps) * g + b


def _ada_kernel(c_ref, w_ref, b_ref, o_ref):
    c = _silu(c_ref[...])
    ch, cl = _split_bf16(c)
    wh, wl = _split_bf16(w_ref[...])
    o_ref[...] = _dot(ch, wh) + _dot(ch, wl) + _dot(cl, wh) + b_ref[...]


def ada_modulation(cvec, ada_w, ada_b_l, layer):
    _, d, n = ada_w.shape
    tn = 512
    return pl.pallas_call(
        _ada_kernel,
        grid=(n // tn,),
        in_specs=[
            pl.BlockSpec((MOD_ROWS, d), lambda j: (0, 0)),
            pl.BlockSpec((None, d, tn), lambda j: (layer, 0, j)),
            pl.BlockSpec((1, tn), lambda j: (0, j)),
        ],
        out_specs=pl.BlockSpec((MOD_ROWS, tn), lambda j: (0, j)),
        out_shape=jax.ShapeDtypeStruct((MOD_ROWS, n), F32),
        compiler_params=_cparams(("arbitrary",), V7X_VMEM_LIMIT),
        name="ada_modulation",
    )(cvec, ada_w, ada_b_l)


def _group_of_tile(i, n_latent_tiles, tiles_per_batch, n_batch):
    return jnp.where(i < n_latent_tiles, i // tiles_per_batch, n_batch)


def _route(h, rwt_ref, rb_ref):
    hh, hl = _split_bf16(h)
    wh, wl = _split_bf16(rwt_ref[...])
    logits = _dot_nt(wh, hh) + _dot_nt(wh, hl) + _dot_nt(wl, hh)
    scores = _sigmoid(logits)
    sel = scores + rb_ref[...]
    n_e, tm = sel.shape
    per = n_e // N_GROUPS
    shape3 = (N_GROUPS, per, tm)
    sel3 = sel.reshape(shape3)
    io_e = lax.broadcasted_iota(jnp.int32, shape3, 1)
    io_g = lax.broadcasted_iota(jnp.int32, shape3, 0)
    m1 = jnp.max(sel3, axis=1, keepdims=True)
    first = jnp.min(jnp.where(sel3 == m1, io_e, per), axis=1, keepdims=True)
    m2 = jnp.max(jnp.where(io_e == first, -jnp.inf, sel3), axis=1, keepdims=True)
    work = m1 + m2
    iog1 = lax.broadcasted_iota(jnp.int32, work.shape, 0)
    gsel = jnp.zeros(work.shape, F32)
    for _ in range(TOPK_GROUPS):
        m = jnp.max(work, axis=0, keepdims=True)
        fi = jnp.min(jnp.where(work == m, iog1, N_GROUPS), axis=0, keepdims=True)
        hit = iog1 == fi
        gsel = jnp.where(hit, 1.0, gsel)
        work = jnp.where(hit, -jnp.inf, work)
    work = jnp.where(jnp.broadcast_to(gsel, shape3) > 0.0, sel3, NEG_BIG)
    flat = io_g * per + io_e
    scores3 = scores.reshape(shape3)
    picked, picked_score = [], []
    for _ in range(TOP_K):
        m = jnp.max(jnp.max(work, axis=1, keepdims=True), axis=0, keepdims=True)
        cand = jnp.where(work == m, flat, n_e)
        fi = jnp.min(jnp.min(cand, axis=1, keepdims=True), axis=0, keepdims=True)
        hit = flat == fi
        sk = jnp.sum(jnp.sum(jnp.where(hit, scores3, 0.0), axis=1, keepdims=True), axis=0, keepdims=True)
        picked.append(fi.reshape(1, tm))
        picked_score.append(sk.reshape(1, tm))
        work = jnp.where(hit, -jnp.inf, work)
    eidx = jnp.concatenate(picked, axis=0)
    w = jnp.concatenate(picked_score, axis=0)
    return eidx, ROUTED_SCALE * w / jnp.sum(w, axis=0, keepdims=True)


def _resid_ln_mod_kernel(*refs, alpha, n_y, has_mod, has_router, group_fn):
    it = iter(refs)
    x_ref = next(it)
    y_refs = [next(it) for _ in range(n_y)]
    gate_ref = next(it) if n_y else None
    g_ref, b_ref = next(it), next(it)
    shift_ref = scale_ref = rwt_ref = rb_ref = None
    if has_mod:
        shift_ref, scale_ref = next(it), next(it)
    if has_router:
        rwt_ref, rb_ref = next(it), next(it)
    xl_ref = next(it)
    h_ref = next(it) if has_mod else None
    hf_ref, eidx_ref, wk_ref = (next(it), next(it), next(it)) if has_router else (None, None, None)

    grp = group_fn(pl.program_id(0))
    v = x_ref[...]
    if n_y:
        y = y_refs[0][...]
        for r in y_refs[1:]:
            y = y + r[...]
        v = alpha * v + gate_ref[pl.ds(grp, 1), :] * y
    xl = _layer_norm_rows(v, g_ref[...], b_ref[...], LN_EPS)
    xl_ref[...] = xl
    if has_mod:
        h = xl * (1.0 + scale_ref[pl.ds(grp, 1), :]) + shift_ref[pl.ds(grp, 1), :]
        h_ref[...] = h.astype(h_ref.dtype)
        if has_router:
            _store_packed(hf_ref, h)
            eidx_ref[...], wk_ref[...] = _route(h, rwt_ref, rb_ref)


def resid_ln_mod(x, ys, gate_mod, gate_col, ln_g, ln_b, mod, shift_col, scale_col, *,
                 alpha, group_fn, n_tiles, router=None):
    d = x.shape[1]
    n_y = len(ys)
    has_mod = mod is not None
    has_router = router is not None
    row = pl.BlockSpec((ROW_TILE, d), lambda i: (i, 0))
    vec = pl.BlockSpec((1, d), lambda i: (0, 0))
    args, specs = [x], [row]
    for y in ys:
        args.append(y)
        specs.append(row)
    if n_y:
        args.append(gate_mod)
        specs.append(pl.BlockSpec((MOD_ROWS, d), lambda i: (0, gate_col)))
    args += [ln_g, ln_b]
    specs += [vec, vec]
    if has_mod:
        args += [mod, mod]
        specs += [pl.BlockSpec((MOD_ROWS, d), lambda i: (0, shift_col)),
                  pl.BlockSpec((MOD_ROWS, d), lambda i: (0, scale_col))]
    out_shapes = [jax.ShapeDtypeStruct((n_tiles * ROW_TILE, d), F32)]
    out_specs = [row]
    if has_mod:
        out_shapes.append(jax.ShapeDtypeStruct((n_tiles * ROW_TILE, d), BF16))
        out_specs.append(row)
    if has_router:
        rwt, rb = router
        n_e = rwt.shape[0]
        args += [rwt, rb]
        specs += [pl.BlockSpec((n_e, d), lambda i: (0, 0)), pl.BlockSpec((n_e, 1), lambda i: (0, 0))]
        out_shapes += [jax.ShapeDtypeStruct((n_tiles * ROW_TILE * ROW_PITCH, LANES), jnp.uint32),
                       jax.ShapeDtypeStruct((TOP_K, n_tiles * ROW_TILE), jnp.int32),
                       jax.ShapeDtypeStruct((TOP_K, n_tiles * ROW_TILE), F32)]
        out_specs += [pl.BlockSpec((ROW_TILE * ROW_PITCH, LANES), lambda i: (i, 0)),
                      pl.BlockSpec((TOP_K, ROW_TILE), lambda i: (0, i)),
                      pl.BlockSpec((TOP_K, ROW_TILE), lambda i: (0, i))]
    kern = functools.partial(_resid_ln_mod_kernel, alpha=alpha, n_y=n_y, has_mod=has_mod,
                             has_router=has_router, group_fn=group_fn)
    return pl.pallas_call(
        kern, grid=(n_tiles,), in_specs=specs, out_specs=out_specs, out_shape=out_shapes,
        compiler_params=_cparams(("arbitrary",), V7X_VMEM_LIMIT),
        name="resid_ln_mod",
    )(*args)


def _mm_kernel(x_ref, w_ref, o_ref, wbf_ref):
    @pl.when(pl.program_id(1) == 0)
    def _():
        wbf_ref[...] = w_ref[...].astype(BF16)

    o_ref[...] = _dot(x_ref[...], wbf_ref[...]).astype(o_ref.dtype)


def matmul_stacked_w(x, w, layer, out_dtype, tm, tn):
    m, k = x.shape
    n = w.shape[2]
    return pl.pallas_call(
        _mm_kernel,
        grid=(n // tn, m // tm),
        in_specs=[
            pl.BlockSpec((tm, k), lambda j, i: (i, 0)),
            pl.BlockSpec((None, k, tn), lambda j, i: (layer, 0, j)),
        ],
        out_specs=pl.BlockSpec((tm, tn), lambda j, i: (i, j)),
        out_shape=jax.ShapeDtypeStruct((m, n), out_dtype),
        scratch_shapes=[pltpu.VMEM((k, tn), BF16)],
        compiler_params=_cparams(("arbitrary", "arbitrary"), V7X_VMEM_LIMIT),
        name="matmul",
    )(x, w)


def _conv_kernel(a1_ref, a2_ref, w_ref, cb_ref, g_ref, b_ref, o_ref, pad_ref, y_ref, *, seg, n_tap):
    half = n_tap // 2
    front = ((half + 7) // 8) * 8
    nseg = ROW_TILE // seg
    c = a1_ref.shape[1]
    u = a1_ref[...].astype(F32) * _sigmoid(a2_ref[...].astype(F32))
    pad_ref[...] = jnp.zeros(pad_ref.shape, F32)
    for s in range(nseg):
        pad_ref[s, front:front + seg, :] = u[s * seg:(s + 1) * seg, :]
    lanes = 128

    def chunk(ci, carry):
        c0 = pl.multiple_of(ci * lanes, lanes)
        acc = jnp.zeros((nseg, seg, lanes), F32)
        for k in range(n_tap):
            off = front - half + k
            acc = acc + w_ref[k:k + 1, pl.ds(c0, lanes)] * pad_ref[:, off:off + seg, pl.ds(c0, lanes)]
        y_ref[:, pl.ds(c0, lanes)] = acc.reshape(ROW_TILE, lanes)
        return carry

    lax.fori_loop(0, c // lanes, chunk, 0)
    y = y_ref[...] + cb_ref[...]
    o_ref[...] = _silu(_layer_norm_rows(y, g_ref[...], b_ref[...], LN_EPS)).astype(o_ref.dtype)


def conformer_conv_act(z, conv_w_l, conv_b_l, ln_g_l, ln_b_l, *, row_tile0, n_tiles, seg):
    n_tap, c = conv_w_l.shape
    half = n_tap // 2
    front = ((half + 7) // 8) * 8
    nseg = ROW_TILE // seg
    vec = pl.BlockSpec((1, c), lambda i: (0, 0))
    kern = functools.partial(_conv_kernel, seg=seg, n_tap=n_tap)
    return pl.pallas_call(
        kern,
        grid=(n_tiles,),
        in_specs=[
            pl.BlockSpec((ROW_TILE, c), lambda i: (row_tile0 + i, 0)),
            pl.BlockSpec((ROW_TILE, c), lambda i: (row_tile0 + i, 1)),
            pl.BlockSpec((n_tap, c), lambda i: (0, 0)),
            vec, vec, vec,
        ],
        out_specs=pl.BlockSpec((ROW_TILE, c), lambda i: (i, 0)),
        out_shape=jax.ShapeDtypeStruct((n_tiles * ROW_TILE, c), BF16),
        scratch_shapes=[pltpu.VMEM((nseg, seg + 2 * front, c), F32), pltpu.VMEM((ROW_TILE, c), F32)],
        compiler_params=_cparams(("arbitrary",), V7X_VMEM_LIMIT),
        name="conformer_conv",
    )(z, z, conv_w_l, conv_b_l, ln_g_l, ln_b_l)


def _rotary(t, cos, sin):
    half = t.shape[1] // 2
    t1, t2 = t[:, :half], t[:, half:]
    return jnp.concatenate([t1 * cos - t2 * sin, t1 * sin + t2 * cos], axis=1)


def _retention_kernel(logg_ref, qf_ref, kf_ref, vf_ref, cf_ref, sf_ref,
                      qb_ref, kb_ref, vb_ref, cb_ref, sb_ref,
                      of_ref, ob_ref, state_ref):
    h = pl.program_id(1)
    n = pl.program_id(2)
    c = ROW_TILE
    k_scale = RET_DK ** -0.5

    @pl.when(n == 0)
    def _():
        state_ref[...] = jnp.zeros(state_ref.shape, F32)

    row = lax.broadcasted_iota(jnp.int32, (c, c), 0)
    col = lax.broadcasted_iota(jnp.int32, (c, c), 1)
    ridx = lax.broadcasted_iota(jnp.int32, (c, 1), 0).astype(F32)

    def one_direction(d, q_ref, k_ref, v_ref, cos_ref, sin_ref, o_ref):
        lg = logg_ref[d, h]
        cos, sin = cos_ref[...], sin_ref[...]
        q = _rotary(q_ref[...].astype(F32), cos, sin)
        k = _rotary(k_ref[...].astype(F32), cos, sin) * k_scale
        v = v_ref[...].astype(BF16)
        dist = (row - col) if d == 0 else (col - row)
        decay = jnp.where(dist >= 0, jnp.exp(lg * jnp.maximum(dist, 0).astype(F32)), 0.0)
        qb = q.astype(BF16)
        scores = _dot_nt(qb, k.astype(BF16)) * decay
        inner = _dot(scores.astype(BF16), v)
        to_prev = (ridx + 1.0) if d == 0 else (c - ridx)
        to_end = (c - 1.0 - ridx) if d == 0 else ridx
        s_prev = state_ref[d]
        cross = _dot(qb, s_prev.astype(BF16)) * jnp.exp(lg * to_prev)
        o_ref[...] = inner + cross
        kw = (k * jnp.exp(lg * to_end)).astype(BF16)
        state_ref[d] = jnp.exp(lg * jnp.full((1, 1), float(c), F32)) * s_prev + _dot_tn(kw, v)

    one_direction(0, qf_ref, kf_ref, vf_ref, cf_ref, sf_ref, of_ref)
    one_direction(1, qb_ref, kb_ref, vb_ref, cb_ref, sb_ref, ob_ref)


def retention_scan(z, log_g, cos_tab, sin_tab, *, n_batch, tiles_per_batch, q_col0):
    ntok = z.shape[0]
    n_lat = n_batch * tiles_per_batch
    n_steps = tiles_per_batch + 1
    dk = RET_DK
    hh = RET_HEADS

    def row_f(b, n):
        return jnp.where(n == 0, n_lat + b, b * tiles_per_batch + n - 1)

    def row_b(b, n):
        return jnp.where(n == 0, n_lat + b, b * tiles_per_batch + tiles_per_batch - n)

    def pos_f(n):
        return n

    def pos_b(n):
        return jnp.where(n == 0, 0, tiles_per_batch + 1 - n)

    def zspec(rowfn, sec):
        return pl.BlockSpec((ROW_TILE, dk), lambda b, h, n: (rowfn(b, n), q_col0 + sec * hh + h))

    def tspec(posfn):
        return pl.BlockSpec((ROW_TILE, dk // 2), lambda b, h, n: (posfn(n), 0))

    def ospec(rowfn):
        return pl.BlockSpec((ROW_TILE, dk), lambda b, h, n: (rowfn(b, n), h))

    smem = pl.BlockSpec(memory_space=pltpu.SMEM)
    return pl.pallas_call(
        _retention_kernel,
        grid=(n_batch, hh, n_steps),
        in_specs=[smem,
                  zspec(row_f, 0), zspec(row_f, 1), zspec(row_f, 2), tspec(pos_f), tspec(pos_f),
                  zspec(row_b, 0), zspec(row_b, 1), zspec(row_b, 2), tspec(pos_b), tspec(pos_b)],
        out_specs=[ospec(row_f), ospec(row_b)],
        out_shape=[jax.ShapeDtypeStruct((ntok, hh * dk), F32)] * 2,
        scratch_shapes=[pltpu.VMEM((2, dk, dk), F32)],
        compiler_params=_cparams(("arbitrary", "arbitrary", "arbitrary"), V7X_VMEM_LIMIT),
        name="retention_scan",
    )(log_g, z, z, z, cos_tab, sin_tab, z, z, z, cos_tab, sin_tab)


def _ret_post_kernel(of_ref, ob_ref, g_ref, o_ref):
    dk = RET_DK
    for h in range(RET_HEADS):
        sl = slice(h * dk, (h + 1) * dk)
        o = of_ref[:, sl] + ob_ref[:, sl]
        mu = jnp.mean(o, axis=-1, keepdims=True)
        oc = o - mu
        var = jnp.mean(oc * oc, axis=-1, keepdims=True)
        on = oc * lax.rsqrt(var + HEAD_NORM_EPS)
        o_ref[:, sl] = (on * _silu(g_ref[:, sl].astype(F32))).astype(o_ref.dtype)


def retention_post(o_f, o_b, z, *, g_col):
    ntok, w = o_f.shape
    row = pl.BlockSpec((ROW_TILE, w), lambda i: (i, 0))
    return pl.pallas_call(
        _ret_post_kernel,
        grid=(ntok // ROW_TILE,),
        in_specs=[row, row, pl.BlockSpec((ROW_TILE, w), lambda i: (i, g_col))],
        out_specs=row,
        out_shape=jax.ShapeDtypeStruct((ntok, w), BF16),
        compiler_params=_cparams(("arbitrary",), V7X_VMEM_LIMIT),
        name="retention_post",
    )(o_f, o_b, z)


def _expand_block_diag(compact, w, rows_per_group):
    gpb = S5_LANES // S5_P
    n_rows, k = compact.shape
    n_cols = k * gpb
    lw, lg, lr = w.bit_length() - 1, gpb.bit_length() - 1, rows_per_group.bit_length() - 1
    i = lax.broadcasted_iota(jnp.int32, (k, n_cols), 0)
    c = lax.broadcasted_iota(jnp.int32, (k, n_cols), 1)
    src = ((c >> (lw + lg)) << lw) + (c & (w - 1))
    rep = jnp.where(i == src, 1.0, 0.0).astype(BF16)
    full = _dot(compact, rep)
    r = lax.broadcasted_iota(jnp.int32, (n_rows, n_cols), 0)
    c2 = lax.broadcasted_iota(jnp.int32, (n_rows, n_cols), 1)
    keep = ((r >> lr) & (gpb - 1)) == ((c2 >> lw) & (gpb - 1))
    return jnp.where(keep, full, 0.0).astype(BF16)


def _s5_local_kernel(u_ref, kc_ref, ic_ref, yl_ref, xre_ref, xim_ref, toep_ref, minc_ref):
    @pl.when(pl.program_id(2) == 0)
    def _():
        toep_ref[...] = _expand_block_diag(kc_ref[...], S5_P, S5_P)
        minc_ref[...] = _expand_block_diag(ic_ref[...], S5_N, S5_P)

    u = u_ref[...]
    yl_ref[...] = _dot(u, toep_ref[...])
    xi = _dot(u, minc_ref[...])
    half = xi.shape[1] // 2
    xre_ref[...] = xi[:, :half]
    xim_ref[...] = xi[:, half:]


def _s5_scan_kernel(xre_ref, xim_ref, are_ref, aim_ref, ore_ref, oim_ref, *, n_ctx_chunk):
    n_chunk = xre_ref.shape[0]
    backward = pl.program_id(1) == 1
    ar, ai = are_ref[...], aim_ref[...]

    def step(i, carry):
        sr, si = carry
        rev = jnp.where(i < n_ctx_chunk, n_ctx_chunk - 1 - i, n_chunk + n_ctx_chunk - 1 - i)
        c = jnp.where(backward, rev, i)
        ore_ref[pl.ds(c, 1), :] = sr
        oim_ref[pl.ds(c, 1), :] = si
        nr = ar * sr - ai * si + xre_ref[pl.ds(c, 1), :]
        ni = ar * si + ai * sr + xim_ref[pl.ds(c, 1), :]
        return nr, ni

    zero = jnp.zeros(ar.shape, F32)
    lax.fori_loop(0, n_chunk, step, (zero, zero))


def _s5_state_kernel(xre_ref, xim_ref, mc_ref, yl_ref, y_ref, mst_ref):
    @pl.when(pl.program_id(1) == 0)
    def _():
        for d in range(2):
            mst_ref[d] = _expand_block_diag(mc_ref[d], S5_P, S5_N)

    y = yl_ref[0] + yl_ref[1]
    for d in range(2):
        x0 = jnp.concatenate([xre_ref[d], xim_ref[d]], axis=1).astype(BF16)
        y = y + _dot(x0, mst_ref[d])
    y_ref[...] = y


def s5_chunked(u_fold, toep, minc, mstate, a_re, a_im, *, n_ctx_chunk):
    nb, gb, nch, fw = u_fold.shape
    sw = fw // 2
    op = pl.BlockSpec((None, None, fw, S5_LANES), lambda g, d, b: (d, g, 0, 0))
    yl, xre, xim = pl.pallas_call(
        _s5_local_kernel,
        grid=(gb, 2, nb),
        in_specs=[pl.BlockSpec((None, None, nch, fw), lambda g, d, b: (b, g, 0, 0)), op, op],
        out_specs=[pl.BlockSpec((None, None, None, nch, fw), lambda g, d, b: (b, d, g, 0, 0)),
                   pl.BlockSpec((None, None, nch, sw), lambda g, d, b: (b, d, 0, g)),
                   pl.BlockSpec((None, None, nch, sw), lambda g, d, b: (b, d, 0, g))],
        out_shape=[jax.ShapeDtypeStruct((nb, 2, gb, nch, fw), F32),
                   jax.ShapeDtypeStruct((nb, 2, nch, gb * sw), F32),
                   jax.ShapeDtypeStruct((nb, 2, nch, gb * sw), F32)],
        scratch_shapes=[pltpu.VMEM((fw, fw), BF16), pltpu.VMEM((fw, fw), BF16)],
        compiler_params=_cparams(("arbitrary", "arbitrary", "arbitrary"), V7X_VMEM_LIMIT),
        name="s5_local",
    )(u_fold, toep, minc)
    scan_w = 2 * sw
    full = pl.BlockSpec((None, None, nch, scan_w), lambda b, d, j: (b, d, 0, j))
    avec = pl.BlockSpec((None, 1, scan_w), lambda b, d, j: (d, 0, j))
    x0re, x0im = pl.pallas_call(
        functools.partial(_s5_scan_kernel, n_ctx_chunk=n_ctx_chunk),
        grid=(nb, 2, gb * sw // scan_w),
        in_specs=[full, full, avec, avec],
        out_specs=[full, full],
        out_shape=[jax.ShapeDtypeStruct((nb, 2, nch, gb * sw), F32)] * 2,
        compiler_params=_cparams(("arbitrary", "arbitrary", "arbitrary"), V7X_VMEM_LIMIT),
        name="s5_scan",
    )(xre, xim, a_re, a_im)
    xcol = pl.BlockSpec((None, 2, nch, sw), lambda g, b: (b, 0, 0, g))
    return pl.pallas_call(
        _s5_state_kernel,
        grid=(gb, nb),
        in_specs=[xcol, xcol,
                  pl.BlockSpec((2, None, fw, S5_LANES), lambda g, b: (0, g, 0, 0)),
                  pl.BlockSpec((None, 2, None, nch, fw), lambda g, b: (b, 0, g, 0, 0))],
        out_specs=pl.BlockSpec((None, None, nch, fw), lambda g, b: (b, g, 0, 0)),
        out_shape=jax.ShapeDtypeStruct((nb, gb, nch, fw), F32),
        scratch_shapes=[pltpu.VMEM((2, fw, fw), BF16)],
        compiler_params=_cparams(("arbitrary", "arbitrary"), V7X_VMEM_LIMIT),
        name="s5_state",
    )(x0re, x0im, mstate, yl)


def _gelu_tanh(v):
    return 0.5 * v * (1.0 + jnp.tanh(math.sqrt(2.0 / math.pi) * (v + 0.044715 * v * v * v)))


def _s5_post_kernel(y_ref, u_ref, d_ref, w_ref, o_ref, wbf_ref):
    @pl.when(pl.program_id(0) == 0)
    def _():
        wbf_ref[...] = w_ref[...].astype(BF16)

    t = _gelu_tanh(y_ref[...] + d_ref[...] * u_ref[...].astype(F32))
    o_ref[...] = (t * _sigmoid(_dot(t.astype(BF16), wbf_ref[...]))).astype(o_ref.dtype)


def s5_post(y, z, s5_d_l, w_glu, layer, *, u_col):
    ntok, c = y.shape
    row = pl.BlockSpec((ROW_TILE, c), lambda i: (i, 0))
    return pl.pallas_call(
        _s5_post_kernel,
        grid=(ntok // ROW_TILE,),
        in_specs=[row, pl.BlockSpec((ROW_TILE, c), lambda i: (i, u_col)),
                  pl.BlockSpec((1, c), lambda i: (0, 0)),
                  pl.BlockSpec((None, c, c), lambda i: (layer, 0, 0))],
        out_specs=row,
        out_shape=jax.ShapeDtypeStruct((ntok, c), BF16),
        scratch_shapes=[pltpu.VMEM((c, c), BF16)],
        compiler_params=_cparams(("arbitrary",), V7X_VMEM_LIMIT),
        name="s5_post",
    )(y, z, s5_d_l, w_glu)


def _merge_kernel(a_ref, b_ref, c_ref, s0_ref, s1_ref, s2_ref, wa_ref, wb_ref, wc_ref, o_ref,
                  wa_bf, wb_bf, wc_bf):
    @pl.when(pl.program_id(1) == 0)
    def _():
        wa_bf[...] = wa_ref[...].astype(BF16)
        wb_bf[...] = wb_ref[...].astype(BF16)
        wc_bf[...] = wc_ref[...].astype(BF16)

    m = _sigmoid(s0_ref[...].astype(F32)) * _dot(a_ref[...], wa_bf[...])
    m = m + _sigmoid(s1_ref[...].astype(F32)) * _dot(b_ref[...], wb_bf[...])
    m = m + _sigmoid(s2_ref[...].astype(F32)) * _dot(c_ref[...], wc_bf[...])
    o_ref[...] = m.astype(o_ref.dtype)


def merge_branches(act_a, act_b, act_c, z, conv_proj, ret_proj, s5_proj, layer, *, s_col0, tm, tn):
    m = act_a.shape[0]
    d = conv_proj.shape[2]
    ka, kb, kc = act_a.shape[1], act_b.shape[1], act_c.shape[1]
    nblk = d // tn

    def aspec(k):
        return pl.BlockSpec((tm, k), lambda j, i: (i, 0))

    def sspec(br):
        return pl.BlockSpec((tm, tn), lambda j, i: (i, s_col0 // tn + br * nblk + j))

    def wspec(k):
        return pl.BlockSpec((None, k, tn), lambda j, i: (layer, 0, j))

    return pl.pallas_call(
        _merge_kernel,
        grid=(nblk, m // tm),
        in_specs=[aspec(ka), aspec(kb), aspec(kc), sspec(0), sspec(1), sspec(2),
                  wspec(ka), wspec(kb), wspec(kc)],
        out_specs=pl.BlockSpec((tm, tn), lambda j, i: (i, j)),
        out_shape=jax.ShapeDtypeStruct((m, d), BF16),
        scratch_shapes=[pltpu.VMEM((ka, tn), BF16), pltpu.VMEM((kb, tn), BF16), pltpu.VMEM((kc, tn), BF16)],
        compiler_params=_cparams(("arbitrary", "arbitrary"), V7X_VMEM_LIMIT),
        name="merge_branches",
    )(act_a, act_b, act_c, z, z, z, conv_proj, ret_proj, s5_proj)


def _ffn_kernel(x_ref, wg_ref, wu_ref, wd_ref, o_ref):
    @pl.when(pl.program_id(1) == 0)
    def _():
        o_ref[...] = jnp.zeros(o_ref.shape, F32)

    x = x_ref[...]
    hg = _dot(x, wg_ref[...].astype(BF16))
    hu = _dot(x, wu_ref[...].astype(BF16))
    o_ref[...] += _dot((_silu(hg) * hu).astype(BF16), wd_ref[...].astype(BF16))


def ffn_blocks(x, wg, wu, wd, layer, ff_block, *, tm):
    m, d = x.shape
    ff = wg.shape[2]
    up = pl.BlockSpec((None, d, ff_block), lambda i, e: (layer, 0, e))
    return pl.pallas_call(
        _ffn_kernel,
        grid=(m // tm, ff // ff_block),
        in_specs=[pl.BlockSpec((tm, d), lambda i, e: (i, 0)), up, up,
                  pl.BlockSpec((None, ff_block, d), lambda i, e: (layer, e, 0))],
        out_specs=pl.BlockSpec((tm, d), lambda i, e: (i, 0)),
        out_shape=jax.ShapeDtypeStruct((m, d), F32),
        compiler_params=_cparams(("arbitrary", "arbitrary"), V7X_VMEM_LIMIT),
        name="ffn_blocks",
    )(x, wg, wu, wd)


def _moe_positions_kernel(eidx_ref, off_ref, pos_ref, run_ref):
    @pl.when(pl.program_id(0) == 0)
    def _():
        run_ref[...] = jnp.zeros(run_ref.shape, F32)

    n_e = off_ref.shape[0]
    top_k, tm = eidx_ref.shape
    expert = lax.broadcasted_iota(jnp.int32, (n_e, tm), 0)
    eidx = eidx_ref[...]
    member = jnp.zeros((n_e, tm), F32)
    for k in range(top_k):
        member = member + jnp.where(eidx[k:k + 1, :] == expert, 1.0, 0.0)
    r = lax.broadcasted_iota(jnp.int32, (tm, tm), 0)
    c = lax.broadcasted_iota(jnp.int32, (tm, tm), 1)
    upper = jnp.where(r <= c, 1.0, 0.0).astype(BF16)
    incl = _dot(member.astype(BF16), upper)
    row_of = off_ref[...] + run_ref[...] + incl - member
    rows = [jnp.sum(jnp.where(eidx[k:k + 1, :] == expert, row_of, 0.0), axis=0, keepdims=True)
            for k in range(top_k)]
    pos_ref[...] = jnp.concatenate(rows, axis=0).astype(jnp.int32)
    run_ref[...] = run_ref[...] + jnp.sum(member, axis=1, keepdims=True)


def moe_positions(eidx, offsets):
    top_k, ntok = eidx.shape
    n_e = offsets.shape[0]
    blk = pl.BlockSpec((top_k, ROW_TILE), lambda i: (0, i))
    return pl.pallas_call(
        _moe_positions_kernel,
        grid=(ntok // ROW_TILE,),
        in_specs=[blk, pl.BlockSpec((n_e, 1), lambda i: (0, 0))],
        out_specs=blk,
        out_shape=jax.ShapeDtypeStruct((top_k, ntok), jnp.int32),
        scratch_shapes=[pltpu.VMEM((n_e, 1), F32)],
        compiler_params=_cparams(("arbitrary",), V7X_VMEM_LIMIT),
        name="moe_positions",
    )(eidx, offsets)


def _token_copy(src_hbm, src_tok, buf, slot, dst_tok, sem, n_data_rows):
    src0 = pl.multiple_of(src_tok * ROW_PITCH, 8)
    dst0 = pl.multiple_of(dst_tok * ROW_PITCH, 8)
    return pltpu.make_async_copy(src_hbm.at[pl.ds(src0, n_data_rows), :],
                                 buf.at[slot, pl.ds(dst0, n_data_rows), :], sem.at[slot])


def _gather_rows_start(idx_ref, src_hbm, buf, slot, sem, n_tok, n_data_rows):
    def body(r, carry):
        _token_copy(src_hbm, idx_ref[0, r], buf, slot, r, sem, n_data_rows).start()
        return carry

    lax.fori_loop(0, n_tok, body, 0, unroll=DMA_ISSUE_UNROLL)


def _gather_rows_wait(src_hbm, buf, slot, sem, n_tok, n_data_rows):
    for _ in range(n_tok):
        _token_copy(src_hbm, 0, buf, slot, 0, sem, n_data_rows).wait()


def _moe_ffn_kernel(te_ref, nv_ref, idx_ref, idx_next_ref, h_hbm, wg_ref, wu_ref, wd_ref, o_ref,
                    xbuf, sem, wg_bf, wu_bf, wd_bf):
    j = pl.program_id(0)
    n_valid = nv_ref[0]
    slot = j % 2
    tm = xbuf.shape[1] // ROW_PITCH
    d = wg_ref.shape[0]
    k = _token_data_rows(d)

    @pl.when(j == 0)
    def _():
        _gather_rows_start(idx_ref, h_hbm, xbuf, 0, sem, tm, k)

    @pl.when(jnp.logical_or(j == 0, te_ref[j] != te_ref[jnp.maximum(j - 1, 0)]))
    def _():
        wg_bf[...] = wg_ref[...].astype(BF16)
        wu_bf[...] = wu_ref[...].astype(BF16)
        wd_bf[...] = wd_ref[...].astype(BF16)

    def compute(prefetch_next):
        _gather_rows_wait(h_hbm, xbuf, slot, sem, tm, k)
        if prefetch_next:
            for r in range(tm):
                _token_copy(h_hbm, idx_next_ref[0, r], xbuf, 1 - slot, r, sem, k).start()
        x = _load_packed(xbuf.at[slot], tm, d).astype(BF16)
        hg = _dot(x, wg_bf[...])
        hu = _dot(x, wu_bf[...])
        _store_packed(o_ref, _dot((_silu(hg) * hu).astype(BF16), wd_bf[...]))

    @pl.when(j + 1 < n_valid)
    def _():
        compute(True)

    @pl.when(j + 1 == n_valid)
    def _():
        compute(False)

    @pl.when(j >= n_valid)
    def _():
        o_ref[...] = jnp.zeros(o_ref.shape, jnp.uint32)


def moe_ffn_sorted(h, src_tok, tile_expert, n_valid, wg, wu, wd, layer, *, tm):
    n_tiles = src_tok.shape[0]
    d, ff = wg.shape[2], wg.shape[3]
    smem_rows = functools.partial(pl.BlockSpec, (None, 1, tm), memory_space=pltpu.SMEM)
    grid_spec = pltpu.PrefetchScalarGridSpec(
        num_scalar_prefetch=2,
        grid=(n_tiles,),
        in_specs=[
            smem_rows(lambda j, te, nv: (j, 0, 0)),
            smem_rows(lambda j, te, nv: (jnp.minimum(j + 1, n_tiles - 1), 0, 0)),
            pl.BlockSpec(memory_space=pl.ANY),
            pl.BlockSpec((None, None, d, ff), lambda j, te, nv: (layer, te[j], 0, 0)),
            pl.BlockSpec((None, None, d, ff), lambda j, te, nv: (layer, te[j], 0, 0)),
            pl.BlockSpec((None, None, ff, d), lambda j, te, nv: (layer, te[j], 0, 0)),
        ],
        out_specs=pl.BlockSpec((tm * ROW_PITCH, LANES), lambda j, te, nv: (j, 0)),
        scratch_shapes=[pltpu.VMEM((2, tm * ROW_PITCH, LANES), jnp.uint32), pltpu.SemaphoreType.DMA((2,)),
                        pltpu.VMEM((d, ff), BF16), pltpu.VMEM((d, ff), BF16), pltpu.VMEM((ff, d), BF16)],
    )
    return pl.pallas_call(
        _moe_ffn_kernel,
        grid_spec=grid_spec,
        out_shape=jax.ShapeDtypeStruct((n_tiles * tm * ROW_PITCH, LANES), jnp.uint32),
        compiler_params=_cparams(("arbitrary",), V7X_VMEM_LIMIT),
        name="moe_ffn_sorted",
    )(tile_expert, n_valid, src_tok, src_tok, h, wg, wu, wd)


def _moe_combine_kernel(pos_ref, pos_next_ref, w_ref, ys_hbm, o_ref, gbuf, sem):
    i = pl.program_id(0)
    n = pl.num_programs(0)
    slot = i % 2
    top_k = gbuf.shape[0] // 2
    tc = gbuf.shape[1] // ROW_PITCH
    n_data_rows = _token_data_rows(o_ref.shape[1])

    def copy(p_ref, s, k, r):
        src0 = pl.multiple_of(p_ref[k, r] * ROW_PITCH, 8)
        dst0 = pl.multiple_of(r * ROW_PITCH, 8)
        return pltpu.make_async_copy(ys_hbm.at[pl.ds(src0, n_data_rows), :],
                                     gbuf.at[s * top_k + k, pl.ds(dst0, n_data_rows), :], sem.at[s])

    def start(p_ref, s):
        for k in range(top_k):
            def body(r2, carry, k=k):
                for prio in range(2):
                    copy(p_ref, s, k, 2 * r2 + prio).start(priority=prio)
                return carry

            lax.fori_loop(0, tc // 2, body, 0, unroll=DMA_ISSUE_UNROLL // 2)

    @pl.when(i == 0)
    def _():
        start(pos_ref, 0)

    @pl.when(i + 1 < n)
    def _():
        start(pos_next_ref, 1 - slot)

    for _ in range(top_k * tc):
        pltpu.make_async_copy(ys_hbm.at[pl.ds(0, n_data_rows), :],
                              gbuf.at[slot * top_k, pl.ds(0, n_data_rows), :], sem.at[slot]).wait()
    w = w_ref[...]
    for c in range(n_data_rows):
        acc_lo = acc_hi = None
        for k in range(top_k):
            lo, hi = _unpack_bf16_pair(gbuf[slot * top_k + k, pl.ds(c, tc, stride=ROW_PITCH), :])
            wk = w[:, k:k + 1]
            acc_lo = wk * lo if acc_lo is None else acc_lo + wk * lo
            acc_hi = wk * hi if acc_hi is None else acc_hi + wk * hi
        o_ref[:, c * LANES:(c + 1) * LANES] = acc_lo
        o_ref[:, (n_data_rows + c) * LANES:(n_data_rows + c + 1) * LANES] = acc_hi


def moe_combine(ys, pos_tiles, w_tok, d, *, tc):
    n_tiles, top_k, _ = pos_tiles.shape
    smem_pos = functools.partial(pl.BlockSpec, (None, top_k, tc), memory_space=pltpu.SMEM)
    return pl.pallas_call(
        _moe_combine_kernel,
        grid=(n_tiles,),
        in_specs=[smem_pos(lambda i: (i, 0, 0)),
                  smem_pos(lambda i: (jnp.minimum(i + 1, n_tiles - 1), 0, 0)),
                  pl.BlockSpec((tc, top_k), lambda i: (i, 0)),
                  pl.BlockSpec(memory_space=pl.ANY)],
        out_specs=pl.BlockSpec((tc, d), lambda i: (i, 0)),
        out_shape=jax.ShapeDtypeStruct((n_tiles * tc, d), F32),
        scratch_shapes=[pltpu.VMEM((2 * top_k, tc * ROW_PITCH, LANES), jnp.uint32),
                        pltpu.SemaphoreType.DMA((2,))],
        compiler_params=_cparams(("arbitrary",), V7X_VMEM_LIMIT),
        name="moe_combine",
    )(pos_tiles, pos_tiles, w_tok, ys)


_HI = lax.Precision.HIGHEST


def _cmul(ar, ai, br, bi):
    return ar * br - ai * bi, ar * bi + ai * br


def s5_operators(a_re, a_im, log_dt, b_re, b_im, c_re, c_im):
    t = S5_T
    dt = jnp.exp(log_dt)[..., None]
    adt_re, adt_im = a_re * dt, a_im * dt
    tau = jnp.arange(t + 1, dtype=F32)[None, None, :, None]
    mag = jnp.exp(adt_re[:, :, None, :] * tau)
    ang = adt_im[:, :, None, :] * tau
    pw_re, pw_im = mag * jnp.cos(ang), mag * jnp.sin(ang)
    ab_re, ab_im = pw_re[:, :, 1], pw_im[:, :, 1]
    den = a_re * a_re + a_im * a_im
    nr, ni = ab_re - 1.0, ab_im
    f_re = (nr * a_re + ni * a_im) / den
    f_im = (ni * a_re - nr * a_im) / den
    bb_re, bb_im = _cmul(f_re[..., None], f_im[..., None], b_re, b_im)
    m1_re, m1_im = _cmul(pw_re[..., None], pw_im[..., None], bb_re[:, :, None], bb_im[:, :, None])
    kk = (jnp.einsum('dgpn,dgtnq->dgtpq', c_re, m1_re[:, :, :t], precision=_HI)
          - jnp.einsum('dgpn,dgtnq->dgtpq', c_im, m1_im[:, :, :t], precision=_HI))
    ti = jnp.arange(t)
    lag = ti[:, None] - ti[None, :]
    kg = kk[:, :, jnp.clip(lag, 0, t - 1)]
    kg = jnp.where((lag >= 0)[None, None, :, :, None, None], kg, 0.0)
    nd, g = a_re.shape[0], a_re.shape[1]
    p = b_re.shape[-1]
    n = a_re.shape[-1]
    inc = jnp.stack([m1_re[:, :, t - 1 - ti], m1_im[:, :, t - 1 - ti]], axis=3)
    w_re, w_im = _cmul(c_re[:, :, None], c_im[:, :, None],
                       pw_re[:, :, 1:, None, :], pw_im[:, :, 1:, None, :])
    mst = jnp.stack([w_re, -w_im], axis=2)

    def reverse_backward(v, axes):
        return jnp.stack([v[0], jnp.flip(v[1], axes)], axis=0)

    kg = reverse_backward(kg, (1, 2))
    inc = reverse_backward(inc, (1,))
    mst = reverse_backward(mst, (2,))
    gpb = S5_LANES // p
    gb = g // gpb
    fw = t * S5_LANES
    assert t * p == S5_LANES and 2 * n == S5_LANES
    toep_c = kg.reshape(nd, gb, gpb, t, t, p, p).transpose(0, 1, 4, 2, 6, 3, 5)
    minc_c = inc.reshape(nd, gb, gpb, t, 2, n, p).transpose(0, 1, 3, 2, 6, 4, 5)
    mstate_c = mst.reshape(nd, gb, gpb, 2, t, p, n).transpose(0, 1, 3, 2, 6, 4, 5)
    a_t_re = pw_re[:, :, t].reshape(nd, 1, g * n)
    a_t_im = pw_im[:, :, t].reshape(nd, 1, g * n)
    return (toep_c.reshape(nd, gb, fw, S5_LANES).astype(BF16), minc_c.reshape(nd, gb, fw, S5_LANES).astype(BF16),
            mstate_c.reshape(nd, gb, fw, S5_LANES).astype(BF16), a_t_re, a_t_im)


def rotary_tables(n_pos, half):
    freq = ROPE_BASE ** (-jnp.arange(half, dtype=F32) / half)
    ang = jnp.arange(n_pos, dtype=F32)[:, None] * freq[None, :]
    return jnp.cos(ang), jnp.sin(ang)


def kernel(x, c, ctx, c_ctx, emb_ln_g, emb_ln_b, ada_w, ada_b, w_in, conv_w, conv_b, conv_ln_g, conv_ln_b, conv_proj, ret_decay_logit, ret_proj, s5_a_re, s5_a_im, s5_log_dt, s5_b_re, s5_b_im, s5_c_re, s5_c_im, s5_d, s5_w_glu, s5_proj, w_out, ln1_g, ln1_b, ln2_g, ln2_b, router_w, router_bias, exp_w_gate, exp_w_up, exp_w_down, sh_w_gate, sh_w_up, sh_w_down):
    n_batch, seq, d = x.shape
    lc = ctx.shape[1]
    depth = w_in.shape[0]
    conv_c = conv_w.shape[2]
    ret_w = ret_proj.shape[1]
    s5_c = s5_d.shape[1]
    s5_g = s5_c // S5_P
    n_exp, _, exp_ff = exp_w_gate.shape[1:]
    sh_ff = sh_w_gate.shape[2]
    rows = seq // GRID_W
    assert lc == ROW_TILE and seq % ROW_TILE == 0 and n_batch + 1 <= MOD_ROWS
    assert RET_HEADS * RET_DK == ret_w and S5_T * S5_LANES == 2 * (S5_LANES // S5_P) * S5_N
    assert s5_c % S5_LANES == 0 and lc % S5_T == 0 and rows % S5_T == 0
    tiles_per_batch = seq // ROW_TILE
    n_lat_tiles = n_batch * tiles_per_batch
    n_tiles = n_lat_tiles + n_batch
    n_lat = n_batch * seq
    alpha = (2.0 * depth) ** 0.25
    ntok = n_tiles * ROW_TILE
    tm_mm = _largest_row_tile(ntok, 2)
    tm_ffn = _largest_row_tile(ntok, 2) if ntok % 544 else 544
    col_a, col_q = 0, 2 * conv_c
    col_g = col_q + 3 * ret_w
    col_u = col_g + ret_w
    col_s = col_u + s5_c

    group_fn = functools.partial(_group_of_tile, n_latent_tiles=n_lat_tiles,
                                 tiles_per_batch=tiles_per_batch, n_batch=n_batch)

    tokens = jnp.concatenate([x.reshape(n_lat, d), ctx.reshape(n_batch * lc, d)], axis=0)
    cvec = jnp.concatenate([c, c_ctx[None, :], jnp.zeros((MOD_ROWS - n_batch - 1, d), F32)], axis=0)
    mods = [ada_modulation(cvec, ada_w, ada_b[i][None, :], i) for i in range(depth)]
    cos_tab, sin_tab = rotary_tables(lc + seq, RET_DK // 2)
    log_g = jax.nn.log_sigmoid(ret_decay_logit.astype(F32))

    xl, h = resid_ln_mod(tokens, [], None, 0, emb_ln_g[None, :], emb_ln_b[None, :], mods[0], 0, 1,
                         alpha=1.0, group_fn=group_fn, n_tiles=n_tiles)

    for i in range(depth):
        last = i == depth - 1
        mod = mods[i]
        z = matmul_stacked_w(h, w_in, i, BF16, tm_mm, 1024)
        act_a = jnp.concatenate([
            conformer_conv_act(z, conv_w[i], conv_b[i][None, :], conv_ln_g[i][None, :], conv_ln_b[i][None, :],
                               row_tile0=0, n_tiles=n_lat_tiles, seg=GRID_W),
            conformer_conv_act(z, conv_w[i], conv_b[i][None, :], conv_ln_g[i][None, :], conv_ln_b[i][None, :],
                               row_tile0=n_lat_tiles, n_tiles=n_batch, seg=lc)], axis=0)
        o_f, o_b = retention_scan(z, log_g[i], cos_tab, sin_tab, n_batch=n_batch,
                                  tiles_per_batch=tiles_per_batch, q_col0=col_q // RET_DK)
        act_b = retention_post(o_f, o_b, z, g_col=col_g // ret_w)
        u = z[:, col_u:col_u + s5_c].astype(BF16)
        gb = s5_c // S5_LANES
        fw = S5_T * S5_LANES
        u_lat = u[:n_lat].reshape(n_batch, rows, GRID_W, gb, S5_LANES).transpose(0, 3, 2, 1, 4)
        u_ctx = u[n_lat:].reshape(n_batch, lc, gb, S5_LANES).transpose(0, 2, 1, 3)
        u_fold = jnp.concatenate([u_ctx.reshape(n_batch, gb, lc // S5_T, fw),
                                  u_lat.reshape(n_batch, gb, seq // S5_T, fw)], axis=2)
        ops = s5_operators(s5_a_re[i], s5_a_im[i], s5_log_dt[i], s5_b_re[i], s5_b_im[i], s5_c_re[i], s5_c_im[i])
        y_fold = s5_chunked(u_fold, *ops, n_ctx_chunk=lc // S5_T)
        y_ctx = y_fold[:, :, :lc // S5_T].reshape(n_batch, gb, lc, S5_LANES).transpose(0, 2, 1, 3)
        y_lat = y_fold[:, :, lc // S5_T:].reshape(n_batch, gb, GRID_W, rows, S5_LANES).transpose(0, 3, 2, 1, 4)
        y_s5 = jnp.concatenate([y_lat.reshape(n_lat, s5_c), y_ctx.reshape(n_batch * lc, s5_c)], axis=0)
        act_c = s5_post(y_s5, z, s5_d[i][None, :], s5_w_glu, i, u_col=col_u // s5_c)
        merged = merge_branches(act_a, act_b, act_c, z, conv_proj, ret_proj, s5_proj, i,
                                s_col0=col_s, tm=tm_mm, tn=512)
        y_mix = matmul_stacked_w(merged, w_out, i, F32, tm_mm, 512)
        xl, h2, h2_f32, eidx, wk = resid_ln_mod(
            xl, [y_mix], mod, 2, ln1_g[i][None, :], ln1_b[i][None, :], mod, 3, 4,
            alpha=alpha, group_fn=group_fn, n_tiles=n_tiles,
            router=(router_w[i].T, router_bias[i][:, None]))
        counts = jnp.sum((eidx[None, :, :] == jnp.arange(n_exp, dtype=jnp.int32)[:, None, None]).astype(jnp.int32),
                         axis=(1, 2))
        padded = ((counts + MOE_ROW_TILE - 1) // MOE_ROW_TILE) * MOE_ROW_TILE
        ends = jnp.cumsum(padded)
        n_sorted_tiles = (ntok * TOP_K) // MOE_ROW_TILE + n_exp
        tile_start = jnp.arange(n_sorted_tiles, dtype=jnp.int32) * MOE_ROW_TILE
        tile_expert = jnp.minimum(jnp.sum((ends[None, :] <= tile_start[:, None]).astype(jnp.int32), axis=1),
                                  n_exp - 1)
        n_valid = (ends[-1:] // MOE_ROW_TILE).astype(jnp.int32)
        pos = moe_positions(eidx, (ends - padded).astype(F32)[:, None])
        src_tok = jnp.zeros((n_sorted_tiles * MOE_ROW_TILE,), jnp.int32).at[pos.reshape(-1)].set(
            jnp.tile(jnp.arange(ntok, dtype=jnp.int32), TOP_K))
        y_sorted = moe_ffn_sorted(h2_f32, src_tok.reshape(n_sorted_tiles, 1, MOE_ROW_TILE), tile_expert, n_valid,
                                  exp_w_gate, exp_w_up, exp_w_down, i, tm=MOE_ROW_TILE)
        pos_tiles = pos.reshape(TOP_K, ntok // MOE_COMBINE_TILE, MOE_COMBINE_TILE).transpose(1, 0, 2)
        y_routed = moe_combine(y_sorted, pos_tiles, wk.T, d, tc=MOE_COMBINE_TILE)
        y_shared = ffn_blocks(h2, sh_w_gate, sh_w_up, sh_w_down, i, exp_ff, tm=tm_ffn)
        if last:
            (xl,) = resid_ln_mod(xl, [y_routed, y_shared], mod, 5, ln2_g[i][None, :], ln2_b[i][None, :],
                                 None, 0, 0, alpha=alpha, group_fn=group_fn, n_tiles=n_lat_tiles)
        else:
            xl, h = resid_ln_mod(xl, [y_routed, y_shared], mod, 5, ln2_g[i][None, :], ln2_b[i][None, :],
                                 mods[i + 1], 0, 1, alpha=alpha, group_fn=group_fn, n_tiles=n_tiles)
    return xl.reshape(n_batch, seq, d)
```
